```python
import math, functools
import jax, jax.numpy as jnp
from jax import lax
import numpy as np


D_MODEL = 1024
BATCH = 16
SEQ = 256
DEPTH = 2
DEC_BATCH = 8
DEC_SEQ = 2048
PAST_LEN = 256

GRID_W = 64
H_A = 4
DH_A = 64
H_B = 8
KV_B = 2
G_B = H_B // KV_B
DH_B = 64
WINDOW = 128
WIN_BLOCK = 128
C_CONV = 512
CONV_K = 31
BRANCH_W = 512
D_FF = 2816
Q_BLOCK = 128
ROPE_BASE = 10000.0
NORM_EPS = 1e-6
SUBLN_EPS = 1e-5
COL_SIZES = (H_A * 2 * DH_A, H_A * 2 * DH_A, H_A * 2 * DH_A,
             H_B * DH_B, KV_B * DH_B, KV_B * DH_B,
             2 * C_CONV, D_MODEL, D_MODEL, D_MODEL)
D_IN = sum(COL_SIZES)
SCALE_A = DH_A ** -0.5
SCALE_B = DH_B ** -0.5

kernel_name = 'hybrid_diffusion_prefix_step'


def rmsnorm(x, g, eps=NORM_EPS):
    xf = x.astype(jnp.float32)
    y = xf * lax.rsqrt(jnp.mean(xf * xf, axis=-1, keepdims=True) + eps)
    return (y * g.astype(jnp.float32)).astype(x.dtype)


def layernorm(x, g, b, eps=NORM_EPS):
    xf = x.astype(jnp.float32)
    mu = jnp.mean(xf, axis=-1, keepdims=True)
    var = jnp.mean(jnp.square(xf - mu), axis=-1, keepdims=True)
    y = (xf - mu) * lax.rsqrt(var + eps) * g.astype(jnp.float32) + b.astype(jnp.float32)
    return y.astype(x.dtype)


def modulated_rmsnorm(x, g, shift, scale):
    return rmsnorm(x, g) * (1 + scale) + shift


def swiglu(x, w_in, w_out):
    a, b = jnp.split(x @ w_in, 2, axis=-1)
    return (jax.nn.silu(a) * b) @ w_out


def split_columns(p):
    idx = np.cumsum(COL_SIZES)[:-1].tolist()
    return jnp.split(p, idx, axis=-1)


def axial_rope_table(n_tokens, dim, dtype):
    n_rows = n_tokens // GRID_W
    rows, cols = jnp.meshgrid(jnp.arange(n_rows, dtype=jnp.float32),
                              jnp.arange(GRID_W, dtype=jnp.float32), indexing='ij')
    inv = ROPE_BASE ** (-jnp.arange(0, dim // 2, 2, dtype=jnp.float32) / (dim // 2))
    ang = jnp.stack([rows.reshape(-1, 1) * inv, cols.reshape(-1, 1) * inv], axis=1)
    ang = jnp.broadcast_to(ang[:, :, None, :], (n_tokens, 2, 2, dim // 4)).reshape(n_tokens, dim)
    return jnp.cos(ang).astype(dtype), jnp.sin(ang).astype(dtype)


def apply_rope(x, cos, sin):
    dim = x.shape[-1]
    x4 = x.reshape(x.shape[:-1] + (2, 2, dim // 4))
    rot = jnp.stack([-x4[..., 1, :], x4[..., 0, :]], axis=-2).reshape(x.shape)
    return x * cos[:, None, :] + rot * sin[:, None, :]


def map_query_blocks(fn, q):
    B, T = q.shape[:2]
    nb = T // Q_BLOCK
    qb = jnp.moveaxis(q.reshape((B, nb, Q_BLOCK) + q.shape[2:]), 1, 0)
    out = lax.map(fn, qb)
    return jnp.moveaxis(out, 0, 1).reshape((B, T) + out.shape[3:])


def diff_attention(q, k, v, lam_full, lam_init, subln_g):
    B, T = q.shape[:2]

    def blk(qn):
        s = jnp.einsum('bqhcd,bshcd->bhcqs', qn, k).astype(jnp.float32) * SCALE_A
        p = jax.nn.softmax(s, axis=-1)
        a = (p[:, :, 0] - lam_full * p[:, :, 1]).astype(v.dtype)
        return jnp.einsum('bhqs,bshe->bqhe', a, v)

    o = map_query_blocks(blk, q)
    o = rmsnorm(o, subln_g, SUBLN_EPS) * (1.0 - lam_init)
    return o.reshape(B, T, H_A * 2 * DH_A)


def sink_attention_dense(q, k, v, sink):
    B, L = q.shape[:2]
    sink_col = sink.astype(jnp.float32).reshape(1, KV_B, G_B, 1, 1)

    def blk(qn):
        s = jnp.einsum('bqkgd,bskd->bkgqs', qn, k).astype(jnp.float32) * SCALE_B
        s = jnp.concatenate([s, jnp.broadcast_to(sink_col, s.shape[:-1] + (1,))], axis=-1)
        p = jax.nn.softmax(s, axis=-1)[..., :-1].astype(v.dtype)
        return jnp.einsum('bkgqs,bskd->bqkgd', p, v)

    o = map_query_blocks(blk, q)
    return o.reshape(B, L, H_B * DH_B)


def window_sink_attention(q, k, v, ck, cv, sink):
    B, T = q.shape[:2]
    nb = T // WIN_BLOCK
    pad = [(0, 0), (WIN_BLOCK, WIN_BLOCK), (0, 0), (0, 0)]
    kp = jnp.pad(k, pad)
    vp = jnp.pad(v, pad)
    qb = jnp.moveaxis(q.reshape(B, nb, WIN_BLOCK, KV_B, G_B, DH_B), 1, 0)
    sink_col = jnp.broadcast_to(sink.astype(jnp.float32).reshape(1, KV_B, G_B, 1, 1),
                                (B, KV_B, G_B, WIN_BLOCK, 1))
    offs_q = jnp.arange(WIN_BLOCK)
    offs_k = jnp.arange(3 * WIN_BLOCK) - WIN_BLOCK
    n_loc = 3 * WIN_BLOCK
    n_ctx = ck.shape[1]

    def blk(args):
        qn, n = args
        start = n * WIN_BLOCK
        kn = lax.dynamic_slice_in_dim(kp, start, n_loc, axis=1)
        vn = lax.dynamic_slice_in_dim(vp, start, n_loc, axis=1)
        qpos = start + offs_q
        kpos = start + offs_k
        valid = ((jnp.abs(kpos[None, :] - qpos[:, None]) <= WINDOW)
                 & (kpos >= 0)[None, :] & (kpos < T)[None, :])
        s_loc = jnp.einsum('bqkgd,bskd->bkgqs', qn, kn).astype(jnp.float32) * SCALE_B
        s_loc = jnp.where(valid, s_loc, -jnp.inf)
        s_ctx = jnp.einsum('bqkgd,bskd->bkgqs', qn, ck).astype(jnp.float32) * SCALE_B
        p = jax.nn.softmax(jnp.concatenate([s_loc, s_ctx, sink_col], axis=-1), axis=-1).astype(v.dtype)
        return (jnp.einsum('bkgqs,bskd->bqkgd', p[..., :n_loc], vn)
                + jnp.einsum('bkgqs,bskd->bqkgd', p[..., n_loc:n_loc + n_ctx], cv))

    o = lax.map(blk, (qb, jnp.arange(nb)))
    return jnp.moveaxis(o, 0, 1).reshape(B, T, H_B * DH_B)


def conv_module(u, w_dw, b_dw, g, b):
    a, gate = jnp.split(u, 2, axis=-1)
    u = a * jax.nn.sigmoid(gate)
    pad = CONV_K // 2
    y = lax.conv_general_dilated(u, w_dw.astype(u.dtype)[:, None, :], window_strides=(1,),
                                 padding=[(pad, pad)], dimension_numbers=('NWC', 'WIO', 'NWC'),
                                 feature_group_count=C_CONV) + b_dw
    return jax.nn.silu(layernorm(y, g, b))


def mix_context(parts, lam_full, lam_init, subln_g, sink, conv_params):
    qa, ka, va, qb, kb, vb, cu = parts
    B, L = qa.shape[:2]
    qa = qa.reshape(B, L, H_A, 2, DH_A)
    ka = ka.reshape(B, L, H_A, 2, DH_A)
    va = va.reshape(B, L, H_A, 2 * DH_A)
    kb = kb.reshape(B, L, KV_B, DH_B)
    vb = vb.reshape(B, L, KV_B, DH_B)
    ya = diff_attention(qa, ka, va, lam_full, lam_init, subln_g)
    yb = sink_attention_dense(qb.reshape(B, L, KV_B, G_B, DH_B), kb, vb, sink)
    yc = conv_module(cu, *conv_params)
    return ya, yb, yc, (ka, va, kb, vb)


def mix_latent(parts, ctx_dk, ctx_dv, ctx_wk, ctx_wv, cos, sin, lam_full, lam_init, subln_g, sink,
               conv_params):
    qa, ka, va, qb, kb, vb, cu = parts
    B, T = qa.shape[:2]
    qa = apply_rope(qa.reshape(B, T, 2 * H_A, DH_A), cos, sin).reshape(B, T, H_A, 2, DH_A)
    ka = apply_rope(ka.reshape(B, T, 2 * H_A, DH_A), cos, sin).reshape(B, T, H_A, 2, DH_A)
    va = va.reshape(B, T, H_A, 2 * DH_A)
    ya = diff_attention(qa, jnp.concatenate([ctx_dk, ka], axis=1),
                        jnp.concatenate([ctx_dv, va], axis=1), lam_full, lam_init, subln_g)
    qb = apply_rope(qb.reshape(B, T, H_B, DH_B), cos, sin).reshape(B, T, KV_B, G_B, DH_B)
    kb = apply_rope(kb.reshape(B, T, KV_B, DH_B), cos, sin)
    vb = vb.reshape(B, T, KV_B, DH_B)
    yb = window_sink_attention(qb, kb, vb, ctx_wk, ctx_wv, sink)
    yc = conv_module(cu, *conv_params)
    return ya, yb, yc, None


def run_layer(x, cond, w_mod, b_mod, norm_g, ffn_w_in, ffn_w_out, w_in, w_branch, w_out, mix):
    mod = cond @ w_mod + b_mod
    sh1, sc1, g1, sh2, sc2, g2, sh3, sc3, g3 = [m[:, None, :] for m in jnp.split(mod, 9, axis=-1)]
    h = x + 0.5 * g1 * swiglu(modulated_rmsnorm(x, norm_g[0], sh1, sc1), ffn_w_in[0], ffn_w_out[0])
    parts = split_columns(modulated_rmsnorm(h, norm_g[1], sh2, sc2) @ w_in)
    ya, yb, yc, ctx_tensors = mix(parts[:7])
    ga, gb, gc = parts[7:]
    merged = (jax.nn.sigmoid(ga) * (ya @ w_branch[0]) + jax.nn.sigmoid(gb) * (yb @ w_branch[1])
              + jax.nn.sigmoid(gc) * (yc @ w_branch[2]))
    h = h + g2 * (merged @ w_out)
    h = h + 0.5 * g3 * swiglu(modulated_rmsnorm(h, norm_g[2], sh3, sc3), ffn_w_in[1], ffn_w_out[1])
    return h, ctx_tensors


def setup_inputs(seed: int = 0) -> dict:
    key = jax.random.key(seed)
    ks = list(jax.random.split(key, 32))

    def nrm(i, shape, s):
        return jax.random.normal(ks[i], shape, jnp.float32) * s

    D = D_MODEL
    return {
        'x_prompt': nrm(0, (BATCH, SEQ, D), 1.0),
        'x_sample': nrm(1, (DEC_BATCH, DEC_SEQ, D), 1.0),
        'cache_diff_k': nrm(2, (DEC_BATCH, DEPTH, PAST_LEN, H_A, 2, DH_A), 1.0),
        'cache_diff_v': nrm(3, (DEC_BATCH, DEPTH, PAST_LEN, H_A, 2 * DH_A), 1.0),
        'cache_win_k': nrm(4, (DEC_BATCH, DEPTH, PAST_LEN, KV_B, DH_B), 1.0),
        'cache_win_v': nrm(5, (DEC_BATCH, DEPTH, PAST_LEN, KV_B, DH_B), 1.0),
        'c': nrm(6, (DEC_BATCH, D), 1.0),
        'c_ctx': nrm(7, (D,), 1.0),
        'w_mod': nrm(8, (DEPTH, D, 9 * D), 0.5 * D ** -0.5),
        'b_mod': nrm(9, (DEPTH, 9 * D), 0.01),
        'norm_g': 1.0 + nrm(10, (DEPTH, 3, D), 0.02),
        'ffn_w_in': nrm(11, (DEPTH, 2, D, 2 * D_FF), D ** -0.5),
        'ffn_w_out': nrm(12, (DEPTH, 2, D_FF, D), D_FF ** -0.5),
        'w_in': nrm(13, (DEPTH, D, D_IN), D ** -0.5),
        'diff_lambda': nrm(14, (DEPTH, 4, DH_A), 0.1),
        'diff_subln_g': 1.0 + nrm(15, (DEPTH, 2 * DH_A), 0.02),
        'win_sink': nrm(16, (DEPTH, H_B), 0.5),
        'conv_dw_w': nrm(17, (DEPTH, CONV_K, C_CONV), CONV_K ** -0.5),
        'conv_dw_b': nrm(18, (DEPTH, C_CONV), 0.01),
        'conv_norm_g': 1.0 + nrm(19, (DEPTH, C_CONV), 0.02),
        'conv_norm_b': nrm(20, (DEPTH, C_CONV), 0.01),
        'w_branch': nrm(21, (DEPTH, 3, BRANCH_W, D), BRANCH_W ** -0.5),
        'w_out': nrm(22, (DEPTH, D, D), D ** -0.5),
        'final_norm_g': 1.0 + nrm(23, (D,), 0.02),
    }


def reference(x_prompt, x_sample, cache_diff_k, cache_diff_v, cache_win_k, cache_win_v, c, c_ctx,
              w_mod, b_mod, norm_g, ffn_w_in, ffn_w_out, w_in, diff_lambda, diff_subln_g, win_sink,
              conv_dw_w, conv_dw_b, conv_norm_g, conv_norm_b, w_branch, w_out, final_norm_g):
    T = x_sample.shape[1]
    cos, sin = axial_rope_table(T, DH_A, x_sample.dtype)
    cond_ctx = jax.nn.silu(c_ctx)[None, :]
    cond_lat = jax.nn.silu(c)
    xp, xs = x_prompt, x_sample
    dks, dvs, wks, wvs = [], [], [], []
    for l in range(DEPTH):
        lam = diff_lambda[l].astype(jnp.float32)
        lam_init = 0.8 - 0.6 * math.exp(-0.3 * l)
        lam_full = jnp.exp(jnp.sum(lam[0] * lam[1])) - jnp.exp(jnp.sum(lam[2] * lam[3])) + lam_init
        conv_p = (conv_dw_w[l], conv_dw_b[l], conv_norm_g[l], conv_norm_b[l])
        shared = (w_mod[l], b_mod[l], norm_g[l], ffn_w_in[l], ffn_w_out[l], w_in[l], w_branch[l], w_out[l])
        ctx_mix = functools.partial(mix_context, lam_full=lam_full, lam_init=lam_init,
                                    subln_g=diff_subln_g[l], sink=win_sink[l], conv_params=conv_p)
        xp, (dk, dv, wk, wv) = run_layer(xp, cond_ctx, *shared, ctx_mix)
        dks.append(dk)
        dvs.append(dv)
        wks.append(wk)
        wvs.append(wv)
        lat_mix = functools.partial(mix_latent, ctx_dk=cache_diff_k[:, l], ctx_dv=cache_diff_v[:, l],
                                    ctx_wk=cache_win_k[:, l], ctx_wv=cache_win_v[:, l], cos=cos, sin=sin,
                                    lam_full=lam_full, lam_init=lam_init, subln_g=diff_subln_g[l],
                                    sink=win_sink[l], conv_params=conv_p)
        xs, _ = run_layer(xs, cond_lat, *shared, lat_mix)
    y_prompt = rmsnorm(xp, final_norm_g)
    y_sample = rmsnorm(xs, final_norm_g)
    new_diff_k = jnp.stack(dks, axis=1)
    new_diff_v = jnp.stack(dvs, axis=1)
    new_win_k = jnp.stack(wks, axis=1)
    new_win_v = jnp.stack(wvs, axis=1)
    return (y_prompt, y_sample, new_diff_k, new_diff_v, new_win_k, new_win_v)
```

```python
import functools
import math

import numpy as np
import jax
import jax.numpy as jnp
from jax import lax
from jax.experimental import pallas as pl
from jax.experimental.pallas import tpu as pltpu

D_MODEL = 1024
DEPTH = 2
GRID_W = 64
H_A = 4
DH_A = 64
H_B = 8
KV_B = 2
G_B = H_B // KV_B
DH_B = 64
WINDOW = 128
WIN_BLOCK = 128
C_CONV = 512
CONV_K = 31
BRANCH_W = 512
D_FF = 2816
ROPE_BASE = 10000.0
NORM_EPS = 1e-6
SUBLN_EPS = 1e-5
SCALE_A = DH_A ** -0.5
SCALE_B = DH_B ** -0.5

N_QA = H_A * 2 * DH_A
N_KVB = KV_B * DH_B
N_ATTN = 3 * N_QA + H_B * DH_B + 2 * N_KVB
N_PROJ = N_ATTN + 2 * C_CONV

LANES = 128
ROPE_HALF = DH_A // 4
FF_CHUNK = 256
CONV_HALO = 16
VMEM_LIMIT = 56 * 1024 * 1024

BF16 = jnp.bfloat16
F32 = jnp.float32


def _params(*sem):
    return pltpu.CompilerParams(dimension_semantics=sem, vmem_limit_bytes=VMEM_LIMIT)


def _resident(shape):
    nd = len(shape)
    return pl.BlockSpec(shape, lambda *_: (0,) * nd, pipeline_mode=pl.Buffered(1))


def _dot(a, b):
    return jnp.dot(a, b, preferred_element_type=F32)


def _dot_nt(a, b):
    return lax.dot_general(a, b, (((1,), (1,)), ((), ())), preferred_element_type=F32)


def _rms(x, eps):
    return x * lax.rsqrt(jnp.mean(x * x, axis=-1, keepdims=True) + eps)


def _mod_rms(x, g, shift, scale):
    return _rms(x, NORM_EPS) * g * (1.0 + scale) + shift


def _silu(x):
    return x * jax.nn.sigmoid(x)


def _mod_kernel(c_ref, w_ref, b_ref, o_ref):
    a = _silu(c_ref[...]).astype(BF16)
    o_ref[...] = _dot(a, w_ref[...].astype(BF16)) + b_ref[...]


def _modulation(cond, w_mod, b_mod):
    R = cond.shape[0]
    tn = D_MODEL
    return pl.pallas_call(
        _mod_kernel,
        grid=(DEPTH, 9 * D_MODEL // tn),
        in_specs=[
            pl.BlockSpec((R, D_MODEL), lambda l, j: (0, 0)),
            pl.BlockSpec((None, D_MODEL, tn), lambda l, j: (l, 0, j)),
            pl.BlockSpec((None, 1, tn), lambda l, j: (l, 0, j)),
        ],
        out_specs=pl.BlockSpec((None, R, tn), lambda l, j: (l, 0, j)),
        out_shape=jax.ShapeDtypeStruct((DEPTH, R, 9 * D_MODEL), F32),
        compiler_params=_params("parallel", "parallel"),
        name="modulation",
    )(cond, w_mod, b_mod)


def _ffn_kernel(x_ref, mod_ref, g_ref, win_ref, wout_ref, *rest, mod_row, g_row, final):
    if final:
        fg_ref, o_ref = rest
    else:
        (o_ref,) = rest
    x = x_ref[...]
    shift = mod_ref[mod_row:mod_row + 1, :]
    scale = mod_ref[mod_row + 1:mod_row + 2, :]
    gate = mod_ref[mod_row + 2:mod_row + 3, :]
    nb = _mod_rms(x, g_ref[g_row:g_row + 1, :], shift, scale).astype(BF16)
    n_chunks, _, two_ck = win_ref.shape
    ck = two_ck // 2

    def body(c, acc):
        ab = _dot(nb, win_ref[c])
        hid = (_silu(ab[:, :ck]) * ab[:, ck:]).astype(BF16)
        return acc + _dot(hid, wout_ref[c])

    acc = lax.fori_loop(0, n_chunks, body, jnp.zeros(x.shape, F32))
    h = x + 0.5 * gate * acc
    if final:
        h = _rms(h, NORM_EPS) * fg_ref[...]
    o_ref[...] = h


def _ffn(x, mod, norm_g, w_in_r, w_out_r, *, mod_row, g_row, tm, final_g=None):
    B, T, D = x.shape
    per_batch = mod.shape[0] > 1
    final = final_g is not None
    in_specs = [
        pl.BlockSpec((None, tm, D), lambda b, i: (b, i, 0)),
        pl.BlockSpec((None, 9, D), (lambda b, i: (b, 0, 0)) if per_batch else (lambda b, i: (0, 0, 0))),
        _resident(norm_g.shape),
        _resident(w_in_r.shape),
        _resident(w_out_r.shape),
    ]
    args = [x, mod, norm_g, w_in_r, w_out_r]
    if final:
        in_specs.append(_resident(final_g.shape))
        args.append(final_g)
    return pl.pallas_call(
        functools.partial(_ffn_kernel, mod_row=mod_row, g_row=g_row, final=final),
        grid=(B, T // tm),
        in_specs=in_specs,
        out_specs=pl.BlockSpec((None, tm, D), lambda b, i: (b, i, 0)),
        out_shape=jax.ShapeDtypeStruct((B, T, D), F32),
        compiler_params=_params("parallel", "parallel"),
        name="ffn",
    )(*args)


def _rope(x, cos, sin_signed, lo_half):
    outs = []
    for j in range(x.shape[1] // LANES):
        xj = x[:, j * LANES:(j + 1) * LANES]
        rot = jnp.where(lo_half, pltpu.roll(xj, LANES - ROPE_HALF, 1), pltpu.roll(xj, ROPE_HALF, 1))
        outs.append(xj * cos + rot * sin_signed)
    return outs[0] if len(outs) == 1 else jnp.concatenate(outs, axis=1)


def _inproj_kernel(x_ref, mod_ref, g_ref, w_ref, *rest, rope):
    if rope:
        cos_ref, sin_ref = rest[:2]
        rest = rest[2:]
    qa_ref, ka_ref, va_ref, qb_ref, kb_ref, vb_ref, u_ref = rest
    x = x_ref[...]
    nb = _mod_rms(x, g_ref[1:2, :], mod_ref[3:4, :], mod_ref[4:5, :]).astype(BF16)

    def proj(lo, hi):
        return _dot(nb, w_ref[:, lo:hi])

    qa = proj(0, N_QA)
    ka = proj(N_QA, 2 * N_QA)
    va = proj(2 * N_QA, 3 * N_QA)
    qb = proj(3 * N_QA, 4 * N_QA)
    kvb = proj(4 * N_QA, N_ATTN)
    kb = kvb[:, :N_KVB]
    vb = kvb[:, N_KVB:]
    cu = proj(N_ATTN, N_PROJ)
    if rope:
        cos = cos_ref[...]
        sin_signed = sin_ref[...]
        lane = lax.broadcasted_iota(jnp.int32, cos.shape, 1)
        lo_half = (lane % (2 * ROPE_HALF)) < ROPE_HALF
        qa = _rope(qa, cos, sin_signed, lo_half)
        ka = _rope(ka, cos, sin_signed, lo_half)
        qb = _rope(qb, cos, sin_signed, lo_half)
        kb = _rope(kb, cos, sin_signed, lo_half)
    qa_ref[...] = (qa * SCALE_A).astype(qa_ref.dtype)
    ka_ref[...] = ka.astype(ka_ref.dtype)
    va_ref[...] = va.astype(va_ref.dtype)
    qb_ref[...] = (qb * SCALE_B).astype(qb_ref.dtype)
    kb_ref[...] = kb.astype(kb_ref.dtype)
    vb_ref[...] = vb.astype(vb_ref.dtype)
    u_ref[...] = cu[:, :C_CONV] * jax.nn.sigmoid(cu[:, C_CONV:])


def _inproj(h, mod, norm_g, w_proj, *, tm, kv_dtype, rope_tables=None):
    B, T, D = h.shape
    per_batch = mod.shape[0] > 1
    rope = rope_tables is not None
    row = lambda b, i: (b, i, 0)
    in_specs = [
        pl.BlockSpec((None, tm, D), row),
        pl.BlockSpec((None, 9, D), (lambda b, i: (b, 0, 0)) if per_batch else (lambda b, i: (0, 0, 0))),
        _resident(norm_g.shape),
        _resident(w_proj.shape),
    ]
    args = [h, mod, norm_g, w_proj]
    if rope:
        in_specs += [pl.BlockSpec((tm, LANES), lambda b, i: (i, 0))] * 2
        args += list(rope_tables)
    widths = (N_QA, N_QA, N_QA, H_B * DH_B, N_KVB, N_KVB, C_CONV)
    dtypes = (BF16, kv_dtype, kv_dtype, BF16, kv_dtype, kv_dtype, F32)
    return pl.pallas_call(
        functools.partial(_inproj_kernel, rope=rope),
        grid=(B, T // tm),
        in_specs=in_specs,
        out_specs=[pl.BlockSpec((None, tm, w), row) for w in widths],
        out_shape=[jax.ShapeDtypeStruct((B, T, w), dt) for w, dt in zip(widths, dtypes)],
        compiler_params=_params("parallel", "parallel"),
        name="inproj",
    )(*args)


def _diff_attn_kernel(*refs, n_cache, lam_init):
    if n_cache:
        q_ref, k_ref, v_ref, ck_ref, cv_ref, lam_ref, g_ref, o_ref, kbuf, vbuf = refs
    else:
        q_ref, k_ref, v_ref, lam_ref, g_ref, o_ref, kbuf, vbuf = refs

    @pl.when(pl.program_id(2) == 0)
    def _():
        if n_cache:
            kbuf[0:n_cache, :] = ck_ref[...].astype(BF16)
            vbuf[0:n_cache, :] = cv_ref[...].astype(BF16)
        kbuf[n_cache:, :] = k_ref[...].astype(BF16)
        vbuf[n_cache:, :] = v_ref[...].astype(BF16)

    lam = lam_ref[...]
    lam_full = (jnp.exp(jnp.sum(lam[0:1] * lam[1:2], axis=1, keepdims=True))
                - jnp.exp(jnp.sum(lam[2:3] * lam[3:4], axis=1, keepdims=True)) + lam_init)
    q = q_ref[...]
    lane = lax.broadcasted_iota(jnp.int32, q.shape, 1)
    zero = jnp.zeros_like(q)
    k = kbuf[...]

    def softmax_parts(qc):
        s = _dot_nt(qc, k)
        e = jnp.exp(s - jnp.max(s, axis=1, keepdims=True))
        return e, jnp.sum(e, axis=1, keepdims=True)

    e0, l0 = softmax_parts(jnp.where(lane < DH_A, q, zero))
    e1, l1 = softmax_parts(jnp.where(lane >= DH_A, q, zero))
    a = e0 * (1.0 / l0) - e1 * (lam_full / l1)
    o = _dot(a.astype(BF16), vbuf[...])
    o = _rms(o, SUBLN_EPS) * g_ref[...] * (1.0 - lam_init)
    o_ref[...] = o.astype(o_ref.dtype)


def _diff_attn(q, k, v, lam, subln_g, *, lam_init, tq, cache=None):
    B, T, _ = q.shape
    n_cache = 0 if cache is None else cache[0].shape[2]
    hd = 2 * DH_A
    in_specs = [
        pl.BlockSpec((None, tq, hd), lambda b, h, i: (b, i, h)),
        pl.BlockSpec((None, T, hd), lambda b, h, i: (b, 0, h)),
        pl.BlockSpec((None, T, hd), lambda b, h, i: (b, 0, h)),
    ]
    args = [q, k, v]
    if cache is not None:
        ck, cv, layer = cache
        spec = pl.BlockSpec((None, None, n_cache, hd), lambda b, h, i: (b, layer, 0, h))
        in_specs += [spec, spec]
        args += [ck, cv]
    in_specs += [_resident(lam.shape), _resident(subln_g.shape)]
    args += [lam, subln_g]
    return pl.pallas_call(
        functools.partial(_diff_attn_kernel, n_cache=n_cache, lam_init=lam_init),
        grid=(B, H_A, T // tq),
        in_specs=in_specs,
        out_specs=pl.BlockSpec((None, tq, hd), lambda b, h, i: (b, i, h)),
        out_shape=jax.ShapeDtypeStruct((B, T, N_QA), BF16),
        scratch_shapes=[pltpu.VMEM((n_cache + T, hd), BF16)] * 2,
        compiler_params=_params("parallel", "parallel", "arbitrary"),
        name="diff_attn",
    )(*args)


def _win_attn_kernel(*refs, local, has_ctx, seq_len):
    if has_ctx:
        q_ref, k_ref, v_ref, ck_ref, cv_ref, sink_ref, o_ref = refs
    else:
        q_ref, k_ref, v_ref, sink_ref, o_ref = refs
    q = q_ref[...]
    tq = q.shape[0]
    if local:
        n_loc = 3 * WIN_BLOCK
        n = pl.program_id(1)
        start = pl.multiple_of(jnp.clip((n - 1) * WIN_BLOCK, 0, seq_len - n_loc), WIN_BLOCK)
        k = k_ref[pl.ds(start, n_loc), :].astype(BF16)
        v = v_ref[pl.ds(start, n_loc), :].astype(BF16)
        rows = lax.broadcasted_iota(jnp.int32, (G_B * tq, n_loc), 0)
        cols = lax.broadcasted_iota(jnp.int32, (G_B * tq, n_loc), 1)
        qpos = n * WIN_BLOCK + rows % tq
        valid = jnp.abs(start + cols - qpos) <= WINDOW
    else:
        k = k_ref[...].astype(BF16)
        v = v_ref[...].astype(BF16)
    if has_ctx:
        ck = ck_ref[...].astype(BF16)
        cv = cv_ref[...].astype(BF16)
    lane = lax.broadcasted_iota(jnp.int32, (tq, LANES), 1)
    zero = jnp.zeros((tq, LANES), q.dtype)
    outs = []
    for j in range(KV_B):
        sel = (lane >= j * DH_B) & (lane < (j + 1) * DH_B)
        qj = jnp.concatenate(
            [jnp.where(sel, q[:, g * LANES:(g + 1) * LANES], zero) for g in range(G_B)], axis=0)
        sink = jnp.concatenate(
            [jnp.full((tq, 1), sink_ref[j * G_B + g], F32) for g in range(G_B)], axis=0)
        s = _dot_nt(qj, k)
        if local:
            s = jnp.where(valid, s, -jnp.inf)
        m = jnp.maximum(jnp.max(s, axis=1, keepdims=True), sink)
        if has_ctx:
            sc = _dot_nt(qj, ck)
            m = jnp.maximum(m, jnp.max(sc, axis=1, keepdims=True))
        e = jnp.exp(s - m)
        l = jnp.sum(e, axis=1, keepdims=True) + jnp.exp(sink - m)
        o = _dot(e.astype(BF16), v)
        if has_ctx:
            ec = jnp.exp(sc - m)
            l = l + jnp.sum(ec, axis=1, keepdims=True)
            o = o + _dot(ec.astype(BF16), cv)
        outs.append(o * (1.0 / l))
    lane4 = lax.broadcasted_iota(jnp.int32, outs[0].shape, 1)
    o = jnp.where(lane4 < DH_B, outs[0], outs[1])
    for g in range(G_B):
        o_ref[:, g * LANES:(g + 1) * LANES] = o[g * tq:(g + 1) * tq].astype(o_ref.dtype)


def _win_attn(q, k, v, sink, *, tq, local, cache=None):
    B, T, _ = q.shape
    has_ctx = cache is not None
    in_specs = [
        pl.BlockSpec((None, tq, G_B * LANES), lambda b, n: (b, n, 0)),
        pl.BlockSpec((None, T, N_KVB), lambda b, n: (b, 0, 0)),
        pl.BlockSpec((None, T, N_KVB), lambda b, n: (b, 0, 0)),
    ]
    args = [q, k, v]
    if has_ctx:
        ck, cv, layer = cache
        spec = pl.BlockSpec((None, None, ck.shape[2], N_KVB), lambda b, n: (b, layer, 0, 0))
        in_specs += [spec, spec]
        args += [ck, cv]
    in_specs.append(pl.BlockSpec(memory_space=pltpu.SMEM))
    args.append(sink)
    return pl.pallas_call(
        functools.partial(_win_attn_kernel, local=local, has_ctx=has_ctx, seq_len=T),
        grid=(B, T // tq),
        in_specs=in_specs,
        out_specs=pl.BlockSpec((None, tq, G_B * LANES), lambda b, n: (b, n, 0)),
        out_shape=jax.ShapeDtypeStruct((B, T, H_B * DH_B), BF16),
        compiler_params=_params("parallel", "parallel"),
        name="win_attn",
    )(*args)


def _conv_kernel(prev_ref, cur_ref, next_ref, w_ref, bdw_ref, g_ref, b_ref, o_ref, win_ref):
    i = pl.program_id(1)
    rt = cur_ref.shape[0]
    zeros = jnp.zeros((CONV_HALO, C_CONV), F32)
    win_ref[0:CONV_HALO, :] = jnp.where(i > 0, prev_ref[...], zeros)
    win_ref[CONV_HALO:CONV_HALO + rt, :] = cur_ref[...]
    win_ref[CONV_HALO + rt:, :] = jnp.where(i < pl.num_programs(1) - 1, next_ref[...], zeros)
    acc = jnp.zeros((rt, C_CONV), F32) + bdw_ref[...]
    off = CONV_HALO - CONV_K // 2
    for t in range(CONV_K):
        acc = acc + win_ref[off + t:off + t + rt, :] * w_ref[t:t + 1, :]
    mu = jnp.mean(acc, axis=-1, keepdims=True)
    cen = acc - mu
    var = jnp.mean(cen * cen, axis=-1, keepdims=True)
    y = cen * lax.rsqrt(var + NORM_EPS) * g_ref[...] + b_ref[...]
    o_ref[...] = _silu(y).astype(o_ref.dtype)


def _conv_module(u, w_dw, b_dw, g, b, *, rt):
    B, T, C = u.shape
    hb = rt // CONV_HALO
    last = T // CONV_HALO - 1
    in_specs = [
        pl.BlockSpec((None, CONV_HALO, C), lambda bb, i: (bb, jnp.maximum(i * hb - 1, 0), 0)),
        pl.BlockSpec((None, rt, C), lambda bb, i: (bb, i, 0)),
        pl.BlockSpec((None, CONV_HALO, C), lambda bb, i: (bb, jnp.minimum((i + 1) * hb, last), 0)),
        _resident(w_dw.shape), _resident(b_dw.shape), _resident(g.shape), _resident(b.shape),
    ]
    return pl.pallas_call(
        _conv_kernel,
        grid=(B, T // rt),
        in_specs=in_specs,
        out_specs=pl.BlockSpec((None, rt, C), lambda bb, i: (bb, i, 0)),
        out_shape=jax.ShapeDtypeStruct((B, T, C), BF16),
        scratch_shapes=[pltpu.VMEM((rt + 2 * CONV_HALO, C), F32)],
        compiler_params=_params("parallel", "parallel"),
        name="conv_module",
    )(u, u, u, w_dw, b_dw, g, b)


def _merge_kernel(h_ref, mod_ref, g_ref, wg_ref, ya_ref, yb_ref, yc_ref, wb_ref, wo_ref, o_ref):
    h = h_ref[...]
    nb = _mod_rms(h, g_ref[1:2, :], mod_ref[3:4, :], mod_ref[4:5, :]).astype(BF16)
    merged = None
    for i, y_ref in enumerate((ya_ref, yb_ref, yc_ref)):
        gate = jax.nn.sigmoid(_dot(nb, wg_ref[:, i * D_MODEL:(i + 1) * D_MODEL]))
        term = gate * _dot(y_ref[...], wb_ref[i])
        merged = term if merged is None else merged + term
    o_ref[...] = h + mod_ref[5:6, :] * _dot(merged.astype(BF16), wo_ref[...])


def _merge(h, mod, norm_g, w_gate, ya, yb, yc, w_branch, w_out, *, tm):
    B, T, D = h.shape
    per_batch = mod.shape[0] > 1
    row = lambda b, i: (b, i, 0)
    y_spec = pl.BlockSpec((None, tm, BRANCH_W), row)
    return pl.pallas_call(
        _merge_kernel,
        grid=(B, T // tm),
        in_specs=[
            pl.BlockSpec((None, tm, D), row),
            pl.BlockSpec((None, 9, D), (lambda b, i: (b, 0, 0)) if per_batch else (lambda b, i: (0, 0, 0))),
            _resident(norm_g.shape), _resident(w_gate.shape),
            y_spec, y_spec, y_spec,
            _resident(w_branch.shape), _resident(w_out.shape),
        ],
        out_specs=pl.BlockSpec((None, tm, D), row),
        out_shape=jax.ShapeDtypeStruct((B, T, D), F32),
        compiler_params=_params("parallel", "parallel"),
        name="merge",
    )(h, mod, norm_g, w_gate, ya, yb, yc, w_branch, w_out)


def _rope_tables(n_tokens):
    quarter = DH_A // 4
    t = np.arange(n_tokens)
    inv = (ROPE_BASE ** (-np.arange(0, DH_A // 2, 2, dtype=np.float32) / (DH_A // 2))).astype(np.float32)
    pos = np.stack([(t // GRID_W).astype(np.float32), (t % GRID_W).astype(np.float32)], axis=1)
    ang = (pos[:, :, None] * inv[None, None, :]).astype(np.float32)
    ang = np.broadcast_to(ang[:, :, None, :], (n_tokens, 2, 2, quarter)).reshape(n_tokens, DH_A)
    sign = np.where((np.arange(DH_A) % (2 * quarter)) < quarter, -1.0, 1.0).astype(np.float32)
    cos = np.cos(ang).astype(np.float32)
    sin_signed = (np.sin(ang) * sign).astype(np.float32)
    reps = LANES // DH_A
    return jnp.asarray(np.tile(cos, (1, reps))), jnp.asarray(np.tile(sin_signed, (1, reps)))


def _group_major(w, axis):
    shape = w.shape
    split = shape[:axis] + (KV_B, G_B, DH_B) + shape[axis + 1:]
    return jnp.swapaxes(w.reshape(split), axis, axis + 1).reshape(shape)


def _layer_weights(l, ffn_w_in, ffn_w_out, w_in, w_branch, w_out):
    n_chunks = D_FF // FF_CHUNK
    ffn_in, ffn_out = [], []
    for s in range(2):
        w = ffn_w_in[l, s].reshape(D_MODEL, 2, n_chunks, FF_CHUNK)
        ffn_in.append(jnp.transpose(w, (2, 0, 1, 3)).reshape(n_chunks, D_MODEL, 2 * FF_CHUNK).astype(BF16))
        ffn_out.append(ffn_w_out[l, s].reshape(n_chunks, FF_CHUNK, D_MODEL).astype(BF16))
    wl = w_in[l]
    qb_lo = 3 * N_QA
    w_proj = jnp.concatenate(
        [wl[:, :qb_lo], _group_major(wl[:, qb_lo:qb_lo + H_B * DH_B], 1), wl[:, qb_lo + H_B * DH_B:N_PROJ]],
        axis=1).astype(BF16)
    w_gate = wl[:, N_PROJ:].astype(BF16)
    wb = jnp.stack([w_branch[l, 0], _group_major(w_branch[l, 1], 0), w_branch[l, 2]]).astype(BF16)
    return ffn_in, ffn_out, w_proj, w_gate, wb, w_out[l].astype(BF16)


def kernel(x_prompt, x_sample, cache_diff_k, cache_diff_v, cache_win_k, cache_win_v, c, c_ctx,
           w_mod, b_mod, norm_g, ffn_w_in, ffn_w_out, w_in, diff_lambda, diff_subln_g, win_sink,
           conv_dw_w, conv_dw_b, conv_norm_g, conv_norm_b, w_branch, w_out, final_norm_g):
    n_ctx_b, n_ctx_t, D = x_prompt.shape
    n_lat_b, n_lat_t, _ = x_sample.shape
    past = cache_diff_k.shape[2]
    tm = 512

    n_rows = 1 + n_lat_b
    pad_rows = -n_rows % 8
    cond = jnp.concatenate([c_ctx[None, :], c, jnp.zeros((pad_rows, D), F32)], axis=0)
    mod_all = _modulation(cond, w_mod, b_mod.reshape(DEPTH, 1, 9 * D))

    rope_tables = _rope_tables(n_lat_t)
    ck_a = cache_diff_k.reshape(n_lat_b, DEPTH, past, N_QA)
    cv_a = cache_diff_v.reshape(n_lat_b, DEPTH, past, N_QA)
    ck_b = cache_win_k.reshape(n_lat_b, DEPTH, past, N_KVB)
    cv_b = cache_win_v.reshape(n_lat_b, DEPTH, past, N_KVB)
    final_g = final_norm_g.reshape(1, D)

    xp = x_prompt.reshape(1, n_ctx_b * n_ctx_t, D)
    xs = x_sample
    new_cache = [[], [], [], []]
    for l in range(DEPTH):
        lam_init = 0.8 - 0.6 * math.exp(-0.3 * l)
        ffn_in, ffn_out, w_proj, w_gate, wb, wo = _layer_weights(l, ffn_w_in, ffn_w_out, w_in, w_branch, w_out)
        g_l = norm_g[l]
        lam = diff_lambda[l]
        subln = diff_subln_g[l].reshape(1, 2 * DH_A)
        sink = win_sink[l]
        conv_p = (conv_dw_w[l], conv_dw_b[l].reshape(1, C_CONV),
                  conv_norm_g[l].reshape(1, C_CONV), conv_norm_b[l].reshape(1, C_CONV))
        mod_ctx = mod_all[l, 0:1].reshape(1, 9, D)
        mod_lat = mod_all[l, 1:n_rows].reshape(n_lat_b, 9, D)
        last = l == DEPTH - 1

        h = _ffn(xp, mod_ctx, g_l, ffn_in[0], ffn_out[0], mod_row=0, g_row=0, tm=tm)
        qa, ka, va, qb, kb, vb, u = _inproj(h, mod_ctx, g_l, w_proj, tm=tm, kv_dtype=F32)
        seq = lambda a: a.reshape(n_ctx_b, n_ctx_t, a.shape[-1])
        ka, va, kb, vb = seq(ka), seq(va), seq(kb), seq(vb)
        ya = _diff_attn(seq(qa), ka, va, lam, subln, lam_init=lam_init, tq=n_ctx_t)
        yb = _win_attn(seq(qb), kb, vb, sink, tq=n_ctx_t, local=False)
        yc = _conv_module(seq(u), *conv_p, rt=n_ctx_t)
        flat = lambda a: a.reshape(1, n_ctx_b * n_ctx_t, a.shape[-1])
        h = _merge(h, mod_ctx, g_l, w_gate, flat(ya), flat(yb), flat(yc), wb, wo, tm=tm)
        xp = _ffn(h, mod_ctx, g_l, ffn_in[1], ffn_out[1], mod_row=6, g_row=2, tm=tm,
                  final_g=final_g if last else None)
        for dst, val in zip(new_cache, (ka, va, kb, vb)):
            dst.append(val)

        h = _ffn(xs, mod_lat, g_l, ffn_in[0], ffn_out[0], mod_row=0, g_row=0, tm=tm)
        qa, ka, va, qb, kb, vb, u = _inproj(h, mod_lat, g_l, w_proj, tm=tm, kv_dtype=BF16,
                                            rope_tables=rope_tables)
        ya = _diff_attn(qa, ka, va, lam, subln, lam_init=lam_init, tq=256, cache=(ck_a, cv_a, l))
        yb = _win_attn(qb, kb, vb, sink, tq=WIN_BLOCK, local=True, cache=(ck_b, cv_b, l))
        yc = _conv_module(u, *conv_p, rt=256)
        h = _merge(h, mod_lat, g_l, w_gate, ya, yb, yc, wb, wo, tm=tm)
        xs = _ffn(h, mod_lat, g_l, ffn_in[1], ffn_out[1], mod_row=6, g_row=2, tm=tm,
                  final_g=final_g if last else None)

    y_prompt = xp.reshape(n_ctx_b, n_ctx_t, D)
    new_diff_k = jnp.stack(new_cache[0], axis=1).reshape(n_ctx_b, DEPTH, n_ctx_t, H_A, 2, DH_A)
    new_diff_v = jnp.stack(new_cache[1], axis=1).reshape(n_ctx_b, DEPTH, n_ctx_t, H_A, 2 * DH_A)
    new_win_k = jnp.stack(new_cache[2], axis=1).reshape(n_ctx_b, DEPTH, n_ctx_t, KV_B, DH_B)
    new_win_v = jnp.stack(new_cache[3], axis=1).reshape(n_ctx_b, DEPTH, n_ctx_t, KV_B, DH_B)
    return (y_prompt, xs, new_diff_k, new_diff_v, new_win_k, new_win_v)
```

```python
import functools
import math

import numpy as np
import jax
import jax.numpy as jnp
from jax import lax
from jax.experimental import pallas as pl
from jax.experimental.pallas import tpu as pltpu

D_MODEL = 1024
DEPTH = 2
GRID_W = 64
H_A = 4
DH_A = 64
H_B = 8
KV_B = 2
G_B = H_B // KV_B
DH_B = 64
WINDOW = 128
WIN_BLOCK = 128
C_CONV = 512
CONV_K = 31
BRANCH_W = 512
D_FF = 2816
ROPE_BASE = 10000.0
NORM_EPS = 1e-6
SUBLN_EPS = 1e-5
SCALE_A = DH_A ** -0.5
SCALE_B = DH_B ** -0.5
LOG2E = math.log2(math.e)

N_QA = H_A * 2 * DH_A
N_KVB = KV_B * DH_B
N_ATTN = 3 * N_QA + H_B * DH_B + 2 * N_KVB
N_PROJ = N_ATTN + 2 * C_CONV

LANES = 128
SUBLANES = 8
ROPE_HALF = DH_A // 4
FF_CHUNK = 256
CONV_HALO = 16
CONV_ROWS = 32
VMEM_LIMIT = 56 * 1024 * 1024

BF16 = jnp.bfloat16
F32 = jnp.float32


def _params(*sem):
    return pltpu.CompilerParams(dimension_semantics=sem, vmem_limit_bytes=VMEM_LIMIT)


def _resident(shape):
    nd = len(shape)
    return pl.BlockSpec(shape, lambda *_: (0,) * nd, pipeline_mode=pl.Buffered(1))


def _resident_slice(shape, lead):
    n = len(lead)
    block = (None,) * n + tuple(shape[n:])
    index = tuple(lead) + (0,) * (len(shape) - n)
    return pl.BlockSpec(block, lambda *_: index, pipeline_mode=pl.Buffered(1))


def _dot(a, b):
    return jnp.dot(a, b, preferred_element_type=F32)


def _dot_nt(a, b):
    return lax.dot_general(a, b, (((1,), (1,)), ((), ())), preferred_element_type=F32)


def _rms(x, eps):
    return x * lax.rsqrt(jnp.mean(x * x, axis=-1, keepdims=True) + eps)


def _mod_rms(x, g, shift, scale):
    return _rms(x, NORM_EPS) * g * (1.0 + scale) + shift


def _silu(x):
    return x * jax.nn.sigmoid(x)


def _mod_kernel(c_ref, w_ref, b_ref, o_ref):
    a = _silu(c_ref[...]).astype(BF16)
    o_ref[...] = _dot(a, w_ref[...].astype(BF16)) + b_ref[...]


def _modulation(cond, w_mod, b_mod):
    R = cond.shape[0]
    tn = D_MODEL
    return pl.pallas_call(
        _mod_kernel,
        grid=(DEPTH, 9 * D_MODEL // tn),
        in_specs=[
            pl.BlockSpec((R, D_MODEL), lambda l, j: (0, 0)),
            pl.BlockSpec((None, D_MODEL, tn), lambda l, j: (l, 0, j)),
            pl.BlockSpec((None, 1, tn), lambda l, j: (l, 0, j)),
        ],
        out_specs=pl.BlockSpec((None, R, tn), lambda l, j: (l, 0, j)),
        out_shape=jax.ShapeDtypeStruct((DEPTH, R, 9 * D_MODEL), F32),
        compiler_params=_params("parallel", "parallel"),
        name="modulation",
    )(cond, w_mod, b_mod)


def _ffn_kernel(x_ref, mod_ref, g_ref, win_ref, wout_ref, *rest, mod_row, g_row, final):
    if final:
        fg_ref, o_ref = rest
    else:
        (o_ref,) = rest
    x = x_ref[...]
    shift = mod_ref[mod_row:mod_row + 1, :]
    scale = mod_ref[mod_row + 1:mod_row + 2, :]
    gate = mod_ref[mod_row + 2:mod_row + 3, :]
    nb = _mod_rms(x, g_ref[g_row:g_row + 1, :], shift, scale).astype(BF16)
    acc = None
    for lo in range(0, D_FF, FF_CHUNK):
        a = _dot(nb, win_ref[:, lo:lo + FF_CHUNK])
        b = _dot(nb, win_ref[:, D_FF + lo:D_FF + lo + FF_CHUNK])
        hid = (_silu(a) * b).astype(BF16)
        part = _dot(hid, wout_ref[lo:lo + FF_CHUNK, :])
        acc = part if acc is None else acc + part
    h = x + 0.5 * gate * acc
    if final:
        h = _rms(h, NORM_EPS) * fg_ref[...]
    o_ref[...] = h


def _ffn(x, mod, norm_g, w_in, w_out, which, *, mod_row, g_row, tm, final_g=None):
    B, T, D = x.shape
    per_batch = mod.shape[0] > 1
    final = final_g is not None
    l, s = which
    in_specs = [
        pl.BlockSpec((None, tm, D), lambda b, i: (b, i, 0)),
        pl.BlockSpec((None, 9, D), (lambda b, i: (b, 0, 0)) if per_batch else (lambda b, i: (0, 0, 0))),
        _resident(norm_g.shape),
        _resident_slice(w_in.shape, (l, s)),
        _resident_slice(w_out.shape, (l, s)),
    ]
    args = [x, mod, norm_g, w_in, w_out]
    if final:
        in_specs.append(_resident(final_g.shape))
        args.append(final_g)
    return pl.pallas_call(
        functools.partial(_ffn_kernel, mod_row=mod_row, g_row=g_row, final=final),
        grid=(B, T // tm),
        in_specs=in_specs,
        out_specs=pl.BlockSpec((None, tm, D), lambda b, i: (b, i, 0)),
        out_shape=jax.ShapeDtypeStruct((B, T, D), F32),
        compiler_params=_params("parallel", "parallel"),
        name="ffn",
    )(*args)


def _rope(x, cos, sin_signed, lo_half):
    outs = []
    for j in range(x.shape[1] // LANES):
        xj = x[:, j * LANES:(j + 1) * LANES]
        rot = jnp.where(lo_half, pltpu.roll(xj, LANES - ROPE_HALF, 1), pltpu.roll(xj, ROPE_HALF, 1))
        outs.append(xj * cos + rot * sin_signed)
    return outs[0] if len(outs) == 1 else jnp.concatenate(outs, axis=1)


def _inproj_kernel(x_ref, mod_ref, g_ref, w_ref, *rest, rope):
    if rope:
        cos_ref, sin_ref = rest[:2]
        rest = rest[2:]
    qa_ref, ka_ref, va_ref, qb_ref, kb_ref, vb_ref, u_ref = rest
    x = x_ref[...]
    nb = _mod_rms(x, g_ref[1:2, :], mod_ref[3:4, :], mod_ref[4:5, :]).astype(BF16)

    def proj(lo, hi):
        return _dot(nb, w_ref[:, lo:hi])

    qa = proj(0, N_QA)
    ka = proj(N_QA, 2 * N_QA)
    va = proj(2 * N_QA, 3 * N_QA)
    qb = proj(3 * N_QA, 4 * N_QA)
    kvb = proj(4 * N_QA, N_ATTN)
    kb = kvb[:, :N_KVB]
    vb = kvb[:, N_KVB:]
    cu = proj(N_ATTN, N_PROJ)
    if rope:
        cos = cos_ref[...]
        sin_signed = sin_ref[...]
        lane = lax.broadcasted_iota(jnp.int32, cos.shape, 1)
        lo_half = (lane % (2 * ROPE_HALF)) < ROPE_HALF
        qa = _rope(qa, cos, sin_signed, lo_half)
        ka = _rope(ka, cos, sin_signed, lo_half)
        qb = _rope(qb, cos, sin_signed, lo_half)
        kb = _rope(kb, cos, sin_signed, lo_half)
    qa_ref[...] = (qa * (SCALE_A * LOG2E)).astype(qa_ref.dtype)
    ka_ref[...] = ka.astype(ka_ref.dtype)
    va_ref[...] = va.astype(va_ref.dtype)
    qb_ref[...] = (qb * (SCALE_B * LOG2E)).astype(qb_ref.dtype)
    kb_ref[...] = kb.astype(kb_ref.dtype)
    vb_ref[...] = vb.astype(vb_ref.dtype)
    u_ref[...] = cu[:, :C_CONV] * jax.nn.sigmoid(cu[:, C_CONV:])


def _inproj(h, mod, norm_g, w_proj, layer, *, tm, kv_dtype, rope_tables=None):
    B, T, D = h.shape
    per_batch = mod.shape[0] > 1
    rope = rope_tables is not None
    row = lambda b, i: (b, i, 0)
    in_specs = [
        pl.BlockSpec((None, tm, D), row),
        pl.BlockSpec((None, 9, D), (lambda b, i: (b, 0, 0)) if per_batch else (lambda b, i: (0, 0, 0))),
        _resident(norm_g.shape),
        _resident_slice(w_proj.shape, (layer,)),
    ]
    args = [h, mod, norm_g, w_proj]
    if rope:
        in_specs += [pl.BlockSpec((tm, LANES), lambda b, i: (i, 0))] * 2
        args += list(rope_tables)
    widths = (N_QA, N_QA, N_QA, H_B * DH_B, N_KVB, N_KVB, C_CONV)
    dtypes = (BF16, kv_dtype, kv_dtype, BF16, kv_dtype, kv_dtype, F32)
    return pl.pallas_call(
        functools.partial(_inproj_kernel, rope=rope),
        grid=(B, T // tm),
        in_specs=in_specs,
        out_specs=[pl.BlockSpec((None, tm, w), row) for w in widths],
        out_shape=[jax.ShapeDtypeStruct((B, T, w), dt) for w, dt in zip(widths, dtypes)],
        compiler_params=_params("parallel", "parallel"),
        name="inproj",
    )(*args)


def _diff_attn_kernel(*refs, n_cache, lam_init):
    if n_cache:
        q_ref, k_ref, v_ref, ck_ref, cv_ref, lam_ref, g_ref, o_ref, kbuf, vbuf = refs
    else:
        q_ref, k_ref, v_ref, lam_ref, g_ref, o_ref, kbuf, vbuf = refs

    @pl.when(pl.program_id(2) == 0)
    def _():
        if n_cache:
            kbuf[0:n_cache, :] = ck_ref[...].astype(BF16)
            vbuf[0:n_cache, :] = cv_ref[...].astype(BF16)
        kbuf[n_cache:, :] = k_ref[...].astype(BF16)
        vbuf[n_cache:, :] = v_ref[...].astype(BF16)

    lam = lam_ref[...]
    lam_full = (jnp.exp(jnp.sum(lam[0:1] * lam[1:2], axis=1, keepdims=True))
                - jnp.exp(jnp.sum(lam[2:3] * lam[3:4], axis=1, keepdims=True)) + lam_init)
    q = q_ref[...]
    lane = lax.broadcasted_iota(jnp.int32, q.shape, 1)
    zero = jnp.zeros_like(q)
    k = kbuf[...]

    def softmax_parts(qc):
        s = _dot_nt(qc, k)
        e = jnp.exp2(s - jnp.max(s, axis=1, keepdims=True))
        return e, jnp.sum(e, axis=1, keepdims=True)

    e0, l0 = softmax_parts(jnp.where(lane < DH_A, q, zero))
    e1, l1 = softmax_parts(jnp.where(lane >= DH_A, q, zero))
    a = e0 * (1.0 / l0) - e1 * (lam_full / l1)
    o = _dot(a.astype(BF16), vbuf[...])
    o = _rms(o, SUBLN_EPS) * g_ref[...] * (1.0 - lam_init)
    o_ref[...] = o.astype(o_ref.dtype)


def _diff_attn(q, k, v, lam, subln_g, *, lam_init, tq, cache=None):
    B, T, _ = q.shape
    n_cache = 0 if cache is None else cache[0].shape[2]
    hd = 2 * DH_A
    in_specs = [
        pl.BlockSpec((None, tq, hd), lambda b, h, i: (b, i, h)),
        pl.BlockSpec((None, T, hd), lambda b, h, i: (b, 0, h)),
        pl.BlockSpec((None, T, hd), lambda b, h, i: (b, 0, h)),
    ]
    args = [q, k, v]
    if cache is not None:
        ck, cv, layer = cache
        spec = pl.BlockSpec((None, None, n_cache, hd), lambda b, h, i: (b, layer, 0, h))
        in_specs += [spec, spec]
        args += [ck, cv]
    in_specs += [_resident(lam.shape), _resident(subln_g.shape)]
    args += [lam, subln_g]
    return pl.pallas_call(
        functools.partial(_diff_attn_kernel, n_cache=n_cache, lam_init=lam_init),
        grid=(B, H_A, T // tq),
        in_specs=in_specs,
        out_specs=pl.BlockSpec((None, tq, hd), lambda b, h, i: (b, i, h)),
        out_shape=jax.ShapeDtypeStruct((B, T, N_QA), BF16),
        scratch_shapes=[pltpu.VMEM((n_cache + T, hd), BF16)] * 2,
        compiler_params=_params("parallel", "parallel", "arbitrary"),
        name="diff_attn",
    )(*args)


def _win_attn_kernel(*refs, local, has_ctx, seq_len, tq, nq):
    if has_ctx:
        q_ref, k_ref, v_ref, ck_ref, cv_ref, sink_ref, o_ref = refs
        ck = ck_ref[...].astype(BF16)
        cv = cv_ref[...].astype(BF16)
    else:
        q_ref, k_ref, v_ref, sink_ref, o_ref = refs
    n_loc = 3 * WIN_BLOCK
    lane = lax.broadcasted_iota(jnp.int32, (tq, LANES), 1)
    zero = jnp.zeros((tq, LANES), BF16)
    for qi in range(nq):
        rows_q = slice(qi * tq, (qi + 1) * tq)
        q = q_ref[rows_q, :]
        if local:
            n = pl.program_id(1) * nq + qi
            start = pl.multiple_of(jnp.clip((n - 1) * WIN_BLOCK, 0, seq_len - n_loc), WIN_BLOCK)
            k = k_ref[pl.ds(start, n_loc), :].astype(BF16)
            v = v_ref[pl.ds(start, n_loc), :].astype(BF16)
        else:
            k = k_ref[...].astype(BF16)
            v = v_ref[...].astype(BF16)
        if has_ctx:
            k = jnp.concatenate([k, ck], axis=0)
            v = jnp.concatenate([v, cv], axis=0)
        n_keys = k.shape[0]
        if local:
            rows = lax.broadcasted_iota(jnp.int32, (tq, n_keys), 0)
            cols = lax.broadcasted_iota(jnp.int32, (tq, n_keys), 1)
            valid = (cols >= n_loc) | (jnp.abs(start + cols - (n * WIN_BLOCK + rows)) <= WINDOW)
        halves = []
        for j in range(KV_B):
            sel = (lane >= j * DH_B) & (lane < (j + 1) * DH_B)
            qj = jnp.concatenate(
                [jnp.where(sel, q[:, g * LANES:(g + 1) * LANES], zero) for g in range(G_B)], axis=0)
            s = _dot_nt(qj, k)
            es, inv = [], []
            for g in range(G_B):
                sg = s[g * tq:(g + 1) * tq]
                if local:
                    sg = jnp.where(valid, sg, -jnp.inf)
                sink = sink_ref[j * G_B + g] * LOG2E
                m = jnp.maximum(jnp.max(sg, axis=1, keepdims=True), sink)
                e = jnp.exp2(sg - m)
                l = jnp.sum(e, axis=1, keepdims=True) + jnp.exp2(sink - m)
                es.append(e.astype(BF16))
                inv.append(1.0 / l)
            halves.append((_dot(jnp.concatenate(es, axis=0), v), inv))
        for g in range(G_B):
            rows_g = slice(g * tq, (g + 1) * tq)
            og = jnp.where(lane < DH_B, halves[0][0][rows_g] * halves[0][1][g],
                           halves[1][0][rows_g] * halves[1][1][g])
            o_ref[rows_q, g * LANES:(g + 1) * LANES] = og.astype(o_ref.dtype)


def _win_attn(q, k, v, sink, *, tq, nq, local, cache=None):
    B, T, _ = q.shape
    has_ctx = cache is not None
    assert not local or tq == WIN_BLOCK
    rows = nq * tq
    in_specs = [
        pl.BlockSpec((None, rows, G_B * LANES), lambda b, n: (b, n, 0)),
        pl.BlockSpec((None, T, N_KVB), lambda b, n: (b, 0, 0)),
        pl.BlockSpec((None, T, N_KVB), lambda b, n: (b, 0, 0)),
    ]
    args = [q, k, v]
    if has_ctx:
        ck, cv, layer = cache
        spec = pl.BlockSpec((None, None, ck.shape[2], N_KVB), lambda b, n: (b, layer, 0, 0))
        in_specs += [spec, spec]
        args += [ck, cv]
    in_specs.append(pl.BlockSpec(memory_space=pltpu.SMEM))
    args.append(sink)
    return pl.pallas_call(
        functools.partial(_win_attn_kernel, local=local, has_ctx=has_ctx, seq_len=T, tq=tq, nq=nq),
        grid=(B, T // rows),
        in_specs=in_specs,
        out_specs=pl.BlockSpec((None, rows, G_B * LANES), lambda b, n: (b, n, 0)),
        out_shape=jax.ShapeDtypeStruct((B, T, H_B * DH_B), BF16),
        compiler_params=_params("parallel", "parallel"),
        name="win_attn",
    )(*args)


def _conv_kernel(prev_ref, cur_ref, next_ref, w_ref, bdw_ref, g_ref, b_ref, o_ref, win_ref, ph_ref):
    i = pl.program_id(1)
    rt = cur_ref.shape[0]
    zeros = jnp.zeros((CONV_HALO, C_CONV), F32)
    win_ref[0:CONV_HALO, :] = jnp.where(i > 0, prev_ref[...], zeros)
    win_ref[CONV_HALO:CONV_HALO + rt, :] = cur_ref[...]
    win_ref[CONV_HALO + rt:, :] = jnp.where(i < pl.num_programs(1) - 1, next_ref[...], zeros)
    n_ph = ph_ref.shape[1]
    for p in range(1, SUBLANES):
        ph_ref[p - 1] = win_ref[p:p + n_ph, :]
    first = CONV_HALO - CONV_K // 2
    tiles = CONV_ROWS // SUBLANES
    for r0 in range(0, rt, CONV_ROWS):
        acc = jnp.zeros((tiles, SUBLANES, C_CONV), F32) + bdw_ref[...]
        for t in range(CONV_K):
            p = (first + t) % SUBLANES
            base = r0 + first + t - p
            x = win_ref[base:base + CONV_ROWS, :] if p == 0 else ph_ref[p - 1, base:base + CONV_ROWS, :]
            acc = acc + x.reshape(tiles, SUBLANES, C_CONV) * w_ref[t]
        acc = acc.reshape(CONV_ROWS, C_CONV)
        mu = jnp.mean(acc, axis=-1, keepdims=True)
        cen = acc - mu
        var = jnp.mean(cen * cen, axis=-1, keepdims=True)
        y = cen * lax.rsqrt(var + NORM_EPS) * g_ref[...] + b_ref[...]
        o_ref[r0:r0 + CONV_ROWS, :] = _silu(y).astype(o_ref.dtype)


def _conv_module(u, w_dw, b_dw, g, b, *, rt):
    B, T, C = u.shape
    hb = rt // CONV_HALO
    last = T // CONV_HALO - 1
    n_ph = rt + (CONV_HALO + CONV_K // 2) // SUBLANES * SUBLANES
    w_tiles = jnp.broadcast_to(w_dw[:, None, :], (CONV_K, SUBLANES, C))
    in_specs = [
        pl.BlockSpec((None, CONV_HALO, C), lambda bb, i: (bb, jnp.maximum(i * hb - 1, 0), 0)),
        pl.BlockSpec((None, rt, C), lambda bb, i: (bb, i, 0)),
        pl.BlockSpec((None, CONV_HALO, C), lambda bb, i: (bb, jnp.minimum((i + 1) * hb, last), 0)),
        _resident(w_tiles.shape), _resident(b_dw.shape), _resident(g.shape), _resident(b.shape),
    ]
    return pl.pallas_call(
        _conv_kernel,
        grid=(B, T // rt),
        in_specs=in_specs,
        out_specs=pl.BlockSpec((None, rt, C), lambda bb, i: (bb, i, 0)),
        out_shape=jax.ShapeDtypeStruct((B, T, C), BF16),
        scratch_shapes=[pltpu.VMEM((rt + 2 * CONV_HALO, C), F32),
                        pltpu.VMEM((SUBLANES - 1, n_ph, C), F32)],
        compiler_params=_params("parallel", "parallel"),
        name="conv_module",
    )(u, u, u, w_tiles, b_dw, g, b)


def _merge_kernel(h_ref, mod_ref, g_ref, wg_ref, ya_ref, yb_ref, yc_ref, wb_ref, wo_ref, o_ref):
    h = h_ref[...]
    nb = _mod_rms(h, g_ref[1:2, :], mod_ref[3:4, :], mod_ref[4:5, :]).astype(BF16)
    merged = None
    for i, y_ref in enumerate((ya_ref, yb_ref, yc_ref)):
        gate = jax.nn.sigmoid(_dot(nb, wg_ref[:, i * D_MODEL:(i + 1) * D_MODEL]))
        term = gate * _dot(y_ref[...], wb_ref[i])
        merged = term if merged is None else merged + term
    o_ref[...] = h + mod_ref[5:6, :] * _dot(merged.astype(BF16), wo_ref[...])


def _merge(h, mod, norm_g, w_gate, ya, yb, yc, w_branch, w_out, layer, *, tm):
    B, T, D = h.shape
    per_batch = mod.shape[0] > 1
    row = lambda b, i: (b, i, 0)
    y_spec = pl.BlockSpec((None, tm, BRANCH_W), row)
    return pl.pallas_call(
        _merge_kernel,
        grid=(B, T // tm),
        in_specs=[
            pl.BlockSpec((None, tm, D), row),
            pl.BlockSpec((None, 9, D), (lambda b, i: (b, 0, 0)) if per_batch else (lambda b, i: (0, 0, 0))),
            _resident(norm_g.shape), _resident_slice(w_gate.shape, (layer,)),
            y_spec, y_spec, y_spec,
            _resident_slice(w_branch.shape, (layer,)), _resident_slice(w_out.shape, (layer,)),
        ],
        out_specs=pl.BlockSpec((None, tm, D), row),
        out_shape=jax.ShapeDtypeStruct((B, T, D), F32),
        compiler_params=_params("parallel", "parallel"),
        name="merge",
    )(h, mod, norm_g, w_gate, ya, yb, yc, w_branch, w_out)


def _rope_tables(n_tokens):
    quarter = DH_A // 4
    t = np.arange(n_tokens)
    inv = (ROPE_BASE ** (-np.arange(0, DH_A // 2, 2, dtype=np.float32) / (DH_A // 2))).astype(np.float32)
    pos = np.stack([(t // GRID_W).astype(np.float32), (t % GRID_W).astype(np.float32)], axis=1)
    ang = (pos[:, :, None] * inv[None, None, :]).astype(np.float32)
    ang = np.broadcast_to(ang[:, :, None, :], (n_tokens, 2, 2, quarter)).reshape(n_tokens, DH_A)
    sign = np.where((np.arange(DH_A) % (2 * quarter)) < quarter, -1.0, 1.0).astype(np.float32)
    cos = np.cos(ang).astype(np.float32)
    sin_signed = (np.sin(ang) * sign).astype(np.float32)
    reps = LANES // DH_A
    return jnp.asarray(np.tile(cos, (1, reps))), jnp.asarray(np.tile(sin_signed, (1, reps)))


def _group_major(w, axis):
    shape = w.shape
    split = shape[:axis] + (KV_B, G_B, DH_B) + shape[axis + 1:]
    return jnp.swapaxes(w.reshape(split), axis, axis + 1).reshape(shape)


def _prepare_weights(ffn_w_in, ffn_w_out, w_in, w_branch, w_out):
    qb_lo = 3 * N_QA
    qb_hi = qb_lo + H_B * DH_B
    w_proj = jnp.concatenate(
        [w_in[:, :, :qb_lo], _group_major(w_in[:, :, qb_lo:qb_hi], 2), w_in[:, :, qb_hi:N_PROJ]],
        axis=2).astype(BF16)
    w_gate = w_in[:, :, N_PROJ:].astype(BF16)
    wb = jnp.concatenate(
        [w_branch[:, 0:1], _group_major(w_branch[:, 1:2], 2), w_branch[:, 2:3]], axis=1).astype(BF16)
    return ffn_w_in.astype(BF16), ffn_w_out.astype(BF16), w_proj, w_gate, wb, w_out.astype(BF16)


def kernel(x_prompt, x_sample, cache_diff_k, cache_diff_v, cache_win_k, cache_win_v, c, c_ctx,
           w_mod, b_mod, norm_g, ffn_w_in, ffn_w_out, w_in, diff_lambda, diff_subln_g, win_sink,
           conv_dw_w, conv_dw_b, conv_norm_g, conv_norm_b, w_branch, w_out, final_norm_g):
    n_ctx_b, n_ctx_t, D = x_prompt.shape
    n_lat_b, n_lat_t, _ = x_sample.shape
    past = cache_diff_k.shape[2]
    tm = 512

    n_rows = 1 + n_lat_b
    pad_rows = -n_rows % 8
    cond = jnp.concatenate([c_ctx[None, :], c, jnp.zeros((pad_rows, D), F32)], axis=0)
    mod_all = _modulation(cond, w_mod, b_mod.reshape(DEPTH, 1, 9 * D))

    rope_tables = _rope_tables(n_lat_t)
    ck_a = cache_diff_k.reshape(n_lat_b, DEPTH, past, N_QA)
    cv_a = cache_diff_v.reshape(n_lat_b, DEPTH, past, N_QA)
    ck_b = cache_win_k.reshape(n_lat_b, DEPTH, past, N_KVB)
    cv_b = cache_win_v.reshape(n_lat_b, DEPTH, past, N_KVB)
    final_g = final_norm_g.reshape(1, D)

    xp = x_prompt.reshape(1, n_ctx_b * n_ctx_t, D)
    xs = x_sample
    new_cache = [[], [], [], []]
    ffn_in, ffn_out, w_proj, w_gate, wb, wo = _prepare_weights(ffn_w_in, ffn_w_out, w_in, w_branch, w_out)
    for l in range(DEPTH):
        lam_init = 0.8 - 0.6 * math.exp(-0.3 * l)
        g_l = norm_g[l]
        lam = diff_lambda[l]
        subln = diff_subln_g[l].reshape(1, 2 * DH_A)
        sink = win_sink[l]
        conv_p = (conv_dw_w[l], conv_dw_b[l].reshape(1, C_CONV),
                  conv_norm_g[l].reshape(1, C_CONV), conv_norm_b[l].reshape(1, C_CONV))
        mod_ctx = mod_all[l, 0:1].reshape(1, 9, D)
        mod_lat = mod_all[l, 1:n_rows].reshape(n_lat_b, 9, D)
        last = l == DEPTH - 1

        h = _ffn(xp, mod_ctx, g_l, ffn_in, ffn_out, (l, 0), mod_row=0, g_row=0, tm=tm)
        qa, ka, va, qb, kb, vb, u = _inproj(h, mod_ctx, g_l, w_proj, l, tm=tm, kv_dtype=F32)
        seq = lambda a: a.reshape(n_ctx_b, n_ctx_t, a.shape[-1])
        ka, va, kb, vb = seq(ka), seq(va), seq(kb), seq(vb)
        ya = _diff_attn(seq(qa), ka, va, lam, subln, lam_init=lam_init, tq=n_ctx_t)
        yb = _win_attn(seq(qb), kb, vb, sink, tq=n_ctx_t, nq=1, local=False)
        yc = _conv_module(seq(u), *conv_p, rt=n_ctx_t)
        flat = lambda a: a.reshape(1, n_ctx_b * n_ctx_t, a.shape[-1])
        h = _merge(h, mod_ctx, g_l, w_gate, flat(ya), flat(yb), flat(yc), wb, wo, l, tm=tm)
        xp = _ffn(h, mod_ctx, g_l, ffn_in, ffn_out, (l, 1), mod_row=6, g_row=2, tm=tm,
                  final_g=final_g if last else None)
        for dst, val in zip(new_cache, (ka, va, kb, vb)):
            dst.append(val)

        h = _ffn(xs, mod_lat, g_l, ffn_in, ffn_out, (l, 0), mod_row=0, g_row=0, tm=tm)
        qa, ka, va, qb, kb, vb, u = _inproj(h, mod_lat, g_l, w_proj, l, tm=tm, kv_dtype=BF16,
                                            rope_tables=rope_tables)
        ya = _diff_attn(qa, ka, va, lam, subln, lam_init=lam_init, tq=256, cache=(ck_a, cv_a, l))
        yb = _win_attn(qb, kb, vb, sink, tq=WIN_BLOCK, nq=4, local=True, cache=(ck_b, cv_b, l))
        yc = _conv_module(u, *conv_p, rt=256)
        h = _merge(h, mod_lat, g_l, w_gate, ya, yb, yc, wb, wo, l, tm=tm)
        xs = _ffn(h, mod_lat, g_l, ffn_in, ffn_out, (l, 1), mod_row=6, g_row=2, tm=tm,
                  final_g=final_g if last else None)

    y_prompt = xp.reshape(n_ctx_b, n_ctx_t, D)
    new_diff_k = jnp.stack(new_cache[0], axis=1).reshape(n_ctx_b, DEPTH, n_ctx_t, H_A, 2, DH_A)
    new_diff_v = jnp.stack(new_cache[1], axis=1).reshape(n_ctx_b, DEPTH, n_ctx_t, H_A, 2 * DH_A)
    new_win_k = jnp.stack(new_cache[2], axis=1).reshape(n_ctx_b, DEPTH, n_ctx_t, KV_B, DH_B)
    new_win_v = jnp.stack(new_cache[3], axis=1).reshape(n_ctx_b, DEPTH, n_ctx_t, KV_B, DH_B)
    return (y_prompt, xs, new_diff_k, new_diff_v, new_win_k, new_win_v)
```

```python
import functools
import math

import numpy as np
import jax
import jax.numpy as jnp
from jax import lax
from jax.experimental import pallas as pl
from jax.experimental.pallas import tpu as pltpu

D_MODEL = 1024
DEPTH = 2
GRID_W = 64
H_A = 4
DH_A = 64
H_B = 8
KV_B = 2
G_B = H_B // KV_B
DH_B = 64
WINDOW = 128
WIN_BLOCK = 128
C_CONV = 512
CONV_K = 31
BRANCH_W = 512
D_FF = 2816
ROPE_BASE = 10000.0
NORM_EPS = 1e-6
SUBLN_EPS = 1e-5
SCALE_A = DH_A ** -0.5
SCALE_B = DH_B ** -0.5
LOG2E = math.log2(math.e)

N_QA = H_A * 2 * DH_A
N_KVB = KV_B * DH_B
N_ATTN = 3 * N_QA + H_B * DH_B + 2 * N_KVB
N_PROJ = N_ATTN + 2 * C_CONV

LANES = 128
SUBLANES = 8
ROPE_HALF = DH_A // 4
FF_CHUNK = 256
KEY_CHUNK = 256
CONV_HALO = 16
CONV_ROWS = 32
VMEM_LIMIT = 56 * 1024 * 1024

BF16 = jnp.bfloat16
F32 = jnp.float32


def _params(*sem):
    return pltpu.CompilerParams(dimension_semantics=sem, vmem_limit_bytes=VMEM_LIMIT)


def _resident(shape):
    nd = len(shape)
    return pl.BlockSpec(shape, lambda *_: (0,) * nd, pipeline_mode=pl.Buffered(1))


def _resident_slice(shape, lead):
    n = len(lead)
    block = (None,) * n + tuple(shape[n:])
    index = tuple(lead) + (0,) * (len(shape) - n)
    return pl.BlockSpec(block, lambda *_: index, pipeline_mode=pl.Buffered(1))


def _dot(a, b):
    return jnp.dot(a, b, preferred_element_type=F32)


def _dot_nt(a, b):
    return lax.dot_general(a, b, (((1,), (1,)), ((), ())), preferred_element_type=F32)


def _rms(x, eps):
    return x * lax.rsqrt(jnp.mean(x * x, axis=-1, keepdims=True) + eps)


def _mod_rms(x, g, shift, scale):
    return _rms(x, NORM_EPS) * g * (1.0 + scale) + shift


def _silu(x):
    return x * jax.nn.sigmoid(x)


def _mod_kernel(c_ref, w_ref, b_ref, o_ref):
    a = _silu(c_ref[...]).astype(BF16)
    o_ref[...] = _dot(a, w_ref[...].astype(BF16)) + b_ref[...]


def _modulation(cond, w_mod, b_mod):
    R = cond.shape[0]
    tn = D_MODEL
    return pl.pallas_call(
        _mod_kernel,
        grid=(DEPTH, 9 * D_MODEL // tn),
        in_specs=[
            pl.BlockSpec((R, D_MODEL), lambda l, j: (0, 0)),
            pl.BlockSpec((None, D_MODEL, tn), lambda l, j: (l, 0, j)),
            pl.BlockSpec((None, 1, tn), lambda l, j: (l, 0, j)),
        ],
        out_specs=pl.BlockSpec((None, R, tn), lambda l, j: (l, 0, j)),
        out_shape=jax.ShapeDtypeStruct((DEPTH, R, 9 * D_MODEL), F32),
        compiler_params=_params("parallel", "parallel"),
        name="modulation",
    )(cond, w_mod, b_mod)


def _ffn_kernel(x_ref, mod_ref, g_ref, win_ref, wout_ref, *rest, mod_row, g_row, final):
    if final:
        fg_ref, o_ref = rest
    else:
        (o_ref,) = rest
    x = x_ref[...]
    shift = mod_ref[mod_row:mod_row + 1, :]
    scale = mod_ref[mod_row + 1:mod_row + 2, :]
    gate = mod_ref[mod_row + 2:mod_row + 3, :]
    nb = _mod_rms(x, g_ref[g_row:g_row + 1, :], shift, scale).astype(BF16)
    acc = None
    for lo in range(0, D_FF, FF_CHUNK):
        a = _dot(nb, win_ref[:, lo:lo + FF_CHUNK])
        b = _dot(nb, win_ref[:, D_FF + lo:D_FF + lo + FF_CHUNK])
        hid = (_silu(a) * b).astype(BF16)
        part = _dot(hid, wout_ref[lo:lo + FF_CHUNK, :])
        acc = part if acc is None else acc + part
    h = x + 0.5 * gate * acc
    if final:
        h = _rms(h, NORM_EPS) * fg_ref[...]
    o_ref[...] = h


def _ffn(x, mod, norm_g, w_in, w_out, which, *, mod_row, g_row, tm, final_g=None):
    B, T, D = x.shape
    per_batch = mod.shape[0] > 1
    final = final_g is not None
    l, s = which
    in_specs = [
        pl.BlockSpec((None, tm, D), lambda b, i: (b, i, 0)),
        pl.BlockSpec((None, 9, D), (lambda b, i: (b, 0, 0)) if per_batch else (lambda b, i: (0, 0, 0))),
        _resident(norm_g.shape),
        _resident_slice(w_in.shape, (l, s)),
        _resident_slice(w_out.shape, (l, s)),
    ]
    args = [x, mod, norm_g, w_in, w_out]
    if final:
        in_specs.append(_resident(final_g.shape))
        args.append(final_g)
    return pl.pallas_call(
        functools.partial(_ffn_kernel, mod_row=mod_row, g_row=g_row, final=final),
        grid=(B, T // tm),
        in_specs=in_specs,
        out_specs=pl.BlockSpec((None, tm, D), lambda b, i: (b, i, 0)),
        out_shape=jax.ShapeDtypeStruct((B, T, D), F32),
        compiler_params=_params("parallel", "parallel"),
        name="ffn",
    )(*args)


def _rope(x, cos, sin_signed, lo_half):
    outs = []
    for j in range(x.shape[1] // LANES):
        xj = x[:, j * LANES:(j + 1) * LANES]
        rot = jnp.where(lo_half, pltpu.roll(xj, LANES - ROPE_HALF, 1), pltpu.roll(xj, ROPE_HALF, 1))
        outs.append(xj * cos + rot * sin_signed)
    return outs[0] if len(outs) == 1 else jnp.concatenate(outs, axis=1)


def _inproj_kernel(x_ref, mod_ref, g_ref, w_ref, *rest, rope):
    if rope:
        cos_ref, sin_ref = rest[:2]
        rest = rest[2:]
    qa_ref, ka_ref, va_ref, qb_ref, kb_ref, vb_ref, u_ref = rest
    x = x_ref[...]
    nb = _mod_rms(x, g_ref[1:2, :], mod_ref[3:4, :], mod_ref[4:5, :]).astype(BF16)

    def proj(lo, hi):
        return _dot(nb, w_ref[:, lo:hi])

    qa = proj(0, N_QA)
    ka = proj(N_QA, 2 * N_QA)
    va = proj(2 * N_QA, 3 * N_QA)
    qb = proj(3 * N_QA, 4 * N_QA)
    kvb = proj(4 * N_QA, N_ATTN)
    kb = kvb[:, :N_KVB]
    vb = kvb[:, N_KVB:]
    cu = proj(N_ATTN, N_PROJ)
    if rope:
        cos = cos_ref[...]
        sin_signed = sin_ref[...]
        lane = lax.broadcasted_iota(jnp.int32, cos.shape, 1)
        lo_half = (lane % (2 * ROPE_HALF)) < ROPE_HALF
        qa = _rope(qa, cos, sin_signed, lo_half)
        ka = _rope(ka, cos, sin_signed, lo_half)
        qb = _rope(qb, cos, sin_signed, lo_half)
        kb = _rope(kb, cos, sin_signed, lo_half)
    qa_ref[...] = (qa * (SCALE_A * LOG2E)).astype(qa_ref.dtype)
    ka_ref[...] = ka.astype(ka_ref.dtype)
    va_ref[...] = va.astype(va_ref.dtype)
    qb_ref[...] = (qb * (SCALE_B * LOG2E)).astype(qb_ref.dtype)
    kb_ref[...] = kb.astype(kb_ref.dtype)
    vb_ref[...] = vb.astype(vb_ref.dtype)
    u_ref[...] = cu[:, :C_CONV] * jax.nn.sigmoid(cu[:, C_CONV:])


def _inproj(h, mod, norm_g, w_proj, layer, *, tm, kv_dtype, rope_tables=None):
    B, T, D = h.shape
    per_batch = mod.shape[0] > 1
    rope = rope_tables is not None
    row = lambda b, i: (b, i, 0)
    in_specs = [
        pl.BlockSpec((None, tm, D), row),
        pl.BlockSpec((None, 9, D), (lambda b, i: (b, 0, 0)) if per_batch else (lambda b, i: (0, 0, 0))),
        _resident(norm_g.shape),
        _resident_slice(w_proj.shape, (layer,)),
    ]
    args = [h, mod, norm_g, w_proj]
    if rope:
        in_specs += [pl.BlockSpec((tm, LANES), lambda b, i: (i, 0))] * 2
        args += list(rope_tables)
    widths = (N_QA, N_QA, N_QA, H_B * DH_B, N_KVB, N_KVB, C_CONV)
    dtypes = (BF16, kv_dtype, kv_dtype, BF16, kv_dtype, kv_dtype, F32)
    return pl.pallas_call(
        functools.partial(_inproj_kernel, rope=rope),
        grid=(B, T // tm),
        in_specs=in_specs,
        out_specs=[pl.BlockSpec((None, tm, w), row) for w in widths],
        out_shape=[jax.ShapeDtypeStruct((B, T, w), dt) for w, dt in zip(widths, dtypes)],
        compiler_params=_params("parallel", "parallel"),
        name="inproj",
    )(*args)


def _diff_attn_kernel(*refs, n_cache, lam_init, nq):
    if n_cache:
        q_ref, k_ref, v_ref, ck_ref, cv_ref, lam_ref, g_ref, o_ref, kbuf, vbuf, s_even, s_odd, m_even, m_odd = refs
    else:
        q_ref, k_ref, v_ref, lam_ref, g_ref, o_ref, kbuf, vbuf, s_even, s_odd, m_even, m_odd = refs
    hd = 2 * DH_A
    t = pl.program_id(0)

    @pl.when(t == 0)
    def _():
        s_odd[...] = jnp.zeros(s_odd.shape, F32)
        m_odd[...] = jnp.zeros(m_odd.shape, F32)
        vbuf[:, 0:hd] = jnp.zeros((vbuf.shape[0], hd), BF16)
        vbuf[:, hd:] = jnp.ones((vbuf.shape[0], vbuf.shape[1] - hd), BF16)

    @pl.when(t % nq == 0)
    def _():
        if n_cache:
            kbuf[0:n_cache, :] = ck_ref[...].astype(BF16)
        kbuf[n_cache:, :] = k_ref[...].astype(BF16)

    @pl.when((t > 0) & ((t - 1) % nq == 0))
    def _():
        if n_cache:
            vbuf[0:n_cache, 0:hd] = cv_ref[...].astype(BF16)
        vbuf[n_cache:, 0:hd] = v_ref[...].astype(BF16)

    def step(s_write, m_write, s_read, m_read):
        q = q_ref[...]
        tq = q.shape[0]
        lane = lax.broadcasted_iota(jnp.int32, q.shape, 1)
        zero = jnp.zeros_like(q)
        q2 = jnp.concatenate([jnp.where(lane < DH_A, q, zero), jnp.where(lane >= DH_A, q, zero)], axis=0)
        row_max = jnp.concatenate([m_read[...]] * (KEY_CHUNK // LANES), axis=1)
        acc = None
        run_max = None
        for lo in range(0, kbuf.shape[0], KEY_CHUNK):
            cols = slice(lo, lo + KEY_CHUNK)
            s = _dot_nt(q2, kbuf[cols, :])
            s_write[:, cols] = s
            run_max = s if run_max is None else jnp.maximum(run_max, s)
            e = jnp.exp2(s_read[:, cols] - row_max).astype(BF16)
            part = _dot(e, vbuf[cols, :])
            acc = part if acc is None else acc + part
        m_write[...] = jnp.broadcast_to(jnp.max(run_max, axis=1, keepdims=True), m_write.shape)
        lam = lam_ref[...]
        lam_full = (jnp.exp(jnp.sum(lam[0:1] * lam[1:2], axis=1, keepdims=True))
                    - jnp.exp(jnp.sum(lam[2:3] * lam[3:4], axis=1, keepdims=True)) + lam_init)
        ratio = acc[:, :hd] / acc[:, hd:]
        o = ratio[:tq] - lam_full * ratio[tq:]
        o = _rms(o, SUBLN_EPS) * g_ref[...] * (1.0 - lam_init)
        o_ref[...] = o.astype(o_ref.dtype)

    pl.when(t % 2 == 0)(lambda: step(s_even, m_even, s_odd, m_odd))
    pl.when(t % 2 == 1)(lambda: step(s_odd, m_odd, s_even, m_even))


def _diff_attn(q, k, v, lam, subln_g, *, lam_init, tq, cache=None):
    B, T, _ = q.shape
    n_cache = 0 if cache is None else cache[0].shape[2]
    hd = 2 * DH_A
    nq = T // tq
    n_blocks = B * H_A * nq

    def block_of(t):
        blk = jnp.clip(t, 0, n_blocks - 1)
        return blk // (H_A * nq), (blk // nq) % H_A, blk % nq

    def q_map(t):
        b, h, i = block_of(t)
        return b, i, h

    def kv_map(t):
        b, h, _ = block_of(t)
        return b, 0, h

    def out_map(t):
        return q_map(t - 1)

    in_specs = [
        pl.BlockSpec((None, tq, hd), q_map),
        pl.BlockSpec((None, T, hd), kv_map),
        pl.BlockSpec((None, T, hd), lambda t: kv_map(t - 1)),
    ]
    args = [q, k, v]
    if cache is not None:
        ck, cv, layer = cache

        def cache_map(t):
            b, h, _ = block_of(t)
            return b, layer, 0, h

        in_specs += [pl.BlockSpec((None, None, n_cache, hd), cache_map),
                     pl.BlockSpec((None, None, n_cache, hd), lambda t: cache_map(t - 1))]
        args += [ck, cv]
    in_specs += [_resident(lam.shape), _resident(subln_g.shape)]
    args += [lam, subln_g]
    n_keys = n_cache + T
    return pl.pallas_call(
        functools.partial(_diff_attn_kernel, n_cache=n_cache, lam_init=lam_init, nq=nq),
        grid=(n_blocks + 1,),
        in_specs=in_specs,
        out_specs=pl.BlockSpec((None, tq, hd), out_map),
        out_shape=jax.ShapeDtypeStruct((B, T, N_QA), BF16),
        scratch_shapes=[pltpu.VMEM((n_keys, hd), BF16), pltpu.VMEM((n_keys, 2 * hd), BF16),
                        pltpu.VMEM((2 * tq, n_keys), F32), pltpu.VMEM((2 * tq, n_keys), F32),
                        pltpu.VMEM((2 * tq, LANES), F32), pltpu.VMEM((2 * tq, LANES), F32)],
        compiler_params=_params("arbitrary"),
        name="diff_attn",
    )(*args)


def _win_attn_kernel(*refs, local, has_ctx, seq_len, tq, nq):
    if has_ctx:
        q_ref, k_ref, v_ref, ck_ref, cv_ref, sink_ref, o_ref = refs
        ck = ck_ref[...].astype(BF16)
        cv = cv_ref[...].astype(BF16)
    else:
        q_ref, k_ref, v_ref, sink_ref, o_ref = refs
    n_loc = 3 * WIN_BLOCK
    lane = lax.broadcasted_iota(jnp.int32, (tq, LANES), 1)
    zero = jnp.zeros((tq, LANES), BF16)
    for qi in range(nq):
        rows_q = slice(qi * tq, (qi + 1) * tq)
        q = q_ref[rows_q, :]
        if local:
            n = pl.program_id(1) * nq + qi
            start = pl.multiple_of(jnp.clip((n - 1) * WIN_BLOCK, 0, seq_len - n_loc), WIN_BLOCK)
            k = k_ref[pl.ds(start, n_loc), :].astype(BF16)
            v = v_ref[pl.ds(start, n_loc), :].astype(BF16)
        else:
            k = k_ref[...].astype(BF16)
            v = v_ref[...].astype(BF16)
        if has_ctx:
            k = jnp.concatenate([k, ck], axis=0)
            v = jnp.concatenate([v, cv], axis=0)
        v_ones = jnp.concatenate([v, jnp.ones(v.shape, BF16)], axis=1)
        if local:
            rows = lax.broadcasted_iota(jnp.int32, (tq, n_loc), 0)
            cols = lax.broadcasted_iota(jnp.int32, (tq, n_loc), 1)
            valid = jnp.abs(cols - rows - (n * WIN_BLOCK - start)) <= WINDOW
        halves = []
        for j in range(KV_B):
            sel = (lane >= j * DH_B) & (lane < (j + 1) * DH_B)
            qj = jnp.concatenate(
                [jnp.where(sel, q[:, g * LANES:(g + 1) * LANES], zero) for g in range(G_B)], axis=0)
            s = _dot_nt(qj, k)
            es, sink_terms = [], []
            for g in range(G_B):
                sg = s[g * tq:(g + 1) * tq]
                if local:
                    band = jnp.where(valid, sg[:, :n_loc], -jnp.inf)
                    sg = jnp.concatenate([band, sg[:, n_loc:]], axis=1) if has_ctx else band
                sink = sink_ref[j * G_B + g] * LOG2E
                m = jnp.maximum(jnp.max(sg, axis=1, keepdims=True), sink)
                es.append(jnp.exp2(sg - m).astype(BF16))
                sink_terms.append(jnp.exp2(sink - m))
            halves.append((_dot(jnp.concatenate(es, axis=0), v_ones), sink_terms))
        for g in range(G_B):
            rows_g = slice(g * tq, (g + 1) * tq)
            outs = []
            for ov, sink_terms in halves:
                outs.append(ov[rows_g, :LANES] / (ov[rows_g, LANES:] + sink_terms[g]))
            og = jnp.where(lane < DH_B, outs[0], outs[1])
            o_ref[rows_q, g * LANES:(g + 1) * LANES] = og.astype(o_ref.dtype)


def _win_attn(q, k, v, sink, *, tq, nq, local, cache=None):
    B, T, _ = q.shape
    has_ctx = cache is not None
    assert not local or tq == WIN_BLOCK
    rows = nq * tq
    in_specs = [
        pl.BlockSpec((None, rows, G_B * LANES), lambda b, n: (b, n, 0)),
        pl.BlockSpec((None, T, N_KVB), lambda b, n: (b, 0, 0)),
        pl.BlockSpec((None, T, N_KVB), lambda b, n: (b, 0, 0)),
    ]
    args = [q, k, v]
    if has_ctx:
        ck, cv, layer = cache
        spec = pl.BlockSpec((None, None, ck.shape[2], N_KVB), lambda b, n: (b, layer, 0, 0))
        in_specs += [spec, spec]
        args += [ck, cv]
    in_specs.append(pl.BlockSpec(memory_space=pltpu.SMEM))
    args.append(sink)
    return pl.pallas_call(
        functools.partial(_win_attn_kernel, local=local, has_ctx=has_ctx, seq_len=T, tq=tq, nq=nq),
        grid=(B, T // rows),
        in_specs=in_specs,
        out_specs=pl.BlockSpec((None, rows, G_B * LANES), lambda b, n: (b, n, 0)),
        out_shape=jax.ShapeDtypeStruct((B, T, H_B * DH_B), BF16),
        compiler_params=_params("parallel", "parallel"),
        name="win_attn",
    )(*args)


def _conv_kernel(prev_ref, cur_ref, next_ref, w_ref, bdw_ref, g_ref, b_ref, o_ref, win_ref, ph_ref):
    i = pl.program_id(1)
    rt = cur_ref.shape[0]
    zeros = jnp.zeros((CONV_HALO, C_CONV), F32)
    win_ref[0:CONV_HALO, :] = jnp.where(i > 0, prev_ref[...], zeros)
    win_ref[CONV_HALO:CONV_HALO + rt, :] = cur_ref[...]
    win_ref[CONV_HALO + rt:, :] = jnp.where(i < pl.num_programs(1) - 1, next_ref[...], zeros)
    n_ph = ph_ref.shape[1]
    for p in range(1, SUBLANES):
        ph_ref[p - 1] = win_ref[p:p + n_ph, :]
    first = CONV_HALO - CONV_K // 2
    tiles = CONV_ROWS // SUBLANES
    for r0 in range(0, rt, CONV_ROWS):
        acc = jnp.zeros((tiles, SUBLANES, C_CONV), F32) + bdw_ref[...]
        for t in range(CONV_K):
            p = (first + t) % SUBLANES
            base = r0 + first + t - p
            x = win_ref[base:base + CONV_ROWS, :] if p == 0 else ph_ref[p - 1, base:base + CONV_ROWS, :]
            acc = acc + x.reshape(tiles, SUBLANES, C_CONV) * w_ref[t]
        acc = acc.reshape(CONV_ROWS, C_CONV)
        mu = jnp.mean(acc, axis=-1, keepdims=True)
        cen = acc - mu
        var = jnp.mean(cen * cen, axis=-1, keepdims=True)
        y = cen * lax.rsqrt(var + NORM_EPS) * g_ref[...] + b_ref[...]
        o_ref[r0:r0 + CONV_ROWS, :] = _silu(y).astype(o_ref.dtype)


def _conv_module(u, w_dw, b_dw, g, b, *, rt):
    B, T, C = u.shape
    hb = rt // CONV_HALO
    last = T // CONV_HALO - 1
    n_ph = rt + (CONV_HALO + CONV_K // 2) // SUBLANES * SUBLANES
    w_tiles = jnp.broadcast_to(w_dw[:, None, :], (CONV_K, SUBLANES, C))
    in_specs = [
        pl.BlockSpec((None, CONV_HALO, C), lambda bb, i: (bb, jnp.maximum(i * hb - 1, 0), 0)),
        pl.BlockSpec((None, rt, C), lambda bb, i: (bb, i, 0)),
        pl.BlockSpec((None, CONV_HALO, C), lambda bb, i: (bb, jnp.minimum((i + 1) * hb, last), 0)),
        _resident(w_tiles.shape), _resident(b_dw.shape), _resident(g.shape), _resident(b.shape),
    ]
    return pl.pallas_call(
        _conv_kernel,
        grid=(B, T // rt),
        in_specs=in_specs,
        out_specs=pl.BlockSpec((None, rt, C), lambda bb, i: (bb, i, 0)),
        out_shape=jax.ShapeDtypeStruct((B, T, C), BF16),
        scratch_shapes=[pltpu.VMEM((rt + 2 * CONV_HALO, C), F32),
                        pltpu.VMEM((SUBLANES - 1, n_ph, C), F32)],
        compiler_params=_params("parallel", "parallel"),
        name="conv_module",
    )(u, u, u, w_tiles, b_dw, g, b)


def _merge_kernel(h_ref, mod_ref, g_ref, wg_ref, ya_ref, yb_ref, yc_ref, wb_ref, wo_ref, o_ref):
    h = h_ref[...]
    nb = _mod_rms(h, g_ref[1:2, :], mod_ref[3:4, :], mod_ref[4:5, :]).astype(BF16)
    merged = None
    for i, y_ref in enumerate((ya_ref, yb_ref, yc_ref)):
        gate = jax.nn.sigmoid(_dot(nb, wg_ref[:, i * D_MODEL:(i + 1) * D_MODEL]))
        term = gate * _dot(y_ref[...], wb_ref[i])
        merged = term if merged is None else merged + term
    o_ref[...] = h + mod_ref[5:6, :] * _dot(merged.astype(BF16), wo_ref[...])


def _merge(h, mod, norm_g, w_gate, ya, yb, yc, w_branch, w_out, layer, *, tm):
    B, T, D = h.shape
    per_batch = mod.shape[0] > 1
    row = lambda b, i: (b, i, 0)
    y_spec = pl.BlockSpec((None, tm, BRANCH_W), row)
    return pl.pallas_call(
        _merge_kernel,
        grid=(B, T // tm),
        in_specs=[
            pl.BlockSpec((None, tm, D), row),
            pl.BlockSpec((None, 9, D), (lambda b, i: (b, 0, 0)) if per_batch else (lambda b, i: (0, 0, 0))),
            _resident(norm_g.shape), _resident_slice(w_gate.shape, (layer,)),
            y_spec, y_spec, y_spec,
            _resident_slice(w_branch.shape, (layer,)), _resident_slice(w_out.shape, (layer,)),
        ],
        out_specs=pl.BlockSpec((None, tm, D), row),
        out_shape=jax.ShapeDtypeStruct((B, T, D), F32),
        compiler_params=_params("parallel", "parallel"),
        name="merge",
    )(h, mod, norm_g, w_gate, ya, yb, yc, w_branch, w_out)


def _rope_tables(n_tokens):
    quarter = DH_A // 4
    t = np.arange(n_tokens)
    inv = (ROPE_BASE ** (-np.arange(0, DH_A // 2, 2, dtype=np.float32) / (DH_A // 2))).astype(np.float32)
    pos = np.stack([(t // GRID_W).astype(np.float32), (t % GRID_W).astype(np.float32)], axis=1)
    ang = (pos[:, :, None] * inv[None, None, :]).astype(np.float32)
    ang = np.broadcast_to(ang[:, :, None, :], (n_tokens, 2, 2, quarter)).reshape(n_tokens, DH_A)
    sign = np.where((np.arange(DH_A) % (2 * quarter)) < quarter, -1.0, 1.0).astype(np.float32)
    cos = np.cos(ang).astype(np.float32)
    sin_signed = (np.sin(ang) * sign).astype(np.float32)
    reps = LANES // DH_A
    return jnp.asarray(np.tile(cos, (1, reps))), jnp.asarray(np.tile(sin_signed, (1, reps)))


def _group_major(w, axis):
    shape = w.shape
    split = shape[:axis] + (KV_B, G_B, DH_B) + shape[axis + 1:]
    return jnp.swapaxes(w.reshape(split), axis, axis + 1).reshape(shape)


def _prepare_weights(ffn_w_in, ffn_w_out, w_in, w_branch, w_out):
    qb_lo = 3 * N_QA
    qb_hi = qb_lo + H_B * DH_B
    w_proj = jnp.concatenate(
        [w_in[:, :, :qb_lo], _group_major(w_in[:, :, qb_lo:qb_hi], 2), w_in[:, :, qb_hi:N_PROJ]],
        axis=2).astype(BF16)
    w_gate = w_in[:, :, N_PROJ:].astype(BF16)
    wb = jnp.concatenate(
        [w_branch[:, 0:1], _group_major(w_branch[:, 1:2], 2), w_branch[:, 2:3]], axis=1).astype(BF16)
    return ffn_w_in.astype(BF16), ffn_w_out.astype(BF16), w_proj, w_gate, wb, w_out.astype(BF16)


def kernel(x_prompt, x_sample, cache_diff_k, cache_diff_v, cache_win_k, cache_win_v, c, c_ctx,
           w_mod, b_mod, norm_g, ffn_w_in, ffn_w_out, w_in, diff_lambda, diff_subln_g, win_sink,
           conv_dw_w, conv_dw_b, conv_norm_g, conv_norm_b, w_branch, w_out, final_norm_g):
    n_ctx_b, n_ctx_t, D = x_prompt.shape
    n_lat_b, n_lat_t, _ = x_sample.shape
    past = cache_diff_k.shape[2]
    tm = 512

    n_rows = 1 + n_lat_b
    pad_rows = -n_rows % 8
    cond = jnp.concatenate([c_ctx[None, :], c, jnp.zeros((pad_rows, D), F32)], axis=0)
    mod_all = _modulation(cond, w_mod, b_mod.reshape(DEPTH, 1, 9 * D))

    rope_tables = _rope_tables(n_lat_t)
    ck_a = cache_diff_k.reshape(n_lat_b, DEPTH, past, N_QA)
    cv_a = cache_diff_v.reshape(n_lat_b, DEPTH, past, N_QA)
    ck_b = cache_win_k.reshape(n_lat_b, DEPTH, past, N_KVB)
    cv_b = cache_win_v.reshape(n_lat_b, DEPTH, past, N_KVB)
    final_g = final_norm_g.reshape(1, D)

    xp = x_prompt.reshape(1, n_ctx_b * n_ctx_t, D)
    xs = x_sample
    new_cache = [[], [], [], []]
    ffn_in, ffn_out, w_proj, w_gate, wb, wo = _prepare_weights(ffn_w_in, ffn_w_out, w_in, w_branch, w_out)
    for l in range(DEPTH):
        lam_init = 0.8 - 0.6 * math.exp(-0.3 * l)
        g_l = norm_g[l]
        lam = diff_lambda[l]
        subln = diff_subln_g[l].reshape(1, 2 * DH_A)
        sink = win_sink[l]
        conv_p = (conv_dw_w[l], conv_dw_b[l].reshape(1, C_CONV),
                  conv_norm_g[l].reshape(1, C_CONV), conv_norm_b[l].reshape(1, C_CONV))
        mod_ctx = mod_all[l, 0:1].reshape(1, 9, D)
        mod_lat = mod_all[l, 1:n_rows].reshape(n_lat_b, 9, D)
        last = l == DEPTH - 1

        h = _ffn(xp, mod_ctx, g_l, ffn_in, ffn_out, (l, 0), mod_row=0, g_row=0, tm=tm)
        qa, ka, va, qb, kb, vb, u = _inproj(h, mod_ctx, g_l, w_proj, l, tm=tm, kv_dtype=F32)
        seq = lambda a: a.reshape(n_ctx_b, n_ctx_t, a.shape[-1])
        ka, va, kb, vb = seq(ka), seq(va), seq(kb), seq(vb)
        ya = _diff_attn(seq(qa), ka, va, lam, subln, lam_init=lam_init, tq=n_ctx_t)
        yb = _win_attn(seq(qb), kb, vb, sink, tq=n_ctx_t, nq=1, local=False)
        yc = _conv_module(seq(u), *conv_p, rt=n_ctx_t)
        flat = lambda a: a.reshape(1, n_ctx_b * n_ctx_t, a.shape[-1])
        h = _merge(h, mod_ctx, g_l, w_gate, flat(ya), flat(yb), flat(yc), wb, wo, l, tm=tm)
        xp = _ffn(h, mod_ctx, g_l, ffn_in, ffn_out, (l, 1), mod_row=6, g_row=2, tm=tm,
                  final_g=final_g if last else None)
        for dst, val in zip(new_cache, (ka, va, kb, vb)):
            dst.append(val)

        h = _ffn(xs, mod_lat, g_l, ffn_in, ffn_out, (l, 0), mod_row=0, g_row=0, tm=tm)
        qa, ka, va, qb, kb, vb, u = _inproj(h, mod_lat, g_l, w_proj, l, tm=tm, kv_dtype=BF16,
                                            rope_tables=rope_tables)
        ya = _diff_attn(qa, ka, va, lam, subln, lam_init=lam_init, tq=512, cache=(ck_a, cv_a, l))
        yb = _win_attn(qb, kb, vb, sink, tq=WIN_BLOCK, nq=4, local=True, cache=(ck_b, cv_b, l))
        yc = _conv_module(u, *conv_p, rt=256)
        h = _merge(h, mod_lat, g_l, w_gate, ya, yb, yc, wb, wo, l, tm=tm)
        xs = _ffn(h, mod_lat, g_l, ffn_in, ffn_out, (l, 1), mod_row=6, g_row=2, tm=tm,
                  final_g=final_g if last else None)

    y_prompt = xp.reshape(n_ctx_b, n_ctx_t, D)
    new_diff_k = jnp.stack(new_cache[0], axis=1).reshape(n_ctx_b, DEPTH, n_ctx_t, H_A, 2, DH_A)
    new_diff_v = jnp.stack(new_cache[1], axis=1).reshape(n_ctx_b, DEPTH, n_ctx_t, H_A, 2 * DH_A)
    new_win_k = jnp.stack(new_cache[2], axis=1).reshape(n_ctx_b, DEPTH, n_ctx_t, KV_B, DH_B)
    new_win_v = jnp.stack(new_cache[3], axis=1).reshape(n_ctx_b, DEPTH, n_ctx_t, KV_B, DH_B)
    return (y_prompt, xs, new_diff_k, new_diff_v, new_win_k, new_win_v)
```

```python
import functools
import itertools
import math

import numpy as np
import jax
import jax.numpy as jnp
from jax import lax
from jax.experimental import pallas as pl
from jax.experimental.pallas import tpu as pltpu

D_MODEL = 1024
DEPTH = 2
GRID_W = 64
H_A = 4
DH_A = 64
H_B = 8
KV_B = 2
G_B = H_B // KV_B
DH_B = 64
WINDOW = 128
WIN_BLOCK = 128
C_CONV = 512
CONV_K = 31
BRANCH_W = 512
D_FF = 2816
ROPE_BASE = 10000.0
NORM_EPS = 1e-6
SUBLN_EPS = 1e-5
SCALE_A = DH_A ** -0.5
SCALE_B = DH_B ** -0.5
LOG2E = math.log2(math.e)

N_QA = H_A * 2 * DH_A
N_KVB = KV_B * DH_B
N_ATTN = 3 * N_QA + H_B * DH_B + 2 * N_KVB
N_PROJ = N_ATTN + 2 * C_CONV

LANES = 128
SUBLANES = 8
ROPE_HALF = DH_A // 4
FF_CHUNK = 256
KEY_CHUNK = 256
CONV_HALO = 16
CONV_ROWS = 32
VMEM_LIMIT = 56 * 1024 * 1024

BF16 = jnp.bfloat16
F32 = jnp.float32


def _params(*sem):
    return pltpu.CompilerParams(dimension_semantics=sem, vmem_limit_bytes=VMEM_LIMIT)


def _resident(shape):
    nd = len(shape)
    return pl.BlockSpec(shape, lambda *_: (0,) * nd, pipeline_mode=pl.Buffered(1))


def _resident_slice(shape, lead):
    n = len(lead)
    block = (None,) * n + tuple(shape[n:])
    index = tuple(lead) + (0,) * (len(shape) - n)
    return pl.BlockSpec(block, lambda *_: index, pipeline_mode=pl.Buffered(1))


def _dot(a, b):
    return jnp.dot(a, b, preferred_element_type=F32)


def _dot_nt(a, b):
    return lax.dot_general(a, b, (((1,), (1,)), ((), ())), preferred_element_type=F32)


def _rms(x, eps):
    return x * lax.rsqrt(jnp.mean(x * x, axis=-1, keepdims=True) + eps)


def _mod_rms(x, g, shift, scale):
    return _rms(x, NORM_EPS) * g * (1.0 + scale) + shift


def _silu(x):
    return x * jax.nn.sigmoid(x)


def _mod_kernel(c_ref, w_ref, b_ref, o_ref):
    a = _silu(c_ref[...]).astype(BF16)
    o_ref[...] = _dot(a, w_ref[...].astype(BF16)) + b_ref[...]


def _modulation(cond, w_mod, b_mod):
    R = cond.shape[0]
    tn = D_MODEL
    return pl.pallas_call(
        _mod_kernel,
        grid=(DEPTH, 9 * D_MODEL // tn),
        in_specs=[
            pl.BlockSpec((R, D_MODEL), lambda l, j: (0, 0)),
            pl.BlockSpec((None, D_MODEL, tn), lambda l, j: (l, 0, j)),
            pl.BlockSpec((None, 1, tn), lambda l, j: (l, 0, j)),
        ],
        out_specs=pl.BlockSpec((None, R, tn), lambda l, j: (l, 0, j)),
        out_shape=jax.ShapeDtypeStruct((DEPTH, R, 9 * D_MODEL), F32),
        compiler_params=_params("parallel", "parallel"),
        name="modulation",
    )(cond, w_mod, b_mod)


def _ffn_kernel(x_ref, mod_ref, g_ref, win_ref, wout_ref, *rest, mod_row, g_row, final):
    if final:
        fg_ref, o_ref = rest
    else:
        (o_ref,) = rest
    x = x_ref[...]
    shift = mod_ref[mod_row:mod_row + 1, :]
    scale = mod_ref[mod_row + 1:mod_row + 2, :]
    gate = mod_ref[mod_row + 2:mod_row + 3, :]
    nb = _mod_rms(x, g_ref[g_row:g_row + 1, :], shift, scale).astype(BF16)
    acc = None
    for lo in range(0, D_FF, FF_CHUNK):
        a = _dot(nb, win_ref[:, lo:lo + FF_CHUNK])
        b = _dot(nb, win_ref[:, D_FF + lo:D_FF + lo + FF_CHUNK])
        hid = (_silu(a) * b).astype(BF16)
        part = _dot(hid, wout_ref[lo:lo + FF_CHUNK, :])
        acc = part if acc is None else acc + part
    h = x + 0.5 * gate * acc
    if final:
        h = _rms(h, NORM_EPS) * fg_ref[...]
    o_ref[...] = h


def _ffn(x, mod, norm_g, w_in, w_out, which, *, mod_row, g_row, tm, final_g=None):
    B, T, D = x.shape
    per_batch = mod.shape[0] > 1
    final = final_g is not None
    l, s = which
    in_specs = [
        pl.BlockSpec((None, tm, D), lambda b, i: (b, i, 0)),
        pl.BlockSpec((None, 9, D), (lambda b, i: (b, 0, 0)) if per_batch else (lambda b, i: (0, 0, 0))),
        _resident(norm_g.shape),
        _resident_slice(w_in.shape, (l, s)),
        _resident_slice(w_out.shape, (l, s)),
    ]
    args = [x, mod, norm_g, w_in, w_out]
    if final:
        in_specs.append(_resident(final_g.shape))
        args.append(final_g)
    return pl.pallas_call(
        functools.partial(_ffn_kernel, mod_row=mod_row, g_row=g_row, final=final),
        grid=(B, T // tm),
        in_specs=in_specs,
        out_specs=pl.BlockSpec((None, tm, D), lambda b, i: (b, i, 0)),
        out_shape=jax.ShapeDtypeStruct((B, T, D), F32),
        compiler_params=_params("parallel", "parallel"),
        name="ffn",
    )(*args)


def _rope(x, cos, sin_signed, lo_half):
    outs = []
    for j in range(x.shape[1] // LANES):
        xj = x[:, j * LANES:(j + 1) * LANES]
        rot = jnp.where(lo_half, pltpu.roll(xj, LANES - ROPE_HALF, 1), pltpu.roll(xj, ROPE_HALF, 1))
        outs.append(xj * cos + rot * sin_signed)
    return outs[0] if len(outs) == 1 else jnp.concatenate(outs, axis=1)


def _inproj_kernel(x_ref, mod_ref, g_ref, w_ref, *rest, rope, n_aliased, cache_out):
    if rope:
        cos_ref, sin_ref = rest[:2]
        rest = rest[2:]
    rest = rest[n_aliased:]
    qa_ref, ka_ref, va_ref, qb_ref, kb_ref, vb_ref, u_ref = rest[:7]
    x = x_ref[...]
    nb = _mod_rms(x, g_ref[1:2, :], mod_ref[3:4, :], mod_ref[4:5, :]).astype(BF16)

    def proj(lo, hi):
        return _dot(nb, w_ref[:, lo:hi])

    qa = proj(0, N_QA)
    ka = proj(N_QA, 2 * N_QA)
    va = proj(2 * N_QA, 3 * N_QA)
    qb = proj(3 * N_QA, 4 * N_QA)
    kvb = proj(4 * N_QA, N_ATTN)
    kb = kvb[:, :N_KVB]
    vb = kvb[:, N_KVB:]
    cu = proj(N_ATTN, N_PROJ)
    if rope:
        cos = cos_ref[...]
        sin_signed = sin_ref[...]
        lane = lax.broadcasted_iota(jnp.int32, cos.shape, 1)
        lo_half = (lane % (2 * ROPE_HALF)) < ROPE_HALF
        qa = _rope(qa, cos, sin_signed, lo_half)
        ka = _rope(ka, cos, sin_signed, lo_half)
        qb = _rope(qb, cos, sin_signed, lo_half)
        kb = _rope(kb, cos, sin_signed, lo_half)
    qa_ref[...] = (qa * (SCALE_A * LOG2E)).astype(qa_ref.dtype)
    ka_ref[...] = ka.astype(ka_ref.dtype)
    va_ref[...] = va.astype(va_ref.dtype)
    qb_ref[...] = (qb * (SCALE_B * LOG2E)).astype(qb_ref.dtype)
    kb_ref[...] = kb.astype(kb_ref.dtype)
    vb_ref[...] = vb.astype(vb_ref.dtype)
    u_ref[...] = cu[:, :C_CONV] * jax.nn.sigmoid(cu[:, C_CONV:])
    if cache_out:
        dk_ref, dv_ref, wk_ref, wv_ref = rest[7:]
        n_seq, _, _, seq_len = dk_ref.shape
        ka_t, kb_t, vb_t = ka.T, kb.T, vb.T
        for s in range(n_seq):
            tok = slice(s * seq_len, (s + 1) * seq_len)
            for hc in range(2 * H_A):
                dk_ref[s, hc] = ka_t[hc * DH_A:(hc + 1) * DH_A, tok]
            for hh in range(H_A):
                dv_ref[s, pl.ds(hh, seq_len, stride=H_A), :] = va[tok, hh * 2 * DH_A:(hh + 1) * 2 * DH_A]
            wk_ref[s] = kb_t[:, tok]
            wv_ref[s] = vb_t[:, tok]


def _inproj(h, mod, norm_g, w_proj, layer, *, tm, kv_dtype, rope_tables=None, cache_seq=None, cache_acc=None):
    B, T, D = h.shape
    per_batch = mod.shape[0] > 1
    rope = rope_tables is not None
    row = lambda b, i: (b, i, 0)
    in_specs = [
        pl.BlockSpec((None, tm, D), row),
        pl.BlockSpec((None, 9, D), (lambda b, i: (b, 0, 0)) if per_batch else (lambda b, i: (0, 0, 0))),
        _resident(norm_g.shape),
        _resident_slice(w_proj.shape, (layer,)),
    ]
    args = [h, mod, norm_g, w_proj]
    if rope:
        in_specs += [pl.BlockSpec((tm, LANES), lambda b, i: (i, 0))] * 2
        args += list(rope_tables)
    widths = (N_QA, N_QA, N_QA, H_B * DH_B, N_KVB, N_KVB, C_CONV)
    dtypes = (BF16, kv_dtype, kv_dtype, BF16, kv_dtype, kv_dtype, F32)
    out_specs = [pl.BlockSpec((None, tm, w), row) for w in widths]
    out_shape = [jax.ShapeDtypeStruct((B, T, w), dt) for w, dt in zip(widths, dtypes)]
    aliases = {}
    n_aliased = 0
    if cache_seq is not None:
        n_all, seq_len = cache_seq
        n_seq = tm // seq_len
        assert B == 1 and n_seq * seq_len == tm
        cache_shapes = ((2 * H_A, DH_A, seq_len), (seq_len * H_A, 2 * DH_A), (N_KVB, seq_len), (N_KVB, seq_len))
        for shape in cache_shapes:
            nd = len(shape)
            out_specs.append(pl.BlockSpec((n_seq, None) + shape,
                                          lambda b, i, nd=nd: (i, layer) + (0,) * nd))
            out_shape.append(jax.ShapeDtypeStruct((n_all, DEPTH) + shape, F32))
        if cache_acc is not None:
            n_aliased = len(cache_acc)
            aliases = {len(args) + j: len(widths) + j for j in range(n_aliased)}
            in_specs += [pl.BlockSpec(memory_space=pl.ANY)] * n_aliased
            args += list(cache_acc)
    return pl.pallas_call(
        functools.partial(_inproj_kernel, rope=rope, n_aliased=n_aliased, cache_out=cache_seq is not None),
        grid=(B, T // tm),
        in_specs=in_specs,
        out_specs=out_specs,
        out_shape=out_shape,
        input_output_aliases=aliases,
        compiler_params=_params("parallel", "parallel"),
        name="inproj",
    )(*args)


def _diff_attn_kernel(*refs, n_cache, lam_init, nq):
    if n_cache:
        q_ref, k_ref, v_ref, ck_ref, cv_ref, lam_ref, g_ref, o_ref, kbuf, vbuf, s_even, s_odd, m_even, m_odd = refs
    else:
        q_ref, k_ref, v_ref, lam_ref, g_ref, o_ref, kbuf, vbuf, s_even, s_odd, m_even, m_odd = refs
    hd = 2 * DH_A
    t = pl.program_id(0)

    @pl.when(t == 0)
    def _():
        s_odd[...] = jnp.zeros(s_odd.shape, F32)
        m_odd[...] = jnp.zeros(m_odd.shape, F32)
        vbuf[:, 0:hd] = jnp.zeros((vbuf.shape[0], hd), BF16)
        vbuf[:, hd:] = jnp.ones((vbuf.shape[0], vbuf.shape[1] - hd), BF16)

    @pl.when(t % nq == 0)
    def _():
        kbuf[...] = k_ref[...].astype(BF16)

    @pl.when((t > 0) & ((t - 1) % nq == 0))
    def _():
        if n_cache:
            head = ((t - 1) // nq) % H_A
            vbuf[0:n_cache, 0:hd] = cv_ref[pl.ds(head, n_cache, stride=H_A), :].astype(BF16)
        vbuf[n_cache:, 0:hd] = v_ref[...].astype(BF16)

    def step(s_write, m_write, s_read, m_read):
        q = q_ref[...]
        tq = q.shape[0]
        lane = lax.broadcasted_iota(jnp.int32, q.shape, 1)
        zero = jnp.zeros_like(q)
        q2 = jnp.concatenate([jnp.where(lane < DH_A, q, zero), jnp.where(lane >= DH_A, q, zero)], axis=0)
        row_max = jnp.concatenate([m_read[...]] * (KEY_CHUNK // LANES), axis=1)
        acc = None
        run_max = None
        for lo in range(0, s_read.shape[1], KEY_CHUNK):
            cols = slice(lo, lo + KEY_CHUNK)
            if lo < n_cache:
                s = _dot(q2, ck_ref[:, cols].astype(BF16))
            else:
                s = _dot_nt(q2, kbuf[lo - n_cache:lo - n_cache + KEY_CHUNK, :])
            s_write[:, cols] = s
            run_max = s if run_max is None else jnp.maximum(run_max, s)
            e = jnp.exp2(s_read[:, cols] - row_max).astype(BF16)
            part = _dot(e, vbuf[cols, :])
            acc = part if acc is None else acc + part
        m_write[...] = jnp.broadcast_to(jnp.max(run_max, axis=1, keepdims=True), m_write.shape)
        lam = lam_ref[...]
        lam_full = (jnp.exp(jnp.sum(lam[0:1] * lam[1:2], axis=1, keepdims=True))
                    - jnp.exp(jnp.sum(lam[2:3] * lam[3:4], axis=1, keepdims=True)) + lam_init)
        ratio = acc[:, :hd] / acc[:, hd:]
        o = ratio[:tq] - lam_full * ratio[tq:]
        o = _rms(o, SUBLN_EPS) * g_ref[...] * (1.0 - lam_init)
        o_ref[...] = o.astype(o_ref.dtype)

    pl.when(t % 2 == 0)(lambda: step(s_even, m_even, s_odd, m_odd))
    pl.when(t % 2 == 1)(lambda: step(s_odd, m_odd, s_even, m_even))


def _diff_attn(q, k, v, lam, subln_g, *, lam_init, tq, cache=None):
    B, T, _ = q.shape
    n_cache = 0 if cache is None else cache[0].shape[-1]
    assert n_cache % KEY_CHUNK == 0
    hd = 2 * DH_A
    nq = T // tq
    n_blocks = B * H_A * nq

    def block_of(t):
        blk = jnp.clip(t, 0, n_blocks - 1)
        return blk // (H_A * nq), (blk // nq) % H_A, blk % nq

    def q_map(t):
        b, h, i = block_of(t)
        return b, i, h

    def kv_map(t):
        b, h, _ = block_of(t)
        return b, 0, h

    def out_map(t):
        return q_map(t - 1)

    in_specs = [
        pl.BlockSpec((None, tq, hd), q_map),
        pl.BlockSpec((None, T, hd), kv_map),
        pl.BlockSpec((None, T, hd), lambda t: kv_map(t - 1)),
    ]
    args = [q, k, v]
    if cache is not None:
        ck, cv, layer = cache

        def ck_map(t):
            b, h, _ = block_of(t)
            return b, layer, h, 0, 0

        def cv_map(t):
            b, _, _ = block_of(t - 1)
            return b, layer, 0, 0

        in_specs += [pl.BlockSpec((None, None, None, hd, n_cache), ck_map),
                     pl.BlockSpec((None, None, n_cache * H_A, hd), cv_map)]
        args += [ck, cv]
    in_specs += [_resident(lam.shape), _resident(subln_g.shape)]
    args += [lam, subln_g]
    n_keys = n_cache + T
    return pl.pallas_call(
        functools.partial(_diff_attn_kernel, n_cache=n_cache, lam_init=lam_init, nq=nq),
        grid=(n_blocks + 1,),
        in_specs=in_specs,
        out_specs=pl.BlockSpec((None, tq, hd), out_map),
        out_shape=jax.ShapeDtypeStruct((B, T, N_QA), BF16),
        scratch_shapes=[pltpu.VMEM((T, hd), BF16), pltpu.VMEM((n_keys, 2 * hd), BF16),
                        pltpu.VMEM((2 * tq, n_keys), F32), pltpu.VMEM((2 * tq, n_keys), F32),
                        pltpu.VMEM((2 * tq, LANES), F32), pltpu.VMEM((2 * tq, LANES), F32)],
        compiler_params=_params("arbitrary"),
        name="diff_attn",
    )(*args)


def _win_attn_kernel(*refs, local, has_ctx, seq_len, tq, nq, nb):
    if has_ctx:
        q_ref, k_ref, v_ref, ck_ref, cv_ref, sink_ref, o_ref = refs
        ck = ck_ref[...].astype(BF16)
        cv = cv_ref[...].astype(BF16)
        cv_ones = jnp.concatenate([cv, jnp.ones(cv.shape, BF16)], axis=0)
    else:
        q_ref, k_ref, v_ref, sink_ref, o_ref = refs
    n_loc = 3 * WIN_BLOCK
    lane = lax.broadcasted_iota(jnp.int32, (tq, LANES), 1)
    zero = jnp.zeros((tq, LANES), BF16)
    for bi, qi in itertools.product(range(nb), range(nq)):
        rows_q = slice(qi * tq, (qi + 1) * tq)
        q = q_ref[bi, rows_q, :]
        if local:
            n = pl.program_id(1) * nq + qi
            start = pl.multiple_of(jnp.clip((n - 1) * WIN_BLOCK, 0, seq_len - n_loc), WIN_BLOCK)
            k = k_ref[bi, pl.ds(start, n_loc), :].astype(BF16)
            v = v_ref[bi, pl.ds(start, n_loc), :].astype(BF16)
        else:
            k = k_ref[bi].astype(BF16)
            v = v_ref[bi].astype(BF16)
        v_ones = jnp.concatenate([v, jnp.ones(v.shape, BF16)], axis=1)
        if local:
            rows = lax.broadcasted_iota(jnp.int32, (tq, n_loc), 0)
            cols = lax.broadcasted_iota(jnp.int32, (tq, n_loc), 1)
            valid = jnp.abs(cols - rows - (n * WIN_BLOCK - start)) <= WINDOW
        halves = []
        for j in range(KV_B):
            sel = (lane >= j * DH_B) & (lane < (j + 1) * DH_B)
            qj = jnp.concatenate(
                [jnp.where(sel, q[:, g * LANES:(g + 1) * LANES], zero) for g in range(G_B)], axis=0)
            s = _dot_nt(qj, k)
            if has_ctx:
                s = jnp.concatenate([s, _dot(qj, ck)], axis=1)
            es, sink_terms = [], []
            for g in range(G_B):
                sg = s[g * tq:(g + 1) * tq]
                if local:
                    band = jnp.where(valid, sg[:, :n_loc], -jnp.inf)
                    sg = jnp.concatenate([band, sg[:, n_loc:]], axis=1) if has_ctx else band
                sink = sink_ref[j * G_B + g] * LOG2E
                m = jnp.maximum(jnp.max(sg, axis=1, keepdims=True), sink)
                es.append(jnp.exp2(sg - m).astype(BF16))
                sink_terms.append(jnp.exp2(sink - m))
            e = jnp.concatenate(es, axis=0)
            n_own = k.shape[0]
            ov = _dot(e[:, :n_own], v_ones)
            if has_ctx:
                ov = ov + _dot_nt(e[:, n_own:], cv_ones)
            halves.append((ov, sink_terms))
        for g in range(G_B):
            rows_g = slice(g * tq, (g + 1) * tq)
            outs = []
            for ov, sink_terms in halves:
                outs.append(ov[rows_g, :LANES] / (ov[rows_g, LANES:] + sink_terms[g]))
            og = jnp.where(lane < DH_B, outs[0], outs[1])
            o_ref[bi, rows_q, g * LANES:(g + 1) * LANES] = og.astype(o_ref.dtype)


def _win_attn(q, k, v, sink, *, tq, nq, nb, local, cache=None):
    B, T, _ = q.shape
    has_ctx = cache is not None
    assert not local or tq == WIN_BLOCK
    assert not has_ctx or nb == 1
    rows = nq * tq
    in_specs = [
        pl.BlockSpec((nb, rows, G_B * LANES), lambda b, n: (b, n, 0)),
        pl.BlockSpec((nb, T, N_KVB), lambda b, n: (b, 0, 0)),
        pl.BlockSpec((nb, T, N_KVB), lambda b, n: (b, 0, 0)),
    ]
    args = [q, k, v]
    if has_ctx:
        ck, cv, layer = cache
        spec = pl.BlockSpec((None, None, N_KVB, ck.shape[3]), lambda b, n: (b, layer, 0, 0))
        in_specs += [spec, spec]
        args += [ck, cv]
    in_specs.append(pl.BlockSpec(memory_space=pltpu.SMEM))
    args.append(sink)
    return pl.pallas_call(
        functools.partial(_win_attn_kernel, local=local, has_ctx=has_ctx, seq_len=T, tq=tq, nq=nq, nb=nb),
        grid=(B // nb, T // rows),
        in_specs=in_specs,
        out_specs=pl.BlockSpec((nb, rows, G_B * LANES), lambda b, n: (b, n, 0)),
        out_shape=jax.ShapeDtypeStruct((B, T, H_B * DH_B), BF16),
        compiler_params=_params("parallel", "parallel"),
        name="win_attn",
    )(*args)


def _conv_kernel(prev_ref, cur_ref, next_ref, w_ref, bdw_ref, g_ref, b_ref, o_ref, win_ref, ph_ref):
    i = pl.program_id(1)
    rt = cur_ref.shape[0]
    zeros = jnp.zeros((CONV_HALO, C_CONV), F32)
    win_ref[0:CONV_HALO, :] = jnp.where(i > 0, prev_ref[...], zeros)
    win_ref[CONV_HALO:CONV_HALO + rt, :] = cur_ref[...]
    win_ref[CONV_HALO + rt:, :] = jnp.where(i < pl.num_programs(1) - 1, next_ref[...], zeros)
    n_ph = ph_ref.shape[1]
    for p in range(1, SUBLANES):
        ph_ref[p - 1] = win_ref[p:p + n_ph, :]
    first = CONV_HALO - CONV_K // 2
    tiles = CONV_ROWS // SUBLANES
    for r0 in range(0, rt, CONV_ROWS):
        acc = jnp.zeros((tiles, SUBLANES, C_CONV), F32) + bdw_ref[...]
        for t in range(CONV_K):
            p = (first + t) % SUBLANES
            base = r0 + first + t - p
            x = win_ref[base:base + CONV_ROWS, :] if p == 0 else ph_ref[p - 1, base:base + CONV_ROWS, :]
            acc = acc + x.reshape(tiles, SUBLANES, C_CONV) * w_ref[t]
        acc = acc.reshape(CONV_ROWS, C_CONV)
        mu = jnp.mean(acc, axis=-1, keepdims=True)
        cen = acc - mu
        var = jnp.mean(cen * cen, axis=-1, keepdims=True)
        y = cen * lax.rsqrt(var + NORM_EPS) * g_ref[...] + b_ref[...]
        o_ref[r0:r0 + CONV_ROWS, :] = _silu(y).astype(o_ref.dtype)


def _conv_module(u, w_dw, b_dw, g, b, *, rt):
    B, T, C = u.shape
    hb = rt // CONV_HALO
    last = T // CONV_HALO - 1
    n_ph = rt + (CONV_HALO + CONV_K // 2) // SUBLANES * SUBLANES
    w_tiles = jnp.broadcast_to(w_dw[:, None, :], (CONV_K, SUBLANES, C))
    in_specs = [
        pl.BlockSpec((None, CONV_HALO, C), lambda bb, i: (bb, jnp.maximum(i * hb - 1, 0), 0)),
        pl.BlockSpec((None, rt, C), lambda bb, i: (bb, i, 0)),
        pl.BlockSpec((None, CONV_HALO, C), lambda bb, i: (bb, jnp.minimum((i + 1) * hb, last), 0)),
        _resident(w_tiles.shape), _resident(b_dw.shape), _resident(g.shape), _resident(b.shape),
    ]
    return pl.pallas_call(
        _conv_kernel,
        grid=(B, T // rt),
        in_specs=in_specs,
        out_specs=pl.BlockSpec((None, rt, C), lambda bb, i: (bb, i, 0)),
        out_shape=jax.ShapeDtypeStruct((B, T, C), BF16),
        scratch_shapes=[pltpu.VMEM((rt + 2 * CONV_HALO, C), F32),
                        pltpu.VMEM((SUBLANES - 1, n_ph, C), F32)],
        compiler_params=_params("parallel", "parallel"),
        name="conv_module",
    )(u, u, u, w_tiles, b_dw, g, b)


def _merge_kernel(h_ref, mod_ref, g_ref, wg_ref, ya_ref, yb_ref, yc_ref, wb_ref, wo_ref, o_ref):
    h = h_ref[...]
    nb = _mod_rms(h, g_ref[1:2, :], mod_ref[3:4, :], mod_ref[4:5, :]).astype(BF16)
    merged = None
    for i, y_ref in enumerate((ya_ref, yb_ref, yc_ref)):
        gate = jax.nn.sigmoid(_dot(nb, wg_ref[:, i * D_MODEL:(i + 1) * D_MODEL]))
        term = gate * _dot(y_ref[...], wb_ref[i])
        merged = term if merged is None else merged + term
    o_ref[...] = h + mod_ref[5:6, :] * _dot(merged.astype(BF16), wo_ref[...])


def _merge(h, mod, norm_g, w_gate, ya, yb, yc, w_branch, w_out, layer, *, tm):
    B, T, D = h.shape
    per_batch = mod.shape[0] > 1
    row = lambda b, i: (b, i, 0)
    y_spec = pl.BlockSpec((None, tm, BRANCH_W), row)
    return pl.pallas_call(
        _merge_kernel,
        grid=(B, T // tm),
        in_specs=[
            pl.BlockSpec((None, tm, D), row),
            pl.BlockSpec((None, 9, D), (lambda b, i: (b, 0, 0)) if per_batch else (lambda b, i: (0, 0, 0))),
            _resident(norm_g.shape), _resident_slice(w_gate.shape, (layer,)),
            y_spec, y_spec, y_spec,
            _resident_slice(w_branch.shape, (layer,)), _resident_slice(w_out.shape, (layer,)),
        ],
        out_specs=pl.BlockSpec((None, tm, D), row),
        out_shape=jax.ShapeDtypeStruct((B, T, D), F32),
        compiler_params=_params("parallel", "parallel"),
        name="merge",
    )(h, mod, norm_g, w_gate, ya, yb, yc, w_branch, w_out)


def _rope_tables(n_tokens):
    quarter = DH_A // 4
    t = np.arange(n_tokens)
    inv = (ROPE_BASE ** (-np.arange(0, DH_A // 2, 2, dtype=np.float32) / (DH_A // 2))).astype(np.float32)
    pos = np.stack([(t // GRID_W).astype(np.float32), (t % GRID_W).astype(np.float32)], axis=1)
    ang = (pos[:, :, None] * inv[None, None, :]).astype(np.float32)
    ang = np.broadcast_to(ang[:, :, None, :], (n_tokens, 2, 2, quarter)).reshape(n_tokens, DH_A)
    sign = np.where((np.arange(DH_A) % (2 * quarter)) < quarter, -1.0, 1.0).astype(np.float32)
    cos = np.cos(ang).astype(np.float32)
    sin_signed = (np.sin(ang) * sign).astype(np.float32)
    reps = LANES // DH_A
    return jnp.asarray(np.tile(cos, (1, reps))), jnp.asarray(np.tile(sin_signed, (1, reps)))


def _group_major(w, axis):
    shape = w.shape
    split = shape[:axis] + (KV_B, G_B, DH_B) + shape[axis + 1:]
    return jnp.swapaxes(w.reshape(split), axis, axis + 1).reshape(shape)


def _prepare_weights(ffn_w_in, ffn_w_out, w_in, w_branch, w_out):
    qb_lo = 3 * N_QA
    qb_hi = qb_lo + H_B * DH_B
    w_proj = jnp.concatenate(
        [w_in[:, :, :qb_lo], _group_major(w_in[:, :, qb_lo:qb_hi], 2), w_in[:, :, qb_hi:N_PROJ]],
        axis=2).astype(BF16)
    w_gate = w_in[:, :, N_PROJ:].astype(BF16)
    wb = jnp.concatenate(
        [w_branch[:, 0:1], _group_major(w_branch[:, 1:2], 2), w_branch[:, 2:3]], axis=1).astype(BF16)
    return ffn_w_in.astype(BF16), ffn_w_out.astype(BF16), w_proj, w_gate, wb, w_out.astype(BF16)


def kernel(x_prompt, x_sample, cache_diff_k, cache_diff_v, cache_win_k, cache_win_v, c, c_ctx,
           w_mod, b_mod, norm_g, ffn_w_in, ffn_w_out, w_in, diff_lambda, diff_subln_g, win_sink,
           conv_dw_w, conv_dw_b, conv_norm_g, conv_norm_b, w_branch, w_out, final_norm_g):
    n_ctx_b, n_ctx_t, D = x_prompt.shape
    n_lat_b, n_lat_t, _ = x_sample.shape
    past = cache_diff_k.shape[2]
    tm = 512

    n_rows = 1 + n_lat_b
    pad_rows = -n_rows % 8
    cond = jnp.concatenate([c_ctx[None, :], c, jnp.zeros((pad_rows, D), F32)], axis=0)
    mod_all = _modulation(cond, w_mod, b_mod.reshape(DEPTH, 1, 9 * D))

    rope_tables = _rope_tables(n_lat_t)
    ck_a = jnp.transpose(cache_diff_k, (0, 1, 3, 4, 5, 2)).reshape(n_lat_b, DEPTH, H_A, 2 * DH_A, past)
    cv_a = cache_diff_v.reshape(n_lat_b, DEPTH, past * H_A, 2 * DH_A)
    ck_b = jnp.transpose(cache_win_k, (0, 1, 3, 4, 2)).reshape(n_lat_b, DEPTH, N_KVB, past)
    cv_b = jnp.transpose(cache_win_v, (0, 1, 3, 4, 2)).reshape(n_lat_b, DEPTH, N_KVB, past)
    final_g = final_norm_g.reshape(1, D)

    xp = x_prompt.reshape(1, n_ctx_b * n_ctx_t, D)
    xs = x_sample
    new_cache = None
    ffn_in, ffn_out, w_proj, w_gate, wb, wo = _prepare_weights(ffn_w_in, ffn_w_out, w_in, w_branch, w_out)
    for l in range(DEPTH):
        lam_init = 0.8 - 0.6 * math.exp(-0.3 * l)
        g_l = norm_g[l]
        lam = diff_lambda[l]
        subln = diff_subln_g[l].reshape(1, 2 * DH_A)
        sink = win_sink[l]
        conv_p = (conv_dw_w[l], conv_dw_b[l].reshape(1, C_CONV),
                  conv_norm_g[l].reshape(1, C_CONV), conv_norm_b[l].reshape(1, C_CONV))
        mod_ctx = mod_all[l, 0:1].reshape(1, 9, D)
        mod_lat = mod_all[l, 1:n_rows].reshape(n_lat_b, 9, D)
        last = l == DEPTH - 1

        h = _ffn(xp, mod_ctx, g_l, ffn_in, ffn_out, (l, 0), mod_row=0, g_row=0, tm=tm)
        qa, ka, va, qb, kb, vb, u, *new_cache = _inproj(
            h, mod_ctx, g_l, w_proj, l, tm=tm, kv_dtype=F32,
            cache_seq=(n_ctx_b, n_ctx_t), cache_acc=new_cache)
        seq = lambda a: a.reshape(n_ctx_b, n_ctx_t, a.shape[-1])
        ka, va, kb, vb = seq(ka), seq(va), seq(kb), seq(vb)
        ya = _diff_attn(seq(qa), ka, va, lam, subln, lam_init=lam_init, tq=n_ctx_t)
        yb = _win_attn(seq(qb), kb, vb, sink, tq=n_ctx_t, nq=1, nb=4, local=False)
        yc = _conv_module(seq(u), *conv_p, rt=n_ctx_t)
        flat = lambda a: a.reshape(1, n_ctx_b * n_ctx_t, a.shape[-1])
        h = _merge(h, mod_ctx, g_l, w_gate, flat(ya), flat(yb), flat(yc), wb, wo, l, tm=tm)
        xp = _ffn(h, mod_ctx, g_l, ffn_in, ffn_out, (l, 1), mod_row=6, g_row=2, tm=tm,
                  final_g=final_g if last else None)

        h = _ffn(xs, mod_lat, g_l, ffn_in, ffn_out, (l, 0), mod_row=0, g_row=0, tm=tm)
        qa, ka, va, qb, kb, vb, u = _inproj(h, mod_lat, g_l, w_proj, l, tm=tm, kv_dtype=BF16,
                                            rope_tables=rope_tables)
        ya = _diff_attn(qa, ka, va, lam, subln, lam_init=lam_init, tq=512, cache=(ck_a, cv_a, l))
        yb = _win_attn(qb, kb, vb, sink, tq=WIN_BLOCK, nq=4, nb=1, local=True, cache=(ck_b, cv_b, l))
        yc = _conv_module(u, *conv_p, rt=256)
        h = _merge(h, mod_lat, g_l, w_gate, ya, yb, yc, wb, wo, l, tm=tm)
        xs = _ffn(h, mod_lat, g_l, ffn_in, ffn_out, (l, 1), mod_row=6, g_row=2, tm=tm,
                  final_g=final_g if last else None)

    y_prompt = xp.reshape(n_ctx_b, n_ctx_t, D)
    dk, dv, wk, wv = new_cache
    new_diff_k = jnp.transpose(dk.reshape(n_ctx_b, DEPTH, H_A, 2, DH_A, n_ctx_t), (0, 1, 5, 2, 3, 4))
    new_diff_v = dv.reshape(n_ctx_b, DEPTH, n_ctx_t, H_A, 2 * DH_A)
    new_win_k = jnp.transpose(wk.reshape(n_ctx_b, DEPTH, KV_B, DH_B, n_ctx_t), (0, 1, 4, 2, 3))
    new_win_v = jnp.transpose(wv.reshape(n_ctx_b, DEPTH, KV_B, DH_B, n_ctx_t), (0, 1, 4, 2, 3))
    return (y_prompt, xs, new_diff_k, new_diff_v, new_win_k, new_win_v)
```

```python
import functools
import itertools
import math

import numpy as np
import jax
import jax.numpy as jnp
from jax import lax
from jax.experimental import pallas as pl
from jax.experimental.pallas import tpu as pltpu

D_MODEL = 1024
DEPTH = 2
GRID_W = 64
H_A = 4
DH_A = 64
H_B = 8
KV_B = 2
G_B = H_B // KV_B
DH_B = 64
WINDOW = 128
WIN_BLOCK = 128
C_CONV = 512
CONV_K = 31
BRANCH_W = 512
D_FF = 2816
ROPE_BASE = 10000.0
NORM_EPS = 1e-6
SUBLN_EPS = 1e-5
SCALE_A = DH_A ** -0.5
SCALE_B = DH_B ** -0.5
LOG2E = math.log2(math.e)

N_QA = H_A * 2 * DH_A
N_KVB = KV_B * DH_B
N_ATTN = 3 * N_QA + H_B * DH_B + 2 * N_KVB
N_PROJ = N_ATTN + 2 * C_CONV

LANES = 128
SUBLANES = 8
ROPE_HALF = DH_A // 4
FF_CHUNK = 256
KEY_CHUNK = 256
CONV_HALO = 16
CONV_ROWS = 32
VMEM_LIMIT = 56 * 1024 * 1024

BF16 = jnp.bfloat16
F32 = jnp.float32


def _params(*sem):
    return pltpu.CompilerParams(dimension_semantics=sem, vmem_limit_bytes=VMEM_LIMIT)


def _resident(shape):
    nd = len(shape)
    return pl.BlockSpec(shape, lambda *_: (0,) * nd, pipeline_mode=pl.Buffered(1))


def _resident_slice(shape, lead):
    n = len(lead)
    block = (None,) * n + tuple(shape[n:])
    index = tuple(lead) + (0,) * (len(shape) - n)
    return pl.BlockSpec(block, lambda *_: index, pipeline_mode=pl.Buffered(1))


def _resident_chunks(shape, lead, axis, chunk):
    n = len(lead)
    dims = list(shape[n:])
    count, rem = divmod(dims[axis], chunk)
    assert rem == 0
    dims[axis] = chunk
    block = (None,) * n + tuple(dims)
    specs = []
    for j in range(count):
        index = tuple(lead) + tuple(j if d == axis else 0 for d in range(len(dims)))
        specs.append(pl.BlockSpec(block, lambda *_, index=index: index, pipeline_mode=pl.Buffered(1)))
    return specs


def _dot(a, b):
    return jnp.dot(a, b, preferred_element_type=F32)


def _dot_nt(a, b):
    return lax.dot_general(a, b, (((1,), (1,)), ((), ())), preferred_element_type=F32)


def _rms(x, eps):
    return x * lax.rsqrt(jnp.mean(x * x, axis=-1, keepdims=True) + eps)


def _mod_rms(x, g, shift, scale):
    return _rms(x, NORM_EPS) * g * (1.0 + scale) + shift


def _silu(x):
    return x * jax.nn.sigmoid(x)


def _mod_kernel(c_ref, w_ref, b_ref, o_ref):
    a = _silu(c_ref[...]).astype(BF16)
    o_ref[...] = _dot(a, w_ref[...].astype(BF16)) + b_ref[...]


def _modulation(cond, w_mod, b_mod):
    R = cond.shape[0]
    tn = D_MODEL
    return pl.pallas_call(
        _mod_kernel,
        grid=(DEPTH, 9 * D_MODEL // tn),
        in_specs=[
            pl.BlockSpec((R, D_MODEL), lambda l, j: (0, 0)),
            pl.BlockSpec((None, D_MODEL, tn), lambda l, j: (l, 0, j)),
            pl.BlockSpec((None, 1, tn), lambda l, j: (l, 0, j)),
        ],
        out_specs=pl.BlockSpec((None, R, tn), lambda l, j: (l, 0, j)),
        out_shape=jax.ShapeDtypeStruct((DEPTH, R, 9 * D_MODEL), F32),
        compiler_params=_params("parallel", "parallel"),
        name="modulation",
    )(cond, w_mod, b_mod)


def _ffn_kernel(x_ref, mod_ref, g_ref, *rest, mod_row, g_row, final):
    wa_ref, wb_ref, wout_ref, *rest = rest
    if final:
        fg_ref, o_ref = rest
    else:
        (o_ref,) = rest
    x = x_ref[...]
    shift = mod_ref[mod_row:mod_row + 1, :]
    scale = mod_ref[mod_row + 1:mod_row + 2, :]
    gate = mod_ref[mod_row + 2:mod_row + 3, :]
    nb = _mod_rms(x, g_ref[g_row:g_row + 1, :], shift, scale).astype(BF16)
    acc = None
    for lo in range(0, D_FF, FF_CHUNK):
        a = _dot(nb, wa_ref[:, lo:lo + FF_CHUNK])
        b = _dot(nb, wb_ref[:, lo:lo + FF_CHUNK])
        hid = (_silu(a) * b).astype(BF16)
        part = _dot(hid, wout_ref[lo:lo + FF_CHUNK, :])
        acc = part if acc is None else acc + part
    h = x + 0.5 * gate * acc
    if final:
        h = _rms(h, NORM_EPS) * fg_ref[...]
    o_ref[...] = h


def _ffn(x, mod, norm_g, w_in, w_out, which, *, mod_row, g_row, tm, final_g=None):
    B, T, D = x.shape
    per_batch = mod.shape[0] > 1
    final = final_g is not None
    l, s = which
    in_specs = [
        pl.BlockSpec((None, tm, D), lambda b, i: (b, i, 0)),
        pl.BlockSpec((None, 9, D), (lambda b, i: (b, 0, 0)) if per_batch else (lambda b, i: (0, 0, 0))),
        _resident(norm_g.shape),
    ]
    in_specs += _resident_chunks(w_in.shape, (l, s), 1, D_FF) + [_resident_slice(w_out.shape, (l, s))]
    args = [x, mod, norm_g, w_in, w_in, w_out]
    if final:
        in_specs.append(_resident(final_g.shape))
        args.append(final_g)
    return pl.pallas_call(
        functools.partial(_ffn_kernel, mod_row=mod_row, g_row=g_row, final=final),
        grid=(B, T // tm),
        in_specs=in_specs,
        out_specs=pl.BlockSpec((None, tm, D), lambda b, i: (b, i, 0)),
        out_shape=jax.ShapeDtypeStruct((B, T, D), F32),
        compiler_params=_params("parallel", "parallel"),
        name="ffn",
    )(*args)


def _rope(x, cos, sin_signed, lo_half):
    outs = []
    for j in range(x.shape[1] // LANES):
        xj = x[:, j * LANES:(j + 1) * LANES]
        rot = jnp.where(lo_half, pltpu.roll(xj, LANES - ROPE_HALF, 1), pltpu.roll(xj, ROPE_HALF, 1))
        outs.append(xj * cos + rot * sin_signed)
    return outs[0] if len(outs) == 1 else jnp.concatenate(outs, axis=1)


def _inproj_kernel(x_ref, mod_ref, g_ref, w_ref, *rest, rope, n_aliased, cache_out):
    if rope:
        cos_ref, sin_ref = rest[:2]
        rest = rest[2:]
    rest = rest[n_aliased:]
    qa_ref, ka_ref, va_ref, qb_ref, kb_ref, vb_ref, u_ref = rest[:7]
    x = x_ref[...]
    nb = _mod_rms(x, g_ref[1:2, :], mod_ref[3:4, :], mod_ref[4:5, :]).astype(BF16)

    def proj(lo, hi):
        return _dot(nb, w_ref[:, lo:hi])

    qa = proj(0, N_QA)
    ka = proj(N_QA, 2 * N_QA)
    va = proj(2 * N_QA, 3 * N_QA)
    qb = proj(3 * N_QA, 4 * N_QA)
    kvb = proj(4 * N_QA, N_ATTN)
    kb = kvb[:, :N_KVB]
    vb = kvb[:, N_KVB:]
    cu = proj(N_ATTN, N_PROJ)
    if rope:
        cos = cos_ref[...]
        sin_signed = sin_ref[...]
        lane = lax.broadcasted_iota(jnp.int32, cos.shape, 1)
        lo_half = (lane % (2 * ROPE_HALF)) < ROPE_HALF
        qa = _rope(qa, cos, sin_signed, lo_half)
        ka = _rope(ka, cos, sin_signed, lo_half)
        qb = _rope(qb, cos, sin_signed, lo_half)
        kb = _rope(kb, cos, sin_signed, lo_half)
    qa_ref[...] = (qa * (SCALE_A * LOG2E)).astype(qa_ref.dtype)
    ka_ref[...] = ka.astype(ka_ref.dtype)
    va_ref[...] = va.astype(va_ref.dtype)
    qb_ref[...] = (qb * (SCALE_B * LOG2E)).astype(qb_ref.dtype)
    kb_ref[...] = kb.astype(kb_ref.dtype)
    vb_ref[...] = vb.astype(vb_ref.dtype)
    u_ref[...] = cu[:, :C_CONV] * jax.nn.sigmoid(cu[:, C_CONV:])
    if cache_out:
        dk_ref, dv_ref, wk_ref, wv_ref = rest[7:]
        n_seq, _, _, seq_len = dk_ref.shape
        ka_t, kb_t, vb_t = ka.T, kb.T, vb.T
        for s in range(n_seq):
            tok = slice(s * seq_len, (s + 1) * seq_len)
            for hc in range(2 * H_A):
                dk_ref[s, hc] = ka_t[hc * DH_A:(hc + 1) * DH_A, tok]
            for hh in range(H_A):
                dv_ref[s, pl.ds(hh, seq_len, stride=H_A), :] = va[tok, hh * 2 * DH_A:(hh + 1) * 2 * DH_A]
            wk_ref[s] = kb_t[:, tok]
            wv_ref[s] = vb_t[:, tok]


def _inproj(h, mod, norm_g, w_proj, layer, *, tm, kv_dtype, rope_tables=None, cache_seq=None, cache_acc=None):
    B, T, D = h.shape
    per_batch = mod.shape[0] > 1
    rope = rope_tables is not None
    row = lambda b, i: (b, i, 0)
    in_specs = [
        pl.BlockSpec((None, tm, D), row),
        pl.BlockSpec((None, 9, D), (lambda b, i: (b, 0, 0)) if per_batch else (lambda b, i: (0, 0, 0))),
        _resident(norm_g.shape),
        _resident_slice(w_proj.shape, (layer,)),
    ]
    args = [h, mod, norm_g, w_proj]
    if rope:
        in_specs += [pl.BlockSpec((tm, LANES), lambda b, i: (i, 0))] * 2
        args += list(rope_tables)
    widths = (N_QA, N_QA, N_QA, H_B * DH_B, N_KVB, N_KVB, C_CONV)
    dtypes = (BF16, kv_dtype, kv_dtype, BF16, kv_dtype, kv_dtype, F32)
    out_specs = [pl.BlockSpec((None, tm, w), row) for w in widths]
    out_shape = [jax.ShapeDtypeStruct((B, T, w), dt) for w, dt in zip(widths, dtypes)]
    aliases = {}
    n_aliased = 0
    if cache_seq is not None:
        n_all, seq_len = cache_seq
        n_seq = tm // seq_len
        assert B == 1 and n_seq * seq_len == tm
        cache_shapes = ((2 * H_A, DH_A, seq_len), (seq_len * H_A, 2 * DH_A), (N_KVB, seq_len), (N_KVB, seq_len))
        for shape in cache_shapes:
            nd = len(shape)
            out_specs.append(pl.BlockSpec((n_seq, None) + shape,
                                          lambda b, i, nd=nd: (i, layer) + (0,) * nd))
            out_shape.append(jax.ShapeDtypeStruct((n_all, DEPTH) + shape, F32))
        if cache_acc is not None:
            n_aliased = len(cache_acc)
            aliases = {len(args) + j: len(widths) + j for j in range(n_aliased)}
            in_specs += [pl.BlockSpec(memory_space=pl.ANY)] * n_aliased
            args += list(cache_acc)
    return pl.pallas_call(
        functools.partial(_inproj_kernel, rope=rope, n_aliased=n_aliased, cache_out=cache_seq is not None),
        grid=(B, T // tm),
        in_specs=in_specs,
        out_specs=out_specs,
        out_shape=out_shape,
        input_output_aliases=aliases,
        compiler_params=_params("parallel", "parallel"),
        name="inproj",
    )(*args)


def _diff_attn_kernel(*refs, n_cache, lam_init, nq, hb):
    if n_cache:
        q_ref, k_ref, v_ref, ck_ref, cv_ref, lam_ref, g_ref, o_ref, kbuf, vbuf, s_even, s_odd, m_even, m_odd = refs
    else:
        q_ref, k_ref, v_ref, lam_ref, g_ref, o_ref, kbuf, vbuf, s_even, s_odd, m_even, m_odd = refs
    hd = 2 * DH_A
    t = pl.program_id(0)
    heads = [(hh, slice(hh * hd, (hh + 1) * hd)) for hh in range(hb)]

    @pl.when(t == 0)
    def _():
        s_odd[...] = jnp.zeros(s_odd.shape, F32)
        m_odd[...] = jnp.zeros(m_odd.shape, F32)
        vbuf[:, :, 0:hd] = jnp.zeros(vbuf.shape[:2] + (hd,), BF16)
        vbuf[:, :, hd:] = jnp.ones(vbuf.shape[:2] + (vbuf.shape[2] - hd,), BF16)

    @pl.when(t % nq == 0)
    def _():
        for hh, lanes in heads:
            kbuf[hh] = k_ref[:, lanes].astype(BF16)

    @pl.when((t > 0) & ((t - 1) % nq == 0))
    def _():
        if n_cache:
            head = ((t - 1) // nq) % H_A
            vbuf[0, 0:n_cache, 0:hd] = cv_ref[pl.ds(head, n_cache, stride=H_A), :].astype(BF16)
        for hh, lanes in heads:
            vbuf[hh, n_cache:, 0:hd] = v_ref[:, lanes].astype(BF16)

    def step(s_write, m_write, s_read, m_read):
        lam = lam_ref[...]
        lam_full = (jnp.exp(jnp.sum(lam[0:1] * lam[1:2], axis=1, keepdims=True))
                    - jnp.exp(jnp.sum(lam[2:3] * lam[3:4], axis=1, keepdims=True)) + lam_init)
        tq = q_ref.shape[0]
        lane = lax.broadcasted_iota(jnp.int32, (tq, hd), 1)
        zero = jnp.zeros((tq, hd), q_ref.dtype)
        for hh, lanes in heads:
            q = q_ref[:, lanes]
            q2 = jnp.concatenate([jnp.where(lane < DH_A, q, zero), jnp.where(lane >= DH_A, q, zero)], axis=0)
            row_max = jnp.concatenate([m_read[hh]] * (KEY_CHUNK // LANES), axis=1)
            acc = None
            run_max = None
            for lo in range(0, s_read.shape[2], KEY_CHUNK):
                cols = slice(lo, lo + KEY_CHUNK)
                if lo < n_cache:
                    s = _dot(q2, ck_ref[:, cols].astype(BF16))
                else:
                    s = _dot_nt(q2, kbuf[hh, lo - n_cache:lo - n_cache + KEY_CHUNK, :])
                s_write[hh, :, cols] = s
                run_max = s if run_max is None else jnp.maximum(run_max, s)
                e = jnp.exp2(s_read[hh, :, cols] - row_max).astype(BF16)
                part = _dot(e, vbuf[hh, cols, :])
                acc = part if acc is None else acc + part
            m_write[hh] = jnp.broadcast_to(jnp.max(run_max, axis=1, keepdims=True), m_write.shape[1:])
            ratio = acc[:, :hd] / acc[:, hd:]
            o = ratio[:tq] - lam_full * ratio[tq:]
            o = _rms(o, SUBLN_EPS) * g_ref[...] * (1.0 - lam_init)
            o_ref[:, lanes] = o.astype(o_ref.dtype)

    pl.when(t % 2 == 0)(lambda: step(s_even, m_even, s_odd, m_odd))
    pl.when(t % 2 == 1)(lambda: step(s_odd, m_odd, s_even, m_even))


def _diff_attn(q, k, v, lam, subln_g, *, lam_init, tq, hb, cache=None):
    B, T, _ = q.shape
    n_cache = 0 if cache is None else cache[0].shape[-1]
    assert n_cache % KEY_CHUNK == 0 and (cache is None or hb == 1)
    hd = 2 * DH_A
    nq = T // tq
    n_groups = H_A // hb
    n_blocks = B * n_groups * nq

    def block_of(t):
        blk = jnp.clip(t, 0, n_blocks - 1)
        return blk // (n_groups * nq), (blk // nq) % n_groups, blk % nq

    def q_map(t):
        b, h, i = block_of(t)
        return b, i, h

    def kv_map(t):
        b, h, _ = block_of(t)
        return b, 0, h

    def out_map(t):
        return q_map(t - 1)

    in_specs = [
        pl.BlockSpec((None, tq, hb * hd), q_map),
        pl.BlockSpec((None, T, hb * hd), kv_map),
        pl.BlockSpec((None, T, hb * hd), lambda t: kv_map(t - 1)),
    ]
    args = [q, k, v]
    if cache is not None:
        ck, cv, layer = cache

        def ck_map(t):
            b, h, _ = block_of(t)
            return b, layer, h, 0, 0

        def cv_map(t):
            b, _, _ = block_of(t - 1)
            return b, layer, 0, 0

        in_specs += [pl.BlockSpec((None, None, None, hd, n_cache), ck_map),
                     pl.BlockSpec((None, None, n_cache * H_A, hd), cv_map)]
        args += [ck, cv]
    in_specs += [_resident(lam.shape), _resident(subln_g.shape)]
    args += [lam, subln_g]
    n_keys = n_cache + T
    return pl.pallas_call(
        functools.partial(_diff_attn_kernel, n_cache=n_cache, lam_init=lam_init, nq=nq, hb=hb),
        grid=(n_blocks + 1,),
        in_specs=in_specs,
        out_specs=pl.BlockSpec((None, tq, hb * hd), out_map),
        out_shape=jax.ShapeDtypeStruct((B, T, N_QA), BF16),
        scratch_shapes=[pltpu.VMEM((hb, T, hd), BF16), pltpu.VMEM((hb, n_keys, 2 * hd), BF16),
                        pltpu.VMEM((hb, 2 * tq, n_keys), F32), pltpu.VMEM((hb, 2 * tq, n_keys), F32),
                        pltpu.VMEM((hb, 2 * tq, LANES), F32), pltpu.VMEM((hb, 2 * tq, LANES), F32)],
        compiler_params=_params("arbitrary"),
        name="diff_attn",
    )(*args)


def _win_attn_kernel(*refs, local, has_ctx, seq_len, tq, nq, nb):
    if has_ctx:
        q_ref, k_ref, v_ref, ck_ref, cv_ref, sink_ref, o_ref = refs
        ck = ck_ref[...].astype(BF16)
        cv = cv_ref[...].astype(BF16)
        cv_ones = jnp.concatenate([cv, jnp.ones(cv.shape, BF16)], axis=0)
    else:
        q_ref, k_ref, v_ref, sink_ref, o_ref = refs
    n_loc = 3 * WIN_BLOCK
    lane = lax.broadcasted_iota(jnp.int32, (tq, LANES), 1)
    zero = jnp.zeros((tq, LANES), BF16)
    for bi, qi in itertools.product(range(nb), range(nq)):
        rows_q = slice(qi * tq, (qi + 1) * tq)
        q = q_ref[bi, rows_q, :]
        if local:
            n = pl.program_id(1) * nq + qi
            start = pl.multiple_of(jnp.clip((n - 1) * WIN_BLOCK, 0, seq_len - n_loc), WIN_BLOCK)
            k = k_ref[bi, pl.ds(start, n_loc), :].astype(BF16)
            v = v_ref[bi, pl.ds(start, n_loc), :].astype(BF16)
        else:
            k = k_ref[bi].astype(BF16)
            v = v_ref[bi].astype(BF16)
        v_ones = jnp.concatenate([v, jnp.ones(v.shape, BF16)], axis=1)
        if local:
            rows = lax.broadcasted_iota(jnp.int32, (tq, n_loc), 0)
            cols = lax.broadcasted_iota(jnp.int32, (tq, n_loc), 1)
            valid = jnp.abs(cols - rows - (n * WIN_BLOCK - start)) <= WINDOW
        halves = []
        for j in range(KV_B):
            sel = (lane >= j * DH_B) & (lane < (j + 1) * DH_B)
            qj = jnp.concatenate(
                [jnp.where(sel, q[:, g * LANES:(g + 1) * LANES], zero) for g in range(G_B)], axis=0)
            s = _dot_nt(qj, k)
            if has_ctx:
                s = jnp.concatenate([s, _dot(qj, ck)], axis=1)
            es, sink_terms = [], []
            for g in range(G_B):
                sg = s[g * tq:(g + 1) * tq]
                if local:
                    band = jnp.where(valid, sg[:, :n_loc], -jnp.inf)
                    sg = jnp.concatenate([band, sg[:, n_loc:]], axis=1) if has_ctx else band
                sink = sink_ref[j * G_B + g] * LOG2E
                m = jnp.maximum(jnp.max(sg, axis=1, keepdims=True), sink)
                es.append(jnp.exp2(sg - m).astype(BF16))
                sink_terms.append(jnp.exp2(sink - m))
            e = jnp.concatenate(es, axis=0)
            n_own = k.shape[0]
            ov = _dot(e[:, :n_own], v_ones)
            if has_ctx:
                ov = ov + _dot_nt(e[:, n_own:], cv_ones)
            halves.append((ov, sink_terms))
        for g in range(G_B):
            rows_g = slice(g * tq, (g + 1) * tq)
            outs = []
            for ov, sink_terms in halves:
                outs.append(ov[rows_g, :LANES] / (ov[rows_g, LANES:] + sink_terms[g]))
            og = jnp.where(lane < DH_B, outs[0], outs[1])
            o_ref[bi, rows_q, g * LANES:(g + 1) * LANES] = og.astype(o_ref.dtype)


def _win_attn(q, k, v, sink, *, tq, nq, nb, local, cache=None):
    B, T, _ = q.shape
    has_ctx = cache is not None
    assert not local or tq == WIN_BLOCK
    assert not has_ctx or nb == 1
    rows = nq * tq
    in_specs = [
        pl.BlockSpec((nb, rows, G_B * LANES), lambda b, n: (b, n, 0)),
        pl.BlockSpec((nb, T, N_KVB), lambda b, n: (b, 0, 0)),
        pl.BlockSpec((nb, T, N_KVB), lambda b, n: (b, 0, 0)),
    ]
    args = [q, k, v]
    if has_ctx:
        ck, cv, layer = cache
        spec = pl.BlockSpec((None, None, N_KVB, ck.shape[3]), lambda b, n: (b, layer, 0, 0))
        in_specs += [spec, spec]
        args += [ck, cv]
    in_specs.append(pl.BlockSpec(memory_space=pltpu.SMEM))
    args.append(sink)
    return pl.pallas_call(
        functools.partial(_win_attn_kernel, local=local, has_ctx=has_ctx, seq_len=T, tq=tq, nq=nq, nb=nb),
        grid=(B // nb, T // rows),
        in_specs=in_specs,
        out_specs=pl.BlockSpec((nb, rows, G_B * LANES), lambda b, n: (b, n, 0)),
        out_shape=jax.ShapeDtypeStruct((B, T, H_B * DH_B), BF16),
        compiler_params=_params("parallel", "parallel"),
        name="win_attn",
    )(*args)


def _conv_kernel(prev_ref, cur_ref, next_ref, w_ref, bdw_ref, g_ref, b_ref, o_ref, win_ref, ph_ref):
    i = pl.program_id(1)
    rt = cur_ref.shape[0]
    zeros = jnp.zeros((CONV_HALO, C_CONV), F32)
    win_ref[0:CONV_HALO, :] = jnp.where(i > 0, prev_ref[...], zeros)
    win_ref[CONV_HALO:CONV_HALO + rt, :] = cur_ref[...]
    win_ref[CONV_HALO + rt:, :] = jnp.where(i < pl.num_programs(1) - 1, next_ref[...], zeros)
    n_ph = ph_ref.shape[1]
    for p in range(1, SUBLANES):
        ph_ref[p - 1] = win_ref[p:p + n_ph, :]
    first = CONV_HALO - CONV_K // 2
    tiles = CONV_ROWS // SUBLANES
    for r0 in range(0, rt, CONV_ROWS):
        acc = jnp.zeros((tiles, SUBLANES, C_CONV), F32) + bdw_ref[...]
        for t in range(CONV_K):
            p = (first + t) % SUBLANES
            base = r0 + first + t - p
            x = win_ref[base:base + CONV_ROWS, :] if p == 0 else ph_ref[p - 1, base:base + CONV_ROWS, :]
            acc = acc + x.reshape(tiles, SUBLANES, C_CONV) * w_ref[t]
        acc = acc.reshape(CONV_ROWS, C_CONV)
        mu = jnp.mean(acc, axis=-1, keepdims=True)
        cen = acc - mu
        var = jnp.mean(cen * cen, axis=-1, keepdims=True)
        y = cen * lax.rsqrt(var + NORM_EPS) * g_ref[...] + b_ref[...]
        o_ref[r0:r0 + CONV_ROWS, :] = _silu(y).astype(o_ref.dtype)


def _conv_module(u, w_dw, b_dw, g, b, *, rt):
    B, T, C = u.shape
    hb = rt // CONV_HALO
    last = T // CONV_HALO - 1
    n_ph = rt + (CONV_HALO + CONV_K // 2) // SUBLANES * SUBLANES
    w_tiles = jnp.broadcast_to(w_dw[:, None, :], (CONV_K, SUBLANES, C))
    in_specs = [
        pl.BlockSpec((None, CONV_HALO, C), lambda bb, i: (bb, jnp.maximum(i * hb - 1, 0), 0)),
        pl.BlockSpec((None, rt, C), lambda bb, i: (bb, i, 0)),
        pl.BlockSpec((None, CONV_HALO, C), lambda bb, i: (bb, jnp.minimum((i + 1) * hb, last), 0)),
        _resident(w_tiles.shape), _resident(b_dw.shape), _resident(g.shape), _resident(b.shape),
    ]
    return pl.pallas_call(
        _conv_kernel,
        grid=(B, T // rt),
        in_specs=in_specs,
        out_specs=pl.BlockSpec((None, rt, C), lambda bb, i: (bb, i, 0)),
        out_shape=jax.ShapeDtypeStruct((B, T, C), BF16),
        scratch_shapes=[pltpu.VMEM((rt + 2 * CONV_HALO, C), F32),
                        pltpu.VMEM((SUBLANES - 1, n_ph, C), F32)],
        compiler_params=_params("parallel", "parallel"),
        name="conv_module",
    )(u, u, u, w_tiles, b_dw, g, b)


def _merge_kernel(h_ref, mod_ref, g_ref, ya_ref, yb_ref, yc_ref, *rest):
    wg_refs, wb_refs, wo_ref, o_ref = rest[:3], rest[3:6], rest[6], rest[7]
    h = h_ref[...]
    nb = _mod_rms(h, g_ref[1:2, :], mod_ref[3:4, :], mod_ref[4:5, :]).astype(BF16)
    merged = None
    for i, y_ref in enumerate((ya_ref, yb_ref, yc_ref)):
        gate = jax.nn.sigmoid(_dot(nb, wg_refs[i][...]))
        term = gate * _dot(y_ref[...], wb_refs[i][0])
        merged = term if merged is None else merged + term
    o_ref[...] = h + mod_ref[5:6, :] * _dot(merged.astype(BF16), wo_ref[...])


def _merge(h, mod, norm_g, w_gate, ya, yb, yc, w_branch, w_out, layer, *, tm):
    B, T, D = h.shape
    per_batch = mod.shape[0] > 1
    row = lambda b, i: (b, i, 0)
    y_spec = pl.BlockSpec((None, tm, BRANCH_W), row)
    return pl.pallas_call(
        _merge_kernel,
        grid=(B, T // tm),
        in_specs=[
            pl.BlockSpec((None, tm, D), row),
            pl.BlockSpec((None, 9, D), (lambda b, i: (b, 0, 0)) if per_batch else (lambda b, i: (0, 0, 0))),
            _resident(norm_g.shape), y_spec, y_spec, y_spec,
            *_resident_chunks(w_gate.shape, (layer,), 1, D_MODEL),
            *_resident_chunks(w_branch.shape, (layer,), 0, 1),
            _resident_slice(w_out.shape, (layer,)),
        ],
        out_specs=pl.BlockSpec((None, tm, D), row),
        out_shape=jax.ShapeDtypeStruct((B, T, D), F32),
        compiler_params=_params("parallel", "parallel"),
        name="merge",
    )(h, mod, norm_g, ya, yb, yc, w_gate, w_gate, w_gate, w_branch, w_branch, w_branch, w_out)


def _rope_tables(n_tokens):
    quarter = DH_A // 4
    t = np.arange(n_tokens)
    inv = (ROPE_BASE ** (-np.arange(0, DH_A // 2, 2, dtype=np.float32) / (DH_A // 2))).astype(np.float32)
    pos = np.stack([(t // GRID_W).astype(np.float32), (t % GRID_W).astype(np.float32)], axis=1)
    ang = (pos[:, :, None] * inv[None, None, :]).astype(np.float32)
    ang = np.broadcast_to(ang[:, :, None, :], (n_tokens, 2, 2, quarter)).reshape(n_tokens, DH_A)
    sign = np.where((np.arange(DH_A) % (2 * quarter)) < quarter, -1.0, 1.0).astype(np.float32)
    cos = np.cos(ang).astype(np.float32)
    sin_signed = (np.sin(ang) * sign).astype(np.float32)
    reps = LANES // DH_A
    return jnp.asarray(np.tile(cos, (1, reps))), jnp.asarray(np.tile(sin_signed, (1, reps)))


def _group_major(w, axis):
    shape = w.shape
    split = shape[:axis] + (KV_B, G_B, DH_B) + shape[axis + 1:]
    return jnp.swapaxes(w.reshape(split), axis, axis + 1).reshape(shape)


def _prepare_weights(ffn_w_in, ffn_w_out, w_in, w_branch, w_out):
    qb_lo = 3 * N_QA
    qb_hi = qb_lo + H_B * DH_B
    w_proj = jnp.concatenate(
        [w_in[:, :, :qb_lo], _group_major(w_in[:, :, qb_lo:qb_hi], 2), w_in[:, :, qb_hi:N_PROJ]],
        axis=2).astype(BF16)
    w_gate = w_in[:, :, N_PROJ:].astype(BF16)
    wb = jnp.concatenate(
        [w_branch[:, 0:1], _group_major(w_branch[:, 1:2], 2), w_branch[:, 2:3]], axis=1).astype(BF16)
    return ffn_w_in.astype(BF16), ffn_w_out.astype(BF16), w_proj, w_gate, wb, w_out.astype(BF16)


def kernel(x_prompt, x_sample, cache_diff_k, cache_diff_v, cache_win_k, cache_win_v, c, c_ctx,
           w_mod, b_mod, norm_g, ffn_w_in, ffn_w_out, w_in, diff_lambda, diff_subln_g, win_sink,
           conv_dw_w, conv_dw_b, conv_norm_g, conv_norm_b, w_branch, w_out, final_norm_g):
    n_ctx_b, n_ctx_t, D = x_prompt.shape
    n_lat_b, n_lat_t, _ = x_sample.shape
    past = cache_diff_k.shape[2]
    tm = 512

    n_rows = 1 + n_lat_b
    pad_rows = -n_rows % 8
    cond = jnp.concatenate([c_ctx[None, :], c, jnp.zeros((pad_rows, D), F32)], axis=0)
    mod_all = _modulation(cond, w_mod, b_mod.reshape(DEPTH, 1, 9 * D))

    rope_tables = _rope_tables(n_lat_t)
    ck_a = jnp.transpose(cache_diff_k, (0, 1, 3, 4, 5, 2)).reshape(n_lat_b, DEPTH, H_A, 2 * DH_A, past)
    cv_a = cache_diff_v.reshape(n_lat_b, DEPTH, past * H_A, 2 * DH_A)
    ck_b = jnp.transpose(cache_win_k, (0, 1, 3, 4, 2)).reshape(n_lat_b, DEPTH, N_KVB, past)
    cv_b = jnp.transpose(cache_win_v, (0, 1, 3, 4, 2)).reshape(n_lat_b, DEPTH, N_KVB, past)
    final_g = final_norm_g.reshape(1, D)

    xp = x_prompt.reshape(1, n_ctx_b * n_ctx_t, D)
    xs = x_sample
    new_cache = None
    ffn_in, ffn_out, w_proj, w_gate, wb, wo = _prepare_weights(ffn_w_in, ffn_w_out, w_in, w_branch, w_out)
    for l in range(DEPTH):
        lam_init = 0.8 - 0.6 * math.exp(-0.3 * l)
        g_l = norm_g[l]
        lam = diff_lambda[l]
        subln = diff_subln_g[l].reshape(1, 2 * DH_A)
        sink = win_sink[l]
        conv_p = (conv_dw_w[l], conv_dw_b[l].reshape(1, C_CONV),
                  conv_norm_g[l].reshape(1, C_CONV), conv_norm_b[l].reshape(1, C_CONV))
        mod_ctx = mod_all[l, 0:1].reshape(1, 9, D)
        mod_lat = mod_all[l, 1:n_rows].reshape(n_lat_b, 9, D)
        last = l == DEPTH - 1

        h = _ffn(xp, mod_ctx, g_l, ffn_in, ffn_out, (l, 0), mod_row=0, g_row=0, tm=tm)
        qa, ka, va, qb, kb, vb, u, *new_cache = _inproj(
            h, mod_ctx, g_l, w_proj, l, tm=tm, kv_dtype=F32,
            cache_seq=(n_ctx_b, n_ctx_t), cache_acc=new_cache)
        seq = lambda a: a.reshape(n_ctx_b, n_ctx_t, a.shape[-1])
        ka, va, kb, vb = seq(ka), seq(va), seq(kb), seq(vb)
        ya = _diff_attn(seq(qa), ka, va, lam, subln, lam_init=lam_init, tq=n_ctx_t, hb=H_A)
        yb = _win_attn(seq(qb), kb, vb, sink, tq=n_ctx_t, nq=1, nb=4, local=False)
        yc = _conv_module(seq(u), *conv_p, rt=n_ctx_t)
        flat = lambda a: a.reshape(1, n_ctx_b * n_ctx_t, a.shape[-1])
        h = _merge(h, mod_ctx, g_l, w_gate, flat(ya), flat(yb), flat(yc), wb, wo, l, tm=tm)
        xp = _ffn(h, mod_ctx, g_l, ffn_in, ffn_out, (l, 1), mod_row=6, g_row=2, tm=tm,
                  final_g=final_g if last else None)

        h = _ffn(xs, mod_lat, g_l, ffn_in, ffn_out, (l, 0), mod_row=0, g_row=0, tm=tm)
        qa, ka, va, qb, kb, vb, u = _inproj(h, mod_lat, g_l, w_proj, l, tm=tm, kv_dtype=BF16,
                                            rope_tables=rope_tables)
        ya = _diff_attn(qa, ka, va, lam, subln, lam_init=lam_init, tq=512, hb=1, cache=(ck_a, cv_a, l))
        yb = _win_attn(qb, kb, vb, sink, tq=WIN_BLOCK, nq=4, nb=1, local=True, cache=(ck_b, cv_b, l))
        yc = _conv_module(u, *conv_p, rt=256)
        h = _merge(h, mod_lat, g_l, w_gate, ya, yb, yc, wb, wo, l, tm=tm)
        xs = _ffn(h, mod_lat, g_l, ffn_in, ffn_out, (l, 1), mod_row=6, g_row=2, tm=tm,
                  final_g=final_g if last else None)

    y_prompt = xp.reshape(n_ctx_b, n_ctx_t, D)
    dk, dv, wk, wv = new_cache
    new_diff_k = jnp.transpose(dk.reshape(n_ctx_b, DEPTH, H_A, 2, DH_A, n_ctx_t), (0, 1, 5, 2, 3, 4))
    new_diff_v = dv.reshape(n_ctx_b, DEPTH, n_ctx_t, H_A, 2 * DH_A)
    new_win_k = jnp.transpose(wk.reshape(n_ctx_b, DEPTH, KV_B, DH_B, n_ctx_t), (0, 1, 4, 2, 3))
    new_win_v = jnp.transpose(wv.reshape(n_ctx_b, DEPTH, KV_B, DH_B, n_ctx_t), (0, 1, 4, 2, 3))
    return (y_prompt, xs, new_diff_k, new_diff_v, new_win_k, new_win_v)
```

```python
import functools
import itertools
import math

import numpy as np
import jax
import jax.numpy as jnp
from jax import lax
from jax.experimental import pallas as pl
from jax.experimental.pallas import tpu as pltpu

D_MODEL = 1024
DEPTH = 2
GRID_W = 64
H_A = 4
DH_A = 64
H_B = 8
KV_B = 2
G_B = H_B // KV_B
DH_B = 64
WINDOW = 128
WIN_BLOCK = 128
C_CONV = 512
CONV_K = 31
BRANCH_W = 512
D_FF = 2816
ROPE_BASE = 10000.0
NORM_EPS = 1e-6
SUBLN_EPS = 1e-5
SCALE_A = DH_A ** -0.5
SCALE_B = DH_B ** -0.5
LOG2E = math.log2(math.e)

N_QA = H_A * 2 * DH_A
N_KVB = KV_B * DH_B
N_ATTN = 3 * N_QA + H_B * DH_B + 2 * N_KVB
N_PROJ = N_ATTN + 2 * C_CONV

LANES = 128
SUBLANES = 8
ROPE_HALF = DH_A // 4
FF_CHUNK = 256
KEY_CHUNK = 256
CONV_HALO = 16
CONV_ROWS = 32
VMEM_LIMIT = 56 * 1024 * 1024

BF16 = jnp.bfloat16
F32 = jnp.float32


def _params(*sem):
    return pltpu.CompilerParams(dimension_semantics=sem, vmem_limit_bytes=VMEM_LIMIT)


def _resident(shape):
    nd = len(shape)
    return pl.BlockSpec(shape, lambda *_: (0,) * nd, pipeline_mode=pl.Buffered(1))


def _resident_slice(shape, lead):
    n = len(lead)
    block = (None,) * n + tuple(shape[n:])
    index = tuple(lead) + (0,) * (len(shape) - n)
    return pl.BlockSpec(block, lambda *_: index, pipeline_mode=pl.Buffered(1))


def _dot(a, b):
    return jnp.dot(a, b, preferred_element_type=F32)


def _dot_nt(a, b):
    return lax.dot_general(a, b, (((1,), (1,)), ((), ())), preferred_element_type=F32)


def _rms(x, eps):
    return x * lax.rsqrt(jnp.mean(x * x, axis=-1, keepdims=True) + eps)


def _mod_rms(x, g, shift, scale):
    return _rms(x, NORM_EPS) * g * (1.0 + scale) + shift


def _silu(x):
    return x * jax.nn.sigmoid(x)


def _mod_kernel(c_ref, w_ref, b_ref, o_ref):
    a = _silu(c_ref[...]).astype(BF16)
    o_ref[...] = _dot(a, w_ref[...].astype(BF16)) + b_ref[...]


def _modulation(cond, w_mod, b_mod):
    R = cond.shape[0]
    tn = D_MODEL
    return pl.pallas_call(
        _mod_kernel,
        grid=(DEPTH, 9 * D_MODEL // tn),
        in_specs=[
            pl.BlockSpec((R, D_MODEL), lambda l, j: (0, 0)),
            pl.BlockSpec((None, D_MODEL, tn), lambda l, j: (l, 0, j)),
            pl.BlockSpec((None, 1, tn), lambda l, j: (l, 0, j)),
        ],
        out_specs=pl.BlockSpec((None, R, tn), lambda l, j: (l, 0, j)),
        out_shape=jax.ShapeDtypeStruct((DEPTH, R, 9 * D_MODEL), F32),
        compiler_params=_params("parallel", "parallel"),
        name="modulation",
    )(cond, w_mod, b_mod)


def _ffn_kernel(x_ref, mod_ref, g_ref, win_ref, wout_ref, *rest, mod_row, g_row, final):
    if final:
        fg_ref, o_ref = rest
    else:
        (o_ref,) = rest
    x = x_ref[...]
    shift = mod_ref[mod_row:mod_row + 1, :]
    scale = mod_ref[mod_row + 1:mod_row + 2, :]
    gate = mod_ref[mod_row + 2:mod_row + 3, :]
    nb = _mod_rms(x, g_ref[g_row:g_row + 1, :], shift, scale).astype(BF16)
    acc = None
    for lo in range(0, D_FF, FF_CHUNK):
        a = _dot(nb, win_ref[:, lo:lo + FF_CHUNK])
        b = _dot(nb, win_ref[:, D_FF + lo:D_FF + lo + FF_CHUNK])
        hid = (_silu(a) * b).astype(BF16)
        part = _dot(hid, wout_ref[lo:lo + FF_CHUNK, :])
        acc = part if acc is None else acc + part
    h = x + 0.5 * gate * acc
    if final:
        h = _rms(h, NORM_EPS) * fg_ref[...]
    o_ref[...] = h


def _ffn(x, mod, norm_g, w_in, w_out, which, *, mod_row, g_row, tm, final_g=None):
    B, T, D = x.shape
    per_batch = mod.shape[0] > 1
    final = final_g is not None
    l, s = which
    in_specs = [
        pl.BlockSpec((None, tm, D), lambda b, i: (b, i, 0)),
        pl.BlockSpec((None, 9, D), (lambda b, i: (b, 0, 0)) if per_batch else (lambda b, i: (0, 0, 0))),
        _resident(norm_g.shape),
    ]
    in_specs += [_resident_slice(w_in.shape, (l, s)), _resident_slice(w_out.shape, (l, s))]
    args = [x, mod, norm_g, w_in, w_out]
    if final:
        in_specs.append(_resident(final_g.shape))
        args.append(final_g)
    return pl.pallas_call(
        functools.partial(_ffn_kernel, mod_row=mod_row, g_row=g_row, final=final),
        grid=(B, T // tm),
        in_specs=in_specs,
        out_specs=pl.BlockSpec((None, tm, D), lambda b, i: (b, i, 0)),
        out_shape=jax.ShapeDtypeStruct((B, T, D), F32),
        compiler_params=_params("parallel", "parallel"),
        name="ffn",
    )(*args)


def _rope(x, cos, sin_signed, lo_half):
    outs = []
    for j in range(x.shape[1] // LANES):
        xj = x[:, j * LANES:(j + 1) * LANES]
        rot = jnp.where(lo_half, pltpu.roll(xj, LANES - ROPE_HALF, 1), pltpu.roll(xj, ROPE_HALF, 1))
        outs.append(xj * cos + rot * sin_signed)
    return outs[0] if len(outs) == 1 else jnp.concatenate(outs, axis=1)


def _inproj_kernel(x_ref, mod_ref, g_ref, w_ref, *rest, rope, n_aliased, cache_out):
    if rope:
        cos_ref, sin_ref = rest[:2]
        rest = rest[2:]
    rest = rest[n_aliased:]
    qa_ref, ka_ref, va_ref, qb_ref, kb_ref, vb_ref, u_ref = rest[:7]
    x = x_ref[...]
    nb = _mod_rms(x, g_ref[1:2, :], mod_ref[3:4, :], mod_ref[4:5, :]).astype(BF16)

    def proj(lo, hi):
        return _dot(nb, w_ref[:, lo:hi])

    qa = proj(0, N_QA)
    ka = proj(N_QA, 2 * N_QA)
    va = proj(2 * N_QA, 3 * N_QA)
    qb = proj(3 * N_QA, 4 * N_QA)
    kvb = proj(4 * N_QA, N_ATTN)
    kb = kvb[:, :N_KVB]
    vb = kvb[:, N_KVB:]
    cu = proj(N_ATTN, N_PROJ)
    if rope:
        cos = cos_ref[...]
        sin_signed = sin_ref[...]
        lane = lax.broadcasted_iota(jnp.int32, cos.shape, 1)
        lo_half = (lane % (2 * ROPE_HALF)) < ROPE_HALF
        qa = _rope(qa, cos, sin_signed, lo_half)
        ka = _rope(ka, cos, sin_signed, lo_half)
        qb = _rope(qb, cos, sin_signed, lo_half)
        kb = _rope(kb, cos, sin_signed, lo_half)
    qa_ref[...] = (qa * (SCALE_A * LOG2E)).astype(qa_ref.dtype)
    ka_ref[...] = ka.astype(ka_ref.dtype)
    va_ref[...] = va.astype(va_ref.dtype)
    qb_ref[...] = (qb * (SCALE_B * LOG2E)).astype(qb_ref.dtype)
    kb_ref[...] = kb.astype(kb_ref.dtype)
    vb_ref[...] = vb.astype(vb_ref.dtype)
    u_ref[...] = cu[:, :C_CONV] * jax.nn.sigmoid(cu[:, C_CONV:])
    if cache_out:
        dk_ref, dv_ref, wk_ref, wv_ref = rest[7:]
        n_seq, _, _, seq_len = dk_ref.shape
        ka_t, kb_t, vb_t = ka.T, kb.T, vb.T
        for s in range(n_seq):
            tok = slice(s * seq_len, (s + 1) * seq_len)
            for hc in range(2 * H_A):
                dk_ref[s, hc] = ka_t[hc * DH_A:(hc + 1) * DH_A, tok]
            for hh in range(H_A):
                dv_ref[s, pl.ds(hh, seq_len, stride=H_A), :] = va[tok, hh * 2 * DH_A:(hh + 1) * 2 * DH_A]
            wk_ref[s] = kb_t[:, tok]
            wv_ref[s] = vb_t[:, tok]


def _inproj(h, mod, norm_g, w_proj, layer, *, tm, kv_dtype, rope_tables=None, cache_seq=None, cache_acc=None):
    B, T, D = h.shape
    per_batch = mod.shape[0] > 1
    rope = rope_tables is not None
    row = lambda b, i: (b, i, 0)
    in_specs = [
        pl.BlockSpec((None, tm, D), row),
        pl.BlockSpec((None, 9, D), (lambda b, i: (b, 0, 0)) if per_batch else (lambda b, i: (0, 0, 0))),
        _resident(norm_g.shape),
        _resident_slice(w_proj.shape, (layer,)),
    ]
    args = [h, mod, norm_g, w_proj]
    if rope:
        in_specs += [pl.BlockSpec((tm, LANES), lambda b, i: (i, 0))] * 2
        args += list(rope_tables)
    widths = (N_QA, N_QA, N_QA, H_B * DH_B, N_KVB, N_KVB, C_CONV)
    dtypes = (BF16, kv_dtype, kv_dtype, BF16, kv_dtype, kv_dtype, F32)
    out_specs = [pl.BlockSpec((None, tm, w), row) for w in widths]
    out_shape = [jax.ShapeDtypeStruct((B, T, w), dt) for w, dt in zip(widths, dtypes)]
    aliases = {}
    n_aliased = 0
    if cache_seq is not None:
        n_all, seq_len = cache_seq
        n_seq = tm // seq_len
        assert B == 1 and n_seq * seq_len == tm
        cache_shapes = ((2 * H_A, DH_A, seq_len), (seq_len * H_A, 2 * DH_A), (N_KVB, seq_len), (N_KVB, seq_len))
        for shape in cache_shapes:
            nd = len(shape)
            out_specs.append(pl.BlockSpec((n_seq, None) + shape,
                                          lambda b, i, nd=nd: (i, layer) + (0,) * nd))
            out_shape.append(jax.ShapeDtypeStruct((n_all, DEPTH) + shape, F32))
        if cache_acc is not None:
            n_aliased = len(cache_acc)
            aliases = {len(args) + j: len(widths) + j for j in range(n_aliased)}
            in_specs += [pl.BlockSpec(memory_space=pl.ANY)] * n_aliased
            args += list(cache_acc)
    return pl.pallas_call(
        functools.partial(_inproj_kernel, rope=rope, n_aliased=n_aliased, cache_out=cache_seq is not None),
        grid=(B, T // tm),
        in_specs=in_specs,
        out_specs=out_specs,
        out_shape=out_shape,
        input_output_aliases=aliases,
        compiler_params=_params("parallel", "parallel"),
        name="inproj",
    )(*args)


def _diff_attn_kernel(*refs, n_cache, lam_init, nq, hb):
    if n_cache:
        q_ref, k_ref, v_ref, ck_ref, cv_ref, lam_ref, g_ref, o_ref, s_even, s_odd, m_even, m_odd, vcache = refs
    else:
        q_ref, k_ref, v_ref, lam_ref, g_ref, o_ref, s_even, s_odd, m_even, m_odd = refs
    hd = 2 * DH_A
    t = pl.program_id(0)
    heads = [(hh, slice(hh * hd, (hh + 1) * hd)) for hh in range(hb)]

    @pl.when(t == 0)
    def _():
        s_odd[...] = jnp.zeros(s_odd.shape, F32)
        m_odd[...] = jnp.zeros(m_odd.shape, F32)
        if n_cache:
            vcache[...] = jnp.zeros(vcache.shape, BF16)

    if n_cache:
        @pl.when((t > 0) & ((t - 1) % nq == 0))
        def _():
            head = ((t - 1) // nq) % H_A
            vcache[...] = cv_ref[pl.ds(head, n_cache, stride=H_A), :].astype(BF16)

    def step(s_write, m_write, s_read, m_read):
        lam = lam_ref[...]
        lam_full = (jnp.exp(jnp.sum(lam[0:1] * lam[1:2], axis=1, keepdims=True))
                    - jnp.exp(jnp.sum(lam[2:3] * lam[3:4], axis=1, keepdims=True)) + lam_init)
        tq = q_ref.shape[0]
        lane = lax.broadcasted_iota(jnp.int32, (tq, hd), 1)
        zero = jnp.zeros((tq, hd), q_ref.dtype)
        for hh, lanes in heads:
            q = q_ref[:, lanes]
            q2 = jnp.concatenate([jnp.where(lane < DH_A, q, zero), jnp.where(lane >= DH_A, q, zero)], axis=0)
            row_max = jnp.concatenate([m_read[hh]] * (KEY_CHUNK // LANES), axis=1)
            acc = None
            run_max = None
            for lo in range(0, s_read.shape[2], KEY_CHUNK):
                cols = slice(lo, lo + KEY_CHUNK)
                if lo < n_cache:
                    s = _dot(q2, ck_ref[:, cols].astype(BF16))
                    v = vcache[cols, :]
                else:
                    own = slice(lo - n_cache, lo - n_cache + KEY_CHUNK)
                    s = _dot_nt(q2, k_ref[own, lanes].astype(BF16))
                    v = v_ref[own, lanes].astype(BF16)
                s_write[hh, :, cols] = s
                run_max = s if run_max is None else jnp.maximum(run_max, s)
                e = jnp.exp2(s_read[hh, :, cols] - row_max).astype(BF16)
                part = _dot(e, jnp.concatenate([v, jnp.ones(v.shape, BF16)], axis=1))
                acc = part if acc is None else acc + part
            m_write[hh] = jnp.broadcast_to(jnp.max(run_max, axis=1, keepdims=True), m_write.shape[1:])
            ratio = acc[:, :hd] / acc[:, hd:]
            o = ratio[:tq] - lam_full * ratio[tq:]
            o = _rms(o, SUBLN_EPS) * g_ref[...] * (1.0 - lam_init)
            o_ref[:, lanes] = o.astype(o_ref.dtype)

    pl.when(t % 2 == 0)(lambda: step(s_even, m_even, s_odd, m_odd))
    pl.when(t % 2 == 1)(lambda: step(s_odd, m_odd, s_even, m_even))


def _diff_attn(q, k, v, lam, subln_g, *, lam_init, tq, hb, cache=None):
    B, T, _ = q.shape
    n_cache = 0 if cache is None else cache[0].shape[-1]
    assert n_cache % KEY_CHUNK == 0 and (cache is None or hb == 1)
    hd = 2 * DH_A
    nq = T // tq
    n_groups = H_A // hb
    n_blocks = B * n_groups * nq

    def block_of(t):
        blk = jnp.clip(t, 0, n_blocks - 1)
        return blk // (n_groups * nq), (blk // nq) % n_groups, blk % nq

    def q_map(t):
        b, h, i = block_of(t)
        return b, i, h

    def kv_map(t):
        b, h, _ = block_of(t)
        return b, 0, h

    def out_map(t):
        return q_map(t - 1)

    in_specs = [
        pl.BlockSpec((None, tq, hb * hd), q_map),
        pl.BlockSpec((None, T, hb * hd), kv_map),
        pl.BlockSpec((None, T, hb * hd), lambda t: kv_map(t - 1)),
    ]
    args = [q, k, v]
    if cache is not None:
        ck, cv, layer = cache

        def ck_map(t):
            b, h, _ = block_of(t)
            return b, layer, h, 0, 0

        def cv_map(t):
            b, _, _ = block_of(t - 1)
            return b, layer, 0, 0

        in_specs += [pl.BlockSpec((None, None, None, hd, n_cache), ck_map),
                     pl.BlockSpec((None, None, n_cache * H_A, hd), cv_map)]
        args += [ck, cv]
    in_specs += [_resident(lam.shape), _resident(subln_g.shape)]
    args += [lam, subln_g]
    n_keys = n_cache + T
    return pl.pallas_call(
        functools.partial(_diff_attn_kernel, n_cache=n_cache, lam_init=lam_init, nq=nq, hb=hb),
        grid=(n_blocks + 1,),
        in_specs=in_specs,
        out_specs=pl.BlockSpec((None, tq, hb * hd), out_map),
        out_shape=jax.ShapeDtypeStruct((B, T, N_QA), BF16),
        scratch_shapes=[pltpu.VMEM((hb, 2 * tq, n_keys), F32), pltpu.VMEM((hb, 2 * tq, n_keys), F32),
                        pltpu.VMEM((hb, 2 * tq, LANES), F32), pltpu.VMEM((hb, 2 * tq, LANES), F32)]
        + ([pltpu.VMEM((n_cache, hd), BF16)] if n_cache else []),
        compiler_params=_params("arbitrary"),
        name="diff_attn",
    )(*args)


def _win_attn_kernel(*refs, local, has_ctx, seq_len, tq, nq, nb):
    if has_ctx:
        q_ref, k_ref, v_ref, ck_ref, cv_ref, sink_ref, o_ref = refs
        ck = ck_ref[...].astype(BF16)
        cv = cv_ref[...].astype(BF16)
        cv_ones = jnp.concatenate([cv, jnp.ones(cv.shape, BF16)], axis=0)
    else:
        q_ref, k_ref, v_ref, sink_ref, o_ref = refs
    n_loc = 3 * WIN_BLOCK
    lane = lax.broadcasted_iota(jnp.int32, (tq, LANES), 1)
    zero = jnp.zeros((tq, LANES), BF16)
    blocks = {}

    def block(bi, qi):
        if (bi, qi) in blocks:
            return blocks[bi, qi]
        blk = {"rows": slice(qi * tq, (qi + 1) * tq)}
        blk["q"] = q_ref[bi, blk["rows"], :]
        if local:
            n = pl.program_id(1) * nq + qi
            start = pl.multiple_of(jnp.clip((n - 1) * WIN_BLOCK, 0, seq_len - n_loc), WIN_BLOCK)
            blk["k"] = k_ref[bi, pl.ds(start, n_loc), :].astype(BF16)
            v = v_ref[bi, pl.ds(start, n_loc), :].astype(BF16)
            rows = lax.broadcasted_iota(jnp.int32, (tq, n_loc), 0)
            cols = lax.broadcasted_iota(jnp.int32, (tq, n_loc), 1)
            blk["valid"] = jnp.abs(cols - rows - (n * WIN_BLOCK - start)) <= WINDOW
        else:
            blk["k"] = k_ref[bi].astype(BF16)
            v = v_ref[bi].astype(BF16)
        blk["v_ones"] = jnp.concatenate([v, jnp.ones(v.shape, BF16)], axis=1)
        blocks[bi, qi] = blk
        return blk

    def scores(bi, qi, j):
        blk = block(bi, qi)
        sel = (lane >= j * DH_B) & (lane < (j + 1) * DH_B)
        qj = jnp.concatenate(
            [jnp.where(sel, blk["q"][:, g * LANES:(g + 1) * LANES], zero) for g in range(G_B)], axis=0)
        s = _dot_nt(qj, blk["k"])
        if has_ctx:
            s = jnp.concatenate([s, _dot(qj, ck)], axis=1)
        return s

    def softmax(bi, qi, j, s):
        es, sink_terms = [], []
        for g in range(G_B):
            sg = s[g * tq:(g + 1) * tq]
            if local:
                band = jnp.where(block(bi, qi)["valid"], sg[:, :n_loc], -jnp.inf)
                sg = jnp.concatenate([band, sg[:, n_loc:]], axis=1) if has_ctx else band
            sink = sink_ref[j * G_B + g] * LOG2E
            m = jnp.maximum(jnp.max(sg, axis=1, keepdims=True), sink)
            es.append(jnp.exp2(sg - m).astype(BF16))
            sink_terms.append(jnp.exp2(sink - m))
        return jnp.concatenate(es, axis=0), sink_terms

    def values(bi, qi, e):
        blk = block(bi, qi)
        n_own = blk["k"].shape[0]
        ov = _dot(e[:, :n_own], blk["v_ones"])
        if has_ctx:
            ov = ov + _dot_nt(e[:, n_own:], cv_ones)
        return ov

    def finish(bi, qi, halves):
        for g in range(G_B):
            rows_g = slice(g * tq, (g + 1) * tq)
            outs = [ov[rows_g, :LANES] / (ov[rows_g, LANES:] + sink_terms[g]) for ov, sink_terms in halves]
            og = jnp.where(lane < DH_B, outs[0], outs[1])
            o_ref[bi, block(bi, qi)["rows"], g * LANES:(g + 1) * LANES] = og.astype(o_ref.dtype)

    units = list(itertools.product(range(nb), range(nq), range(KV_B)))
    pending_s, pending_e, done = {}, {}, {}
    for i in range(len(units) + 2):
        if i < len(units):
            pending_s[i] = scores(*units[i])
        if 0 <= i - 1 < len(units):
            pending_e[i - 1] = softmax(*units[i - 1], pending_s.pop(i - 1))
        if 0 <= i - 2 < len(units):
            bi, qi, j = units[i - 2]
            e, sink_terms = pending_e.pop(i - 2)
            done[bi, qi, j] = (values(bi, qi, e), sink_terms)
            if j == KV_B - 1:
                finish(bi, qi, [done.pop((bi, qi, jj)) for jj in range(KV_B)])


def _win_attn(q, k, v, sink, *, tq, nq, nb, local, cache=None):
    B, T, _ = q.shape
    has_ctx = cache is not None
    assert not local or tq == WIN_BLOCK
    assert not has_ctx or nb == 1
    rows = nq * tq
    in_specs = [
        pl.BlockSpec((nb, rows, G_B * LANES), lambda b, n: (b, n, 0)),
        pl.BlockSpec((nb, T, N_KVB), lambda b, n: (b, 0, 0)),
        pl.BlockSpec((nb, T, N_KVB), lambda b, n: (b, 0, 0)),
    ]
    args = [q, k, v]
    if has_ctx:
        ck, cv, layer = cache
        spec = pl.BlockSpec((None, None, N_KVB, ck.shape[3]), lambda b, n: (b, layer, 0, 0))
        in_specs += [spec, spec]
        args += [ck, cv]
    in_specs.append(pl.BlockSpec(memory_space=pltpu.SMEM))
    args.append(sink)
    return pl.pallas_call(
        functools.partial(_win_attn_kernel, local=local, has_ctx=has_ctx, seq_len=T, tq=tq, nq=nq, nb=nb),
        grid=(B // nb, T // rows),
        in_specs=in_specs,
        out_specs=pl.BlockSpec((nb, rows, G_B * LANES), lambda b, n: (b, n, 0)),
        out_shape=jax.ShapeDtypeStruct((B, T, H_B * DH_B), BF16),
        compiler_params=_params("parallel", "parallel"),
        name="win_attn",
    )(*args)


def _conv_kernel(prev_ref, cur_ref, next_ref, w_ref, bdw_ref, g_ref, b_ref, o_ref, win_ref, ph_ref):
    i = pl.program_id(1)
    rt = cur_ref.shape[0]
    zeros = jnp.zeros((CONV_HALO, C_CONV), F32)
    win_ref[0:CONV_HALO, :] = jnp.where(i > 0, prev_ref[...], zeros)
    win_ref[CONV_HALO:CONV_HALO + rt, :] = cur_ref[...]
    win_ref[CONV_HALO + rt:, :] = jnp.where(i < pl.num_programs(1) - 1, next_ref[...], zeros)
    n_ph = ph_ref.shape[1]
    for p in range(1, SUBLANES):
        ph_ref[p - 1] = win_ref[p:p + n_ph, :]
    first = CONV_HALO - CONV_K // 2
    tiles = CONV_ROWS // SUBLANES
    for r0 in range(0, rt, CONV_ROWS):
        acc = jnp.zeros((tiles, SUBLANES, C_CONV), F32) + bdw_ref[...]
        for t in range(CONV_K):
            p = (first + t) % SUBLANES
            base = r0 + first + t - p
            x = win_ref[base:base + CONV_ROWS, :] if p == 0 else ph_ref[p - 1, base:base + CONV_ROWS, :]
            acc = acc + x.reshape(tiles, SUBLANES, C_CONV) * w_ref[t]
        acc = acc.reshape(CONV_ROWS, C_CONV)
        mu = jnp.mean(acc, axis=-1, keepdims=True)
        cen = acc - mu
        var = jnp.mean(cen * cen, axis=-1, keepdims=True)
        y = cen * lax.rsqrt(var + NORM_EPS) * g_ref[...] + b_ref[...]
        o_ref[r0:r0 + CONV_ROWS, :] = _silu(y).astype(o_ref.dtype)


def _conv_module(u, w_dw, b_dw, g, b, *, rt):
    B, T, C = u.shape
    hb = rt // CONV_HALO
    last = T // CONV_HALO - 1
    n_ph = rt + (CONV_HALO + CONV_K // 2) // SUBLANES * SUBLANES
    w_tiles = jnp.broadcast_to(w_dw[:, None, :], (CONV_K, SUBLANES, C))
    in_specs = [
        pl.BlockSpec((None, CONV_HALO, C), lambda bb, i: (bb, jnp.maximum(i * hb - 1, 0), 0)),
        pl.BlockSpec((None, rt, C), lambda bb, i: (bb, i, 0)),
        pl.BlockSpec((None, CONV_HALO, C), lambda bb, i: (bb, jnp.minimum((i + 1) * hb, last), 0)),
        _resident(w_tiles.shape), _resident(b_dw.shape), _resident(g.shape), _resident(b.shape),
    ]
    return pl.pallas_call(
        _conv_kernel,
        grid=(B, T // rt),
        in_specs=in_specs,
        out_specs=pl.BlockSpec((None, rt, C), lambda bb, i: (bb, i, 0)),
        out_shape=jax.ShapeDtypeStruct((B, T, C), BF16),
        scratch_shapes=[pltpu.VMEM((rt + 2 * CONV_HALO, C), F32),
                        pltpu.VMEM((SUBLANES - 1, n_ph, C), F32)],
        compiler_params=_params("parallel", "parallel"),
        name="conv_module",
    )(u, u, u, w_tiles, b_dw, g, b)


def _merge_kernel(h_ref, mod_ref, g_ref, ya_ref, yb_ref, yc_ref, wg_ref, wb_ref, wo_ref, o_ref):
    h = h_ref[...]
    nb = _mod_rms(h, g_ref[1:2, :], mod_ref[3:4, :], mod_ref[4:5, :]).astype(BF16)
    merged = None
    for i, y_ref in enumerate((ya_ref, yb_ref, yc_ref)):
        gate = jax.nn.sigmoid(_dot(nb, wg_ref[:, i * D_MODEL:(i + 1) * D_MODEL]))
        term = gate * _dot(y_ref[...], wb_ref[i])
        merged = term if merged is None else merged + term
    o_ref[...] = h + mod_ref[5:6, :] * _dot(merged.astype(BF16), wo_ref[...])


def _merge(h, mod, norm_g, w_gate, ya, yb, yc, w_branch, w_out, layer, *, tm):
    B, T, D = h.shape
    per_batch = mod.shape[0] > 1
    row = lambda b, i: (b, i, 0)
    y_spec = pl.BlockSpec((None, tm, BRANCH_W), row)
    return pl.pallas_call(
        _merge_kernel,
        grid=(B, T // tm),
        in_specs=[
            pl.BlockSpec((None, tm, D), row),
            pl.BlockSpec((None, 9, D), (lambda b, i: (b, 0, 0)) if per_batch else (lambda b, i: (0, 0, 0))),
            _resident(norm_g.shape), y_spec, y_spec, y_spec,
            _resident_slice(w_gate.shape, (layer,)), _resident_slice(w_branch.shape, (layer,)),
            _resident_slice(w_out.shape, (layer,)),
        ],
        out_specs=pl.BlockSpec((None, tm, D), row),
        out_shape=jax.ShapeDtypeStruct((B, T, D), F32),
        compiler_params=_params("parallel", "parallel"),
        name="merge",
    )(h, mod, norm_g, ya, yb, yc, w_gate, w_branch, w_out)


def _rope_tables(n_tokens):
    quarter = DH_A // 4
    t = np.arange(n_tokens)
    inv = (ROPE_BASE ** (-np.arange(0, DH_A // 2, 2, dtype=np.float32) / (DH_A // 2))).astype(np.float32)
    pos = np.stack([(t // GRID_W).astype(np.float32), (t % GRID_W).astype(np.float32)], axis=1)
    ang = (pos[:, :, None] * inv[None, None, :]).astype(np.float32)
    ang = np.broadcast_to(ang[:, :, None, :], (n_tokens, 2, 2, quarter)).reshape(n_tokens, DH_A)
    sign = np.where((np.arange(DH_A) % (2 * quarter)) < quarter, -1.0, 1.0).astype(np.float32)
    cos = np.cos(ang).astype(np.float32)
    sin_signed = (np.sin(ang) * sign).astype(np.float32)
    reps = LANES // DH_A
    return jnp.asarray(np.tile(cos, (1, reps))), jnp.asarray(np.tile(sin_signed, (1, reps)))


def _group_major(w, axis):
    shape = w.shape
    split = shape[:axis] + (KV_B, G_B, DH_B) + shape[axis + 1:]
    return jnp.swapaxes(w.reshape(split), axis, axis + 1).reshape(shape)


def _prepare_weights(ffn_w_in, ffn_w_out, w_in, w_branch, w_out):
    qb_lo = 3 * N_QA
    qb_hi = qb_lo + H_B * DH_B
    w_proj = jnp.concatenate(
        [w_in[:, :, :qb_lo], _group_major(w_in[:, :, qb_lo:qb_hi], 2), w_in[:, :, qb_hi:N_PROJ]],
        axis=2).astype(BF16)
    w_gate = w_in[:, :, N_PROJ:].astype(BF16)
    wb = jnp.concatenate(
        [w_branch[:, 0:1], _group_major(w_branch[:, 1:2], 2), w_branch[:, 2:3]], axis=1).astype(BF16)
    return ffn_w_in.astype(BF16), ffn_w_out.astype(BF16), w_proj, w_gate, wb, w_out.astype(BF16)


def kernel(x_prompt, x_sample, cache_diff_k, cache_diff_v, cache_win_k, cache_win_v, c, c_ctx,
           w_mod, b_mod, norm_g, ffn_w_in, ffn_w_out, w_in, diff_lambda, diff_subln_g, win_sink,
           conv_dw_w, conv_dw_b, conv_norm_g, conv_norm_b, w_branch, w_out, final_norm_g):
    n_ctx_b, n_ctx_t, D = x_prompt.shape
    n_lat_b, n_lat_t, _ = x_sample.shape
    past = cache_diff_k.shape[2]
    tm = 512
    tm_ffn = 1024

    n_rows = 1 + n_lat_b
    pad_rows = -n_rows % 8
    cond = jnp.concatenate([c_ctx[None, :], c, jnp.zeros((pad_rows, D), F32)], axis=0)
    mod_all = _modulation(cond, w_mod, b_mod.reshape(DEPTH, 1, 9 * D))

    rope_tables = _rope_tables(n_lat_t)
    ck_a = jnp.transpose(cache_diff_k, (0, 1, 3, 4, 5, 2)).reshape(n_lat_b, DEPTH, H_A, 2 * DH_A, past)
    cv_a = cache_diff_v.reshape(n_lat_b, DEPTH, past * H_A, 2 * DH_A)
    ck_b = jnp.transpose(cache_win_k, (0, 1, 3, 4, 2)).reshape(n_lat_b, DEPTH, N_KVB, past)
    cv_b = jnp.transpose(cache_win_v, (0, 1, 3, 4, 2)).reshape(n_lat_b, DEPTH, N_KVB, past)
    final_g = final_norm_g.reshape(1, D)

    xp = x_prompt.reshape(1, n_ctx_b * n_ctx_t, D)
    xs = x_sample
    new_cache = None
    ffn_in, ffn_out, w_proj, w_gate, wb, wo = _prepare_weights(ffn_w_in, ffn_w_out, w_in, w_branch, w_out)
    for l in range(DEPTH):
        lam_init = 0.8 - 0.6 * math.exp(-0.3 * l)
        g_l = norm_g[l]
        lam = diff_lambda[l]
        subln = diff_subln_g[l].reshape(1, 2 * DH_A)
        sink = win_sink[l]
        conv_p = (conv_dw_w[l], conv_dw_b[l].reshape(1, C_CONV),
                  conv_norm_g[l].reshape(1, C_CONV), conv_norm_b[l].reshape(1, C_CONV))
        mod_ctx = mod_all[l, 0:1].reshape(1, 9, D)
        mod_lat = mod_all[l, 1:n_rows].reshape(n_lat_b, 9, D)
        last = l == DEPTH - 1

        h = _ffn(xp, mod_ctx, g_l, ffn_in, ffn_out, (l, 0), mod_row=0, g_row=0, tm=tm_ffn)
        qa, ka, va, qb, kb, vb, u, *new_cache = _inproj(
            h, mod_ctx, g_l, w_proj, l, tm=tm, kv_dtype=F32,
            cache_seq=(n_ctx_b, n_ctx_t), cache_acc=new_cache)
        seq = lambda a: a.reshape(n_ctx_b, n_ctx_t, a.shape[-1])
        ka, va, kb, vb = seq(ka), seq(va), seq(kb), seq(vb)
        ya = _diff_attn(seq(qa), ka, va, lam, subln, lam_init=lam_init, tq=n_ctx_t, hb=H_A)
        yb = _win_attn(seq(qb), kb, vb, sink, tq=n_ctx_t, nq=1, nb=4, local=False)
        yc = _conv_module(seq(u), *conv_p, rt=n_ctx_t)
        flat = lambda a: a.reshape(1, n_ctx_b * n_ctx_t, a.shape[-1])
        h = _merge(h, mod_ctx, g_l, w_gate, flat(ya), flat(yb), flat(yc), wb, wo, l, tm=tm)
        xp = _ffn(h, mod_ctx, g_l, ffn_in, ffn_out, (l, 1), mod_row=6, g_row=2, tm=tm_ffn,
                  final_g=final_g if last else None)

        h = _ffn(xs, mod_lat, g_l, ffn_in, ffn_out, (l, 0), mod_row=0, g_row=0, tm=tm_ffn)
        qa, ka, va, qb, kb, vb, u = _inproj(h, mod_lat, g_l, w_proj, l, tm=tm, kv_dtype=BF16,
                                            rope_tables=rope_tables)
        ya = _diff_attn(qa, ka, va, lam, subln, lam_init=lam_init, tq=512, hb=1, cache=(ck_a, cv_a, l))
        yb = _win_attn(qb, kb, vb, sink, tq=WIN_BLOCK, nq=4, nb=1, local=True, cache=(ck_b, cv_b, l))
        yc = _conv_module(u, *conv_p, rt=256)
        h = _merge(h, mod_lat, g_l, w_gate, ya, yb, yc, wb, wo, l, tm=tm)
        xs = _ffn(h, mod_lat, g_l, ffn_in, ffn_out, (l, 1), mod_row=6, g_row=2, tm=tm_ffn,
                  final_g=final_g if last else None)

    y_prompt = xp.reshape(n_ctx_b, n_ctx_t, D)
    dk, dv, wk, wv = new_cache
    new_diff_k = jnp.transpose(dk.reshape(n_ctx_b, DEPTH, H_A, 2, DH_A, n_ctx_t), (0, 1, 5, 2, 3, 4))
    new_diff_v = dv.reshape(n_ctx_b, DEPTH, n_ctx_t, H_A, 2 * DH_A)
    new_win_k = jnp.transpose(wk.reshape(n_ctx_b, DEPTH, KV_B, DH_B, n_ctx_t), (0, 1, 4, 2, 3))
    new_win_v = jnp.transpose(wv.reshape(n_ctx_b, DEPTH, KV_B, DH_B, n_ctx_t), (0, 1, 4, 2, 3))
    return (y_prompt, xs, new_diff_k, new_diff_v, new_win_k, new_win_v)
```

```python
import functools
import itertools
import math

import numpy as np
import jax
import jax.numpy as jnp
from jax import lax
from jax.experimental import pallas as pl
from jax.experimental.pallas import tpu as pltpu

D_MODEL = 1024
DEPTH = 2
GRID_W = 64
H_A = 4
DH_A = 64
H_B = 8
KV_B = 2
G_B = H_B // KV_B
DH_B = 64
WINDOW = 128
WIN_BLOCK = 128
C_CONV = 512
CONV_K = 31
BRANCH_W = 512
D_FF = 2816
ROPE_BASE = 10000.0
NORM_EPS = 1e-6
SUBLN_EPS = 1e-5
SCALE_A = DH_A ** -0.5
SCALE_B = DH_B ** -0.5
LOG2E = math.log2(math.e)

N_QA = H_A * 2 * DH_A
N_KVB = KV_B * DH_B
N_ATTN = 3 * N_QA + H_B * DH_B + 2 * N_KVB
N_PROJ = N_ATTN + 2 * C_CONV

LANES = 128
SUBLANES = 8
ROPE_HALF = DH_A // 4
FF_CHUNK = 256
KEY_CHUNK = 256
WIN_SKEW = 1
CONV_HALO = 16
CONV_ROWS = 32
VMEM_LIMIT = 56 * 1024 * 1024

BF16 = jnp.bfloat16
F32 = jnp.float32


def _params(*sem):
    return pltpu.CompilerParams(dimension_semantics=sem, vmem_limit_bytes=VMEM_LIMIT)


def _resident(shape):
    nd = len(shape)
    return pl.BlockSpec(shape, lambda *_: (0,) * nd, pipeline_mode=pl.Buffered(1))


def _resident_slice(shape, lead):
    n = len(lead)
    block = (None,) * n + tuple(shape[n:])
    index = tuple(lead) + (0,) * (len(shape) - n)
    return pl.BlockSpec(block, lambda *_: index, pipeline_mode=pl.Buffered(1))


def _dot(a, b):
    return jnp.dot(a, b, preferred_element_type=F32)


def _dot_nt(a, b):
    return lax.dot_general(a, b, (((1,), (1,)), ((), ())), preferred_element_type=F32)


def _rms(x, eps):
    return x * lax.rsqrt(jnp.mean(x * x, axis=-1, keepdims=True) + eps)


def _mod_rms(x, g, shift, scale):
    return _rms(x, NORM_EPS) * g * (1.0 + scale) + shift


def _silu(x):
    return x * jax.nn.sigmoid(x)


def _mod_kernel(c_ref, w_ref, b_ref, o_ref):
    a = _silu(c_ref[...]).astype(BF16)
    o_ref[...] = _dot(a, w_ref[...].astype(BF16)) + b_ref[...]


def _modulation(cond, w_mod, b_mod):
    R = cond.shape[0]
    tn = D_MODEL
    return pl.pallas_call(
        _mod_kernel,
        grid=(DEPTH, 9 * D_MODEL // tn),
        in_specs=[
            pl.BlockSpec((R, D_MODEL), lambda l, j: (0, 0)),
            pl.BlockSpec((None, D_MODEL, tn), lambda l, j: (l, 0, j)),
            pl.BlockSpec((None, 1, tn), lambda l, j: (l, 0, j)),
        ],
        out_specs=pl.BlockSpec((None, R, tn), lambda l, j: (l, 0, j)),
        out_shape=jax.ShapeDtypeStruct((DEPTH, R, 9 * D_MODEL), F32),
        compiler_params=_params("parallel", "parallel"),
        name="modulation",
    )(cond, w_mod, b_mod)


def _ffn_kernel(x_ref, mod_ref, g_ref, win_ref, wout_ref, *rest, mod_row, g_row, final):
    if final:
        fg_ref, o_ref = rest
    else:
        (o_ref,) = rest
    x = x_ref[...]
    shift = mod_ref[mod_row:mod_row + 1, :]
    scale = mod_ref[mod_row + 1:mod_row + 2, :]
    gate = mod_ref[mod_row + 2:mod_row + 3, :]
    nb = _mod_rms(x, g_ref[g_row:g_row + 1, :], shift, scale).astype(BF16)
    acc = None
    for lo in range(0, D_FF, FF_CHUNK):
        a = _dot(nb, win_ref[:, lo:lo + FF_CHUNK])
        b = _dot(nb, win_ref[:, D_FF + lo:D_FF + lo + FF_CHUNK])
        hid = (_silu(a) * b).astype(BF16)
        part = _dot(hid, wout_ref[lo:lo + FF_CHUNK, :])
        acc = part if acc is None else acc + part
    h = x + 0.5 * gate * acc
    if final:
        h = _rms(h, NORM_EPS) * fg_ref[...]
    o_ref[...] = h


def _ffn(x, mod, norm_g, w_in, w_out, which, *, mod_row, g_row, tm, final_g=None):
    B, T, D = x.shape
    per_batch = mod.shape[0] > 1
    final = final_g is not None
    l, s = which
    in_specs = [
        pl.BlockSpec((None, tm, D), lambda b, i: (b, i, 0)),
        pl.BlockSpec((None, 9, D), (lambda b, i: (b, 0, 0)) if per_batch else (lambda b, i: (0, 0, 0))),
        _resident(norm_g.shape),
    ]
    in_specs += [_resident_slice(w_in.shape, (l, s)), _resident_slice(w_out.shape, (l, s))]
    args = [x, mod, norm_g, w_in, w_out]
    if final:
        in_specs.append(_resident(final_g.shape))
        args.append(final_g)
    return pl.pallas_call(
        functools.partial(_ffn_kernel, mod_row=mod_row, g_row=g_row, final=final),
        grid=(B, T // tm),
        in_specs=in_specs,
        out_specs=pl.BlockSpec((None, tm, D), lambda b, i: (b, i, 0)),
        out_shape=jax.ShapeDtypeStruct((B, T, D), F32),
        compiler_params=_params("parallel", "parallel"),
        name="ffn",
    )(*args)


def _rope(x, cos, sin_signed, lo_half):
    outs = []
    for j in range(x.shape[1] // LANES):
        xj = x[:, j * LANES:(j + 1) * LANES]
        rot = jnp.where(lo_half, pltpu.roll(xj, LANES - ROPE_HALF, 1), pltpu.roll(xj, ROPE_HALF, 1))
        outs.append(xj * cos + rot * sin_signed)
    return outs[0] if len(outs) == 1 else jnp.concatenate(outs, axis=1)


def _inproj_kernel(x_ref, mod_ref, g_ref, w_ref, *rest, rope, n_aliased, cache_out, cache_layer):
    if rope:
        cos_ref, sin_ref = rest[:2]
        rest = rest[2:]
    rest = rest[n_aliased:]
    qa_ref, ka_ref, va_ref, qb_ref, kb_ref, vb_ref, u_ref = rest[:7]
    x = x_ref[...]
    nb = _mod_rms(x, g_ref[1:2, :], mod_ref[3:4, :], mod_ref[4:5, :]).astype(BF16)

    def proj(lo, hi):
        return _dot(nb, w_ref[:, lo:hi])

    qa = proj(0, N_QA)
    ka = proj(N_QA, 2 * N_QA)
    va = proj(2 * N_QA, 3 * N_QA)
    qb = proj(3 * N_QA, 4 * N_QA)
    kvb = proj(4 * N_QA, N_ATTN)
    kb = kvb[:, :N_KVB]
    vb = kvb[:, N_KVB:]
    cu = proj(N_ATTN, N_PROJ)
    if rope:
        cos = cos_ref[...]
        sin_signed = sin_ref[...]
        lane = lax.broadcasted_iota(jnp.int32, cos.shape, 1)
        lo_half = (lane % (2 * ROPE_HALF)) < ROPE_HALF
        qa = _rope(qa, cos, sin_signed, lo_half)
        ka = _rope(ka, cos, sin_signed, lo_half)
        qb = _rope(qb, cos, sin_signed, lo_half)
        kb = _rope(kb, cos, sin_signed, lo_half)
    qa_ref[...] = (qa * (SCALE_A * LOG2E)).astype(qa_ref.dtype)
    ka_ref[...] = ka.astype(ka_ref.dtype)
    va_ref[...] = va.astype(va_ref.dtype)
    qb_ref[...] = (qb * (SCALE_B * LOG2E)).astype(qb_ref.dtype)
    kb_ref[...] = kb.astype(kb_ref.dtype)
    vb_ref[...] = vb.astype(vb_ref.dtype)
    u_ref[...] = cu[:, :C_CONV] * jax.nn.sigmoid(cu[:, C_CONV:])
    if cache_out:
        cache_refs = rest[7:]
        if n_aliased == 0:
            for ref in cache_refs:
                for other in range(ref.shape[1]):
                    if other != cache_layer:
                        ref[:, other] = jnp.zeros(ref.shape[:1] + ref.shape[2:], F32)
            cache_refs = [ref.at[:, cache_layer] for ref in cache_refs]
        dk_ref, dv_ref, wk_ref, wv_ref = cache_refs
        n_seq, _, _, seq_len = dk_ref.shape
        ka_t, kb_t, vb_t = ka.T, kb.T, vb.T
        for s in range(n_seq):
            tok = slice(s * seq_len, (s + 1) * seq_len)
            for hc in range(2 * H_A):
                dk_ref[s, hc] = ka_t[hc * DH_A:(hc + 1) * DH_A, tok]
            for hh in range(H_A):
                dv_ref[s, pl.ds(hh, seq_len, stride=H_A), :] = va[tok, hh * 2 * DH_A:(hh + 1) * 2 * DH_A]
            wk_ref[s] = kb_t[:, tok]
            wv_ref[s] = vb_t[:, tok]


def _inproj(h, mod, norm_g, w_proj, layer, *, tm, kv_dtype, rope_tables=None, cache_seq=None, cache_acc=None):
    B, T, D = h.shape
    per_batch = mod.shape[0] > 1
    rope = rope_tables is not None
    row = lambda b, i: (b, i, 0)
    in_specs = [
        pl.BlockSpec((None, tm, D), row),
        pl.BlockSpec((None, 9, D), (lambda b, i: (b, 0, 0)) if per_batch else (lambda b, i: (0, 0, 0))),
        _resident(norm_g.shape),
        _resident_slice(w_proj.shape, (layer,)),
    ]
    args = [h, mod, norm_g, w_proj]
    if rope:
        in_specs += [pl.BlockSpec((tm, LANES), lambda b, i: (i, 0))] * 2
        args += list(rope_tables)
    widths = (N_QA, N_QA, N_QA, H_B * DH_B, N_KVB, N_KVB, C_CONV)
    dtypes = (BF16, kv_dtype, kv_dtype, BF16, kv_dtype, kv_dtype, F32)
    out_specs = [pl.BlockSpec((None, tm, w), row) for w in widths]
    out_shape = [jax.ShapeDtypeStruct((B, T, w), dt) for w, dt in zip(widths, dtypes)]
    aliases = {}
    n_aliased = 0
    if cache_seq is not None:
        n_all, seq_len = cache_seq
        n_seq = tm // seq_len
        assert B == 1 and n_seq * seq_len == tm
        cache_shapes = ((2 * H_A, DH_A, seq_len), (seq_len * H_A, 2 * DH_A), (N_KVB, seq_len), (N_KVB, seq_len))
        for shape in cache_shapes:
            nd = len(shape)
            if cache_acc is None:
                spec = pl.BlockSpec((n_seq, DEPTH) + shape, lambda b, i, nd=nd: (i, 0) + (0,) * nd)
            else:
                spec = pl.BlockSpec((n_seq, None) + shape, lambda b, i, nd=nd: (i, layer) + (0,) * nd)
            out_specs.append(spec)
            out_shape.append(jax.ShapeDtypeStruct((n_all, DEPTH) + shape, F32))
        if cache_acc is not None:
            n_aliased = len(cache_acc)
            aliases = {len(args) + j: len(widths) + j for j in range(n_aliased)}
            in_specs += [pl.BlockSpec(memory_space=pl.ANY)] * n_aliased
            args += list(cache_acc)
    return pl.pallas_call(
        functools.partial(_inproj_kernel, rope=rope, n_aliased=n_aliased, cache_out=cache_seq is not None,
                          cache_layer=layer),
        grid=(B, T // tm),
        in_specs=in_specs,
        out_specs=out_specs,
        out_shape=out_shape,
        input_output_aliases=aliases,
        compiler_params=_params("parallel", "parallel"),
        name="inproj",
    )(*args)


def _diff_attn_kernel(*refs, n_cache, lam_init, nq, hb):
    if n_cache:
        q_ref, k_ref, v_ref, ck_ref, cv_ref, lam_ref, g_ref, o_ref, s_even, s_odd, m_even, m_odd, vcache = refs
    else:
        q_ref, k_ref, v_ref, lam_ref, g_ref, o_ref, s_even, s_odd, m_even, m_odd = refs
    hd = 2 * DH_A
    t = pl.program_id(0)
    heads = [(hh, slice(hh * hd, (hh + 1) * hd)) for hh in range(hb)]

    @pl.when(t == 0)
    def _():
        s_odd[...] = jnp.zeros(s_odd.shape, F32)
        m_odd[...] = jnp.zeros(m_odd.shape, F32)
        if n_cache:
            vcache[...] = jnp.zeros(vcache.shape, BF16)

    if n_cache:
        @pl.when((t > 0) & ((t - 1) % nq == 0))
        def _():
            head = ((t - 1) // nq) % H_A
            vcache[...] = cv_ref[pl.ds(head, n_cache, stride=H_A), :].astype(BF16)

    def step(s_write, m_write, s_read, m_read):
        lam = lam_ref[...]
        lam_full = (jnp.exp(jnp.sum(lam[0:1] * lam[1:2], axis=1, keepdims=True))
                    - jnp.exp(jnp.sum(lam[2:3] * lam[3:4], axis=1, keepdims=True)) + lam_init)
        tq = q_ref.shape[0]
        lane = lax.broadcasted_iota(jnp.int32, (tq, hd), 1)
        zero = jnp.zeros((tq, hd), q_ref.dtype)
        for hh, lanes in heads:
            q = q_ref[:, lanes]
            q2 = jnp.concatenate([jnp.where(lane < DH_A, q, zero), jnp.where(lane >= DH_A, q, zero)], axis=0)
            row_max = jnp.concatenate([m_read[hh]] * (KEY_CHUNK // LANES), axis=1)
            acc = None
            run_max = None
            for lo in range(0, s_read.shape[2], KEY_CHUNK):
                cols = slice(lo, lo + KEY_CHUNK)
                if lo < n_cache:
                    s = _dot(q2, ck_ref[:, cols].astype(BF16))
                    v = vcache[cols, :]
                else:
                    own = slice(lo - n_cache, lo - n_cache + KEY_CHUNK)
                    s = _dot_nt(q2, k_ref[own, lanes].astype(BF16))
                    v = v_ref[own, lanes].astype(BF16)
                s_write[hh, :, cols] = s
                run_max = s if run_max is None else jnp.maximum(run_max, s)
                e = jnp.exp2(s_read[hh, :, cols] - row_max).astype(BF16)
                part = _dot(e, jnp.concatenate([v, jnp.ones(v.shape, BF16)], axis=1))
                acc = part if acc is None else acc + part
            m_write[hh] = jnp.broadcast_to(jnp.max(run_max, axis=1, keepdims=True), m_write.shape[1:])
            ratio = acc[:, :hd] / acc[:, hd:]
            o = ratio[:tq] - lam_full * ratio[tq:]
            o = _rms(o, SUBLN_EPS) * g_ref[...] * (1.0 - lam_init)
            o_ref[:, lanes] = o.astype(o_ref.dtype)

    pl.when(t % 2 == 0)(lambda: step(s_even, m_even, s_odd, m_odd))
    pl.when(t % 2 == 1)(lambda: step(s_odd, m_odd, s_even, m_even))


def _diff_attn(q, k, v, lam, subln_g, *, lam_init, tq, hb, cache=None):
    B, T, _ = q.shape
    n_cache = 0 if cache is None else cache[0].shape[-1]
    assert n_cache % KEY_CHUNK == 0 and (cache is None or hb == 1)
    hd = 2 * DH_A
    nq = T // tq
    n_groups = H_A // hb
    n_blocks = B * n_groups * nq

    def block_of(t):
        blk = jnp.clip(t, 0, n_blocks - 1)
        return blk // (n_groups * nq), (blk // nq) % n_groups, blk % nq

    def q_map(t):
        b, h, i = block_of(t)
        return b, i, h

    def kv_map(t):
        b, h, _ = block_of(t)
        return b, 0, h

    def out_map(t):
        return q_map(t - 1)

    in_specs = [
        pl.BlockSpec((None, tq, hb * hd), q_map),
        pl.BlockSpec((None, T, hb * hd), kv_map),
        pl.BlockSpec((None, T, hb * hd), lambda t: kv_map(t - 1)),
    ]
    args = [q, k, v]
    if cache is not None:
        ck, cv, layer = cache

        def ck_map(t):
            b, h, _ = block_of(t)
            return b, layer, h, 0, 0

        def cv_map(t):
            b, _, _ = block_of(t - 1)
            return b, layer, 0, 0

        in_specs += [pl.BlockSpec((None, None, None, hd, n_cache), ck_map),
                     pl.BlockSpec((None, None, n_cache * H_A, hd), cv_map)]
        args += [ck, cv]
    in_specs += [_resident(lam.shape), _resident(subln_g.shape)]
    args += [lam, subln_g]
    n_keys = n_cache + T
    return pl.pallas_call(
        functools.partial(_diff_attn_kernel, n_cache=n_cache, lam_init=lam_init, nq=nq, hb=hb),
        grid=(n_blocks + 1,),
        in_specs=in_specs,
        out_specs=pl.BlockSpec((None, tq, hb * hd), out_map),
        out_shape=jax.ShapeDtypeStruct((B, T, N_QA), BF16),
        scratch_shapes=[pltpu.VMEM((hb, 2 * tq, n_keys), F32), pltpu.VMEM((hb, 2 * tq, n_keys), F32),
                        pltpu.VMEM((hb, 2 * tq, LANES), F32), pltpu.VMEM((hb, 2 * tq, LANES), F32)]
        + ([pltpu.VMEM((n_cache, hd), BF16)] if n_cache else []),
        compiler_params=_params("arbitrary"),
        name="diff_attn",
    )(*args)


def _win_attn_kernel(*refs, local, has_ctx, seq_len, tq, nq, nb):
    if has_ctx:
        q_ref, k_ref, v_ref, ck_ref, cv_ref, sink_ref, o_ref = refs
        ck = ck_ref[...].astype(BF16)
        cv = cv_ref[...].astype(BF16)
        cv_ones = jnp.concatenate([cv, jnp.ones(cv.shape, BF16)], axis=0)
    else:
        q_ref, k_ref, v_ref, sink_ref, o_ref = refs
    n_loc = 3 * WIN_BLOCK
    lane = lax.broadcasted_iota(jnp.int32, (tq, LANES), 1)
    zero = jnp.zeros((tq, LANES), BF16)
    blocks = {}

    def block(bi, qi):
        if (bi, qi) in blocks:
            return blocks[bi, qi]
        blk = {"rows": slice(qi * tq, (qi + 1) * tq)}
        blk["q"] = q_ref[bi, blk["rows"], :]
        if local:
            n = pl.program_id(1) * nq + qi
            start = pl.multiple_of(jnp.clip((n - 1) * WIN_BLOCK, 0, seq_len - n_loc), WIN_BLOCK)
            blk["k"] = k_ref[bi, pl.ds(start, n_loc), :].astype(BF16)
            v = v_ref[bi, pl.ds(start, n_loc), :].astype(BF16)
            rows = lax.broadcasted_iota(jnp.int32, (tq, n_loc), 0)
            cols = lax.broadcasted_iota(jnp.int32, (tq, n_loc), 1)
            blk["valid"] = jnp.abs(cols - rows - (n * WIN_BLOCK - start)) <= WINDOW
        else:
            blk["k"] = k_ref[bi].astype(BF16)
            v = v_ref[bi].astype(BF16)
        blk["v_ones"] = jnp.concatenate([v, jnp.ones(v.shape, BF16)], axis=1)
        blocks[bi, qi] = blk
        return blk

    def scores(bi, qi, j):
        blk = block(bi, qi)
        sel = (lane >= j * DH_B) & (lane < (j + 1) * DH_B)
        qj = jnp.concatenate(
            [jnp.where(sel, blk["q"][:, g * LANES:(g + 1) * LANES], zero) for g in range(G_B)], axis=0)
        s = _dot_nt(qj, blk["k"])
        if has_ctx:
            s = jnp.concatenate([s, _dot(qj, ck)], axis=1)
        return s

    def softmax(bi, qi, j, s):
        es, sink_terms = [], []
        for g in range(G_B):
            sg = s[g * tq:(g + 1) * tq]
            if local:
                band = jnp.where(block(bi, qi)["valid"], sg[:, :n_loc], -jnp.inf)
                sg = jnp.concatenate([band, sg[:, n_loc:]], axis=1) if has_ctx else band
            sink = sink_ref[j * G_B + g] * LOG2E
            m = jnp.maximum(jnp.max(sg, axis=1, keepdims=True), sink)
            es.append(jnp.exp2(sg - m).astype(BF16))
            sink_terms.append(jnp.exp2(sink - m))
        return jnp.concatenate(es, axis=0), sink_terms

    def values(bi, qi, e):
        blk = block(bi, qi)
        n_own = blk["k"].shape[0]
        ov = _dot(e[:, :n_own], blk["v_ones"])
        if has_ctx:
            ov = ov + _dot_nt(e[:, n_own:], cv_ones)
        return ov

    def finish(bi, qi, halves):
        for g in range(G_B):
            rows_g = slice(g * tq, (g + 1) * tq)
            outs = [ov[rows_g, :LANES] / (ov[rows_g, LANES:] + sink_terms[g]) for ov, sink_terms in halves]
            og = jnp.where(lane < DH_B, outs[0], outs[1])
            o_ref[bi, block(bi, qi)["rows"], g * LANES:(g + 1) * LANES] = og.astype(o_ref.dtype)

    units = list(itertools.product(range(nb), range(nq), range(KV_B)))
    pending_s, pending_e, done = {}, {}, {}
    lag = WIN_SKEW
    for i in range(len(units) + 2 * lag):
        if i < len(units):
            pending_s[i] = scores(*units[i])
        if 0 <= i - lag < len(units):
            pending_e[i - lag] = softmax(*units[i - lag], pending_s.pop(i - lag))
        if 0 <= i - 2 * lag < len(units):
            bi, qi, j = units[i - 2 * lag]
            e, sink_terms = pending_e.pop(i - 2 * lag)
            done[bi, qi, j] = (values(bi, qi, e), sink_terms)
            if j == KV_B - 1:
                finish(bi, qi, [done.pop((bi, qi, jj)) for jj in range(KV_B)])


def _win_attn(q, k, v, sink, *, tq, nq, nb, local, cache=None):
    B, T, _ = q.shape
    has_ctx = cache is not None
    assert not local or tq == WIN_BLOCK
    assert not has_ctx or nb == 1
    rows = nq * tq
    in_specs = [
        pl.BlockSpec((nb, rows, G_B * LANES), lambda b, n: (b, n, 0)),
        pl.BlockSpec((nb, T, N_KVB), lambda b, n: (b, 0, 0)),
        pl.BlockSpec((nb, T, N_KVB), lambda b, n: (b, 0, 0)),
    ]
    args = [q, k, v]
    if has_ctx:
        ck, cv, layer = cache
        spec = pl.BlockSpec((None, None, N_KVB, ck.shape[3]), lambda b, n: (b, layer, 0, 0))
        in_specs += [spec, spec]
        args += [ck, cv]
    in_specs.append(pl.BlockSpec(memory_space=pltpu.SMEM))
    args.append(sink)
    return pl.pallas_call(
        functools.partial(_win_attn_kernel, local=local, has_ctx=has_ctx, seq_len=T, tq=tq, nq=nq, nb=nb),
        grid=(B // nb, T // rows),
        in_specs=in_specs,
        out_specs=pl.BlockSpec((nb, rows, G_B * LANES), lambda b, n: (b, n, 0)),
        out_shape=jax.ShapeDtypeStruct((B, T, H_B * DH_B), BF16),
        compiler_params=_params("parallel", "parallel"),
        name="win_attn",
    )(*args)


def _conv_kernel(prev_ref, cur_ref, next_ref, w_ref, bdw_ref, g_ref, b_ref, o_ref, win_ref, ph_ref):
    i = pl.program_id(1)
    rt = cur_ref.shape[0]
    zeros = jnp.zeros((CONV_HALO, C_CONV), F32)
    win_ref[0:CONV_HALO, :] = jnp.where(i > 0, prev_ref[...], zeros)
    win_ref[CONV_HALO:CONV_HALO + rt, :] = cur_ref[...]
    win_ref[CONV_HALO + rt:, :] = jnp.where(i < pl.num_programs(1) - 1, next_ref[...], zeros)
    n_ph = ph_ref.shape[1]
    for p in range(1, SUBLANES):
        ph_ref[p - 1] = win_ref[p:p + n_ph, :]
    first = CONV_HALO - CONV_K // 2
    tiles = CONV_ROWS // SUBLANES
    for r0 in range(0, rt, CONV_ROWS):
        acc = jnp.zeros((tiles, SUBLANES, C_CONV), F32) + bdw_ref[...]
        for t in range(CONV_K):
            p = (first + t) % SUBLANES
            base = r0 + first + t - p
            x = win_ref[base:base + CONV_ROWS, :] if p == 0 else ph_ref[p - 1, base:base + CONV_ROWS, :]
            acc = acc + x.reshape(tiles, SUBLANES, C_CONV) * w_ref[t]
        acc = acc.reshape(CONV_ROWS, C_CONV)
        mu = jnp.mean(acc, axis=-1, keepdims=True)
        cen = acc - mu
        var = jnp.mean(cen * cen, axis=-1, keepdims=True)
        y = cen * lax.rsqrt(var + NORM_EPS) * g_ref[...] + b_ref[...]
        o_ref[r0:r0 + CONV_ROWS, :] = _silu(y).astype(o_ref.dtype)


def _conv_module(u, w_dw, b_dw, g, b, *, rt):
    B, T, C = u.shape
    hb = rt // CONV_HALO
    last = T // CONV_HALO - 1
    n_ph = rt + (CONV_HALO + CONV_K // 2) // SUBLANES * SUBLANES
    w_tiles = jnp.broadcast_to(w_dw[:, None, :], (CONV_K, SUBLANES, C))
    in_specs = [
        pl.BlockSpec((None, CONV_HALO, C), lambda bb, i: (bb, jnp.maximum(i * hb - 1, 0), 0)),
        pl.BlockSpec((None, rt, C), lambda bb, i: (bb, i, 0)),
        pl.BlockSpec((None, CONV_HALO, C), lambda bb, i: (bb, jnp.minimum((i + 1) * hb, last), 0)),
        _resident(w_tiles.shape), _resident(b_dw.shape), _resident(g.shape), _resident(b.shape),
    ]
    return pl.pallas_call(
        _conv_kernel,
        grid=(B, T // rt),
        in_specs=in_specs,
        out_specs=pl.BlockSpec((None, rt, C), lambda bb, i: (bb, i, 0)),
        out_shape=jax.ShapeDtypeStruct((B, T, C), BF16),
        scratch_shapes=[pltpu.VMEM((rt + 2 * CONV_HALO, C), F32),
                        pltpu.VMEM((SUBLANES - 1, n_ph, C), F32)],
        compiler_params=_params("parallel", "parallel"),
        name="conv_module",
    )(u, u, u, w_tiles, b_dw, g, b)


def _merge_kernel(h_ref, mod_ref, g_ref, ya_ref, yb_ref, yc_ref, wg_ref, wb_ref, wo_ref, o_ref):
    h = h_ref[...]
    nb = _mod_rms(h, g_ref[1:2, :], mod_ref[3:4, :], mod_ref[4:5, :]).astype(BF16)
    merged = None
    for i, y_ref in enumerate((ya_ref, yb_ref, yc_ref)):
        gate = jax.nn.sigmoid(_dot(nb, wg_ref[:, i * D_MODEL:(i + 1) * D_MODEL]))
        term = gate * _dot(y_ref[...], wb_ref[i])
        merged = term if merged is None else merged + term
    o_ref[...] = h + mod_ref[5:6, :] * _dot(merged.astype(BF16), wo_ref[...])


def _merge(h, mod, norm_g, w_gate, ya, yb, yc, w_branch, w_out, layer, *, tm):
    B, T, D = h.shape
    per_batch = mod.shape[0] > 1
    row = lambda b, i: (b, i, 0)
    y_spec = pl.BlockSpec((None, tm, BRANCH_W), row)
    return pl.pallas_call(
        _merge_kernel,
        grid=(B, T // tm),
        in_specs=[
            pl.BlockSpec((None, tm, D), row),
            pl.BlockSpec((None, 9, D), (lambda b, i: (b, 0, 0)) if per_batch else (lambda b, i: (0, 0, 0))),
            _resident(norm_g.shape), y_spec, y_spec, y_spec,
            _resident_slice(w_gate.shape, (layer,)), _resident_slice(w_branch.shape, (layer,)),
            _resident_slice(w_out.shape, (layer,)),
        ],
        out_specs=pl.BlockSpec((None, tm, D), row),
        out_shape=jax.ShapeDtypeStruct((B, T, D), F32),
        compiler_params=_params("parallel", "parallel"),
        name="merge",
    )(h, mod, norm_g, ya, yb, yc, w_gate, w_branch, w_out)


def _rope_tables(n_tokens):
    quarter = DH_A // 4
    t = np.arange(n_tokens)
    inv = (ROPE_BASE ** (-np.arange(0, DH_A // 2, 2, dtype=np.float32) / (DH_A // 2))).astype(np.float32)
    pos = np.stack([(t // GRID_W).astype(np.float32), (t % GRID_W).astype(np.float32)], axis=1)
    ang = (pos[:, :, None] * inv[None, None, :]).astype(np.float32)
    ang = np.broadcast_to(ang[:, :, None, :], (n_tokens, 2, 2, quarter)).reshape(n_tokens, DH_A)
    sign = np.where((np.arange(DH_A) % (2 * quarter)) < quarter, -1.0, 1.0).astype(np.float32)
    cos = np.cos(ang).astype(np.float32)
    sin_signed = (np.sin(ang) * sign).astype(np.float32)
    reps = LANES // DH_A
    return jnp.asarray(np.tile(cos, (1, reps))), jnp.asarray(np.tile(sin_signed, (1, reps)))


def _group_major(w, axis):
    shape = w.shape
    split = shape[:axis] + (KV_B, G_B, DH_B) + shape[axis + 1:]
    return jnp.swapaxes(w.reshape(split), axis, axis + 1).reshape(shape)


def _prepare_weights(ffn_w_in, ffn_w_out, w_in, w_branch, w_out):
    qb_lo = 3 * N_QA
    qb_hi = qb_lo + H_B * DH_B
    w_proj = jnp.concatenate(
        [w_in[:, :, :qb_lo], _group_major(w_in[:, :, qb_lo:qb_hi], 2), w_in[:, :, qb_hi:N_PROJ]],
        axis=2).astype(BF16)
    w_gate = w_in[:, :, N_PROJ:].astype(BF16)
    wb = jnp.concatenate(
        [w_branch[:, 0:1], _group_major(w_branch[:, 1:2], 2), w_branch[:, 2:3]], axis=1).astype(BF16)
    return ffn_w_in.astype(BF16), ffn_w_out.astype(BF16), w_proj, w_gate, wb, w_out.astype(BF16)


def kernel(x_prompt, x_sample, cache_diff_k, cache_diff_v, cache_win_k, cache_win_v, c, c_ctx,
           w_mod, b_mod, norm_g, ffn_w_in, ffn_w_out, w_in, diff_lambda, diff_subln_g, win_sink,
           conv_dw_w, conv_dw_b, conv_norm_g, conv_norm_b, w_branch, w_out, final_norm_g):
    n_ctx_b, n_ctx_t, D = x_prompt.shape
    n_lat_b, n_lat_t, _ = x_sample.shape
    past = cache_diff_k.shape[2]
    tm = 512

    n_rows = 1 + n_lat_b
    pad_rows = -n_rows % 8
    cond = jnp.concatenate([c_ctx[None, :], c, jnp.zeros((pad_rows, D), F32)], axis=0)
    mod_all = _modulation(cond, w_mod, b_mod.reshape(DEPTH, 1, 9 * D))

    rope_tables = _rope_tables(n_lat_t)
    ck_a = jnp.transpose(cache_diff_k, (0, 1, 3, 4, 5, 2)).reshape(n_lat_b, DEPTH, H_A, 2 * DH_A, past)
    cv_a = cache_diff_v.reshape(n_lat_b, DEPTH, past * H_A, 2 * DH_A)
    ck_b = jnp.transpose(cache_win_k, (0, 1, 3, 4, 2)).reshape(n_lat_b, DEPTH, N_KVB, past)
    cv_b = jnp.transpose(cache_win_v, (0, 1, 3, 4, 2)).reshape(n_lat_b, DEPTH, N_KVB, past)
    final_g = final_norm_g.reshape(1, D)

    xp = x_prompt.reshape(1, n_ctx_b * n_ctx_t, D)
    xs = x_sample
    new_cache = None
    ffn_in, ffn_out, w_proj, w_gate, wb, wo = _prepare_weights(ffn_w_in, ffn_w_out, w_in, w_branch, w_out)
    for l in range(DEPTH):
        lam_init = 0.8 - 0.6 * math.exp(-0.3 * l)
        g_l = norm_g[l]
        lam = diff_lambda[l]
        subln = diff_subln_g[l].reshape(1, 2 * DH_A)
        sink = win_sink[l]
        conv_p = (conv_dw_w[l], conv_dw_b[l].reshape(1, C_CONV),
                  conv_norm_g[l].reshape(1, C_CONV), conv_norm_b[l].reshape(1, C_CONV))
        mod_ctx = mod_all[l, 0:1].reshape(1, 9, D)
        mod_lat = mod_all[l, 1:n_rows].reshape(n_lat_b, 9, D)
        last = l == DEPTH - 1

        h = _ffn(xp, mod_ctx, g_l, ffn_in, ffn_out, (l, 0), mod_row=0, g_row=0, tm=tm)
        qa, ka, va, qb, kb, vb, u, *new_cache = _inproj(
            h, mod_ctx, g_l, w_proj, l, tm=tm, kv_dtype=F32,
            cache_seq=(n_ctx_b, n_ctx_t), cache_acc=new_cache)
        seq = lambda a: a.reshape(n_ctx_b, n_ctx_t, a.shape[-1])
        ka, va, kb, vb = seq(ka), seq(va), seq(kb), seq(vb)
        ya = _diff_attn(seq(qa), ka, va, lam, subln, lam_init=lam_init, tq=n_ctx_t, hb=H_A)
        yb = _win_attn(seq(qb), kb, vb, sink, tq=n_ctx_t, nq=1, nb=4, local=False)
        yc = _conv_module(seq(u), *conv_p, rt=n_ctx_t)
        flat = lambda a: a.reshape(1, n_ctx_b * n_ctx_t, a.shape[-1])
        h = _merge(h, mod_ctx, g_l, w_gate, flat(ya), flat(yb), flat(yc), wb, wo, l, tm=tm)
        xp = _ffn(h, mod_ctx, g_l, ffn_in, ffn_out, (l, 1), mod_row=6, g_row=2, tm=tm,
                  final_g=final_g if last else None)

        h = _ffn(xs, mod_lat, g_l, ffn_in, ffn_out, (l, 0), mod_row=0, g_row=0, tm=tm)
        qa, ka, va, qb, kb, vb, u = _inproj(h, mod_lat, g_l, w_proj, l, tm=tm, kv_dtype=BF16,
                                            rope_tables=rope_tables)
        ya = _diff_attn(qa, ka, va, lam, subln, lam_init=lam_init, tq=512, hb=1, cache=(ck_a, cv_a, l))
        yb = _win_attn(qb, kb, vb, sink, tq=WIN_BLOCK, nq=4, nb=1, local=True, cache=(ck_b, cv_b, l))
        yc = _conv_module(u, *conv_p, rt=256)
        h = _merge(h, mod_lat, g_l, w_gate, ya, yb, yc, wb, wo, l, tm=tm)
        xs = _ffn(h, mod_lat, g_l, ffn_in, ffn_out, (l, 1), mod_row=6, g_row=2, tm=tm,
                  final_g=final_g if last else None)

    y_prompt = xp.reshape(n_ctx_b, n_ctx_t, D)
    dk, dv, wk, wv = new_cache
    new_diff_k = jnp.transpose(dk.reshape(n_ctx_b, DEPTH, H_A, 2, DH_A, n_ctx_t), (0, 1, 5, 2, 3, 4))
    new_diff_v = dv.reshape(n_ctx_b, DEPTH, n_ctx_t, H_A, 2 * DH_A)
    new_win_k = jnp.transpose(wk.reshape(n_ctx_b, DEPTH, KV_B, DH_B, n_ctx_t), (0, 1, 4, 2, 3))
    new_win_v = jnp.transpose(wv.reshape(n_ctx_b, DEPTH, KV_B, DH_B, n_ctx_t), (0, 1, 4, 2, 3))
    return (y_prompt, xs, new_diff_k, new_diff_v, new_win_k, new_win_v)
```

```python
import functools
import itertools
import math

import numpy as np
import jax
import jax.numpy as jnp
from jax import lax
from jax.experimental import pallas as pl
from jax.experimental.pallas import tpu as pltpu

D_MODEL = 1024
DEPTH = 2
GRID_W = 64
H_A = 4
DH_A = 64
H_B = 8
KV_B = 2
G_B = H_B // KV_B
DH_B = 64
WINDOW = 128
WIN_BLOCK = 128
C_CONV = 512
CONV_K = 31
BRANCH_W = 512
D_FF = 2816
ROPE_BASE = 10000.0
NORM_EPS = 1e-6
SUBLN_EPS = 1e-5
SCALE_A = DH_A ** -0.5
SCALE_B = DH_B ** -0.5
LOG2E = math.log2(math.e)

N_QA = H_A * 2 * DH_A
N_KVB = KV_B * DH_B
N_ATTN = 3 * N_QA + H_B * DH_B + 2 * N_KVB
N_PROJ = N_ATTN + 2 * C_CONV

LANES = 128
SUBLANES = 8
ROPE_HALF = DH_A // 4
FF_CHUNK = 256
KEY_CHUNK = 256
WIN_SKEW = 1
CONV_HALO = 16
CONV_ROWS = 32
VMEM_LIMIT = 56 * 1024 * 1024

BF16 = jnp.bfloat16
F32 = jnp.float32


def _params(*sem):
    return pltpu.CompilerParams(dimension_semantics=sem, vmem_limit_bytes=VMEM_LIMIT)


def _resident(shape):
    nd = len(shape)
    return pl.BlockSpec(shape, lambda *_: (0,) * nd, pipeline_mode=pl.Buffered(1))


def _resident_slice(shape, lead):
    n = len(lead)
    block = (None,) * n + tuple(shape[n:])
    index = tuple(lead) + (0,) * (len(shape) - n)
    return pl.BlockSpec(block, lambda *_: index, pipeline_mode=pl.Buffered(1))


def _dot(a, b):
    return jnp.dot(a, b, preferred_element_type=F32)


def _dot_nt(a, b):
    return lax.dot_general(a, b, (((1,), (1,)), ((), ())), preferred_element_type=F32)


def _rms(x, eps):
    return x * lax.rsqrt(jnp.mean(x * x, axis=-1, keepdims=True) + eps)


def _mod_rms(x, g, shift, scale):
    return _rms(x, NORM_EPS) * g * (1.0 + scale) + shift


def _silu(x):
    return x * jax.nn.sigmoid(x)


def _mod_kernel(c_ref, w_ref, b_ref, o_ref):
    a = _silu(c_ref[...]).astype(BF16)
    o_ref[...] = _dot(a, w_ref[...].astype(BF16)) + b_ref[...]


def _modulation(cond, w_mod, b_mod):
    R = cond.shape[0]
    tn = 9 * D_MODEL // 4
    return pl.pallas_call(
        _mod_kernel,
        grid=(DEPTH, 9 * D_MODEL // tn),
        in_specs=[
            pl.BlockSpec((R, D_MODEL), lambda l, j: (0, 0)),
            pl.BlockSpec((None, D_MODEL, tn), lambda l, j: (l, 0, j)),
            pl.BlockSpec((None, 1, tn), lambda l, j: (l, 0, j)),
        ],
        out_specs=pl.BlockSpec((None, R, tn), lambda l, j: (l, 0, j)),
        out_shape=jax.ShapeDtypeStruct((DEPTH, R, 9 * D_MODEL), F32),
        compiler_params=_params("parallel", "parallel"),
        name="modulation",
    )(cond, w_mod, b_mod)


def _ffn_kernel(x_ref, mod_ref, g_ref, win_ref, wout_ref, *rest, mod_row, g_row, final):
    if final:
        fg_ref, o_ref = rest
    else:
        (o_ref,) = rest
    x = x_ref[...]
    shift = mod_ref[mod_row:mod_row + 1, :]
    scale = mod_ref[mod_row + 1:mod_row + 2, :]
    gate = mod_ref[mod_row + 2:mod_row + 3, :]
    nb = _mod_rms(x, g_ref[g_row:g_row + 1, :], shift, scale).astype(BF16)
    acc = None
    for lo in range(0, D_FF, FF_CHUNK):
        a = _dot(nb, win_ref[:, lo:lo + FF_CHUNK])
        b = _dot(nb, win_ref[:, D_FF + lo:D_FF + lo + FF_CHUNK])
        hid = (_silu(a) * b).astype(BF16)
        part = _dot(hid, wout_ref[lo:lo + FF_CHUNK, :])
        acc = part if acc is None else acc + part
    h = x + 0.5 * gate * acc
    if final:
        h = _rms(h, NORM_EPS) * fg_ref[...]
    o_ref[...] = h


def _ffn(x, mod, norm_g, w_in, w_out, which, *, mod_row, g_row, tm, final_g=None):
    B, T, D = x.shape
    per_batch = mod.shape[0] > 1
    final = final_g is not None
    l, s = which
    in_specs = [
        pl.BlockSpec((None, tm, D), lambda b, i: (b, i, 0)),
        pl.BlockSpec((None, 9, D), (lambda b, i: (b, 0, 0)) if per_batch else (lambda b, i: (0, 0, 0))),
        _resident(norm_g.shape),
    ]
    in_specs += [_resident_slice(w_in.shape, (l, s)), _resident_slice(w_out.shape, (l, s))]
    args = [x, mod, norm_g, w_in, w_out]
    if final:
        in_specs.append(_resident(final_g.shape))
        args.append(final_g)
    return pl.pallas_call(
        functools.partial(_ffn_kernel, mod_row=mod_row, g_row=g_row, final=final),
        grid=(B, T // tm),
        in_specs=in_specs,
        out_specs=pl.BlockSpec((None, tm, D), lambda b, i: (b, i, 0)),
        out_shape=jax.ShapeDtypeStruct((B, T, D), F32),
        compiler_params=_params("parallel", "parallel"),
        name="ffn",
    )(*args)


def _rope(x, cos, sin_signed, lo_half):
    outs = []
    for j in range(x.shape[1] // LANES):
        xj = x[:, j * LANES:(j + 1) * LANES]
        rot = jnp.where(lo_half, pltpu.roll(xj, LANES - ROPE_HALF, 1), pltpu.roll(xj, ROPE_HALF, 1))
        outs.append(xj * cos + rot * sin_signed)
    return outs[0] if len(outs) == 1 else jnp.concatenate(outs, axis=1)


def _inproj_kernel(x_ref, mod_ref, g_ref, w_ref, *rest, rope, n_aliased, cache_out, cache_layer):
    if rope:
        cos_ref, sin_ref = rest[:2]
        rest = rest[2:]
    rest = rest[n_aliased:]
    qa_ref, ka_ref, va_ref, qb_ref, kb_ref, vb_ref, u_ref = rest[:7]
    x = x_ref[...]
    nb = _mod_rms(x, g_ref[1:2, :], mod_ref[3:4, :], mod_ref[4:5, :]).astype(BF16)

    def proj(lo, hi):
        return _dot(nb, w_ref[:, lo:hi])

    qa = proj(0, N_QA)
    ka = proj(N_QA, 2 * N_QA)
    va = proj(2 * N_QA, 3 * N_QA)
    qb = proj(3 * N_QA, 4 * N_QA)
    kvb = proj(4 * N_QA, N_ATTN)
    kb = kvb[:, :N_KVB]
    vb = kvb[:, N_KVB:]
    cu = proj(N_ATTN, N_PROJ)
    if rope:
        cos = cos_ref[...]
        sin_signed = sin_ref[...]
        lane = lax.broadcasted_iota(jnp.int32, cos.shape, 1)
        lo_half = (lane % (2 * ROPE_HALF)) < ROPE_HALF
        qa = _rope(qa, cos, sin_signed, lo_half)
        ka = _rope(ka, cos, sin_signed, lo_half)
        qb = _rope(qb, cos, sin_signed, lo_half)
        kb = _rope(kb, cos, sin_signed, lo_half)
    qa_ref[...] = (qa * (SCALE_A * LOG2E)).astype(qa_ref.dtype)
    ka_ref[...] = ka.astype(ka_ref.dtype)
    va_ref[...] = va.astype(va_ref.dtype)
    qb_ref[...] = (qb * (SCALE_B * LOG2E)).astype(qb_ref.dtype)
    kb_ref[...] = kb.astype(kb_ref.dtype)
    vb_ref[...] = vb.astype(vb_ref.dtype)
    u_ref[...] = cu[:, :C_CONV] * jax.nn.sigmoid(cu[:, C_CONV:])
    if cache_out:
        cache_refs = rest[7:]
        if n_aliased == 0:
            for ref in cache_refs:
                for other in range(ref.shape[1]):
                    if other != cache_layer:
                        ref[:, other] = jnp.zeros(ref.shape[:1] + ref.shape[2:], F32)
            cache_refs = [ref.at[:, cache_layer] for ref in cache_refs]
        dk_ref, dv_ref, wk_ref, wv_ref = cache_refs
        n_seq, _, _, seq_len = dk_ref.shape
        ka_t, kb_t, vb_t = ka.T, kb.T, vb.T
        for s in range(n_seq):
            tok = slice(s * seq_len, (s + 1) * seq_len)
            for hc in range(2 * H_A):
                dk_ref[s, hc] = ka_t[hc * DH_A:(hc + 1) * DH_A, tok]
            for hh in range(H_A):
                dv_ref[s, pl.ds(hh, seq_len, stride=H_A), :] = va[tok, hh * 2 * DH_A:(hh + 1) * 2 * DH_A]
            wk_ref[s] = kb_t[:, tok]
            wv_ref[s] = vb_t[:, tok]


def _inproj(h, mod, norm_g, w_proj, layer, *, tm, kv_dtype, rope_tables=None, cache_seq=None, cache_acc=None):
    B, T, D = h.shape
    per_batch = mod.shape[0] > 1
    rope = rope_tables is not None
    row = lambda b, i: (b, i, 0)
    in_specs = [
        pl.BlockSpec((None, tm, D), row),
        pl.BlockSpec((None, 9, D), (lambda b, i: (b, 0, 0)) if per_batch else (lambda b, i: (0, 0, 0))),
        _resident(norm_g.shape),
        _resident_slice(w_proj.shape, (layer,)),
    ]
    args = [h, mod, norm_g, w_proj]
    if rope:
        in_specs += [pl.BlockSpec((tm, LANES), lambda b, i: (i, 0))] * 2
        args += list(rope_tables)
    widths = (N_QA, N_QA, N_QA, H_B * DH_B, N_KVB, N_KVB, C_CONV)
    dtypes = (BF16, kv_dtype, kv_dtype, BF16, kv_dtype, kv_dtype, F32)
    out_specs = [pl.BlockSpec((None, tm, w), row) for w in widths]
    out_shape = [jax.ShapeDtypeStruct((B, T, w), dt) for w, dt in zip(widths, dtypes)]
    aliases = {}
    n_aliased = 0
    if cache_seq is not None:
        n_all, seq_len = cache_seq
        n_seq = tm // seq_len
        assert B == 1 and n_seq * seq_len == tm
        cache_shapes = ((2 * H_A, DH_A, seq_len), (seq_len * H_A, 2 * DH_A), (N_KVB, seq_len), (N_KVB, seq_len))
        for shape in cache_shapes:
            nd = len(shape)
            if cache_acc is None:
                spec = pl.BlockSpec((n_seq, DEPTH) + shape, lambda b, i, nd=nd: (i, 0) + (0,) * nd)
            else:
                spec = pl.BlockSpec((n_seq, None) + shape, lambda b, i, nd=nd: (i, layer) + (0,) * nd)
            out_specs.append(spec)
            out_shape.append(jax.ShapeDtypeStruct((n_all, DEPTH) + shape, F32))
        if cache_acc is not None:
            n_aliased = len(cache_acc)
            aliases = {len(args) + j: len(widths) + j for j in range(n_aliased)}
            in_specs += [pl.BlockSpec(memory_space=pl.ANY)] * n_aliased
            args += list(cache_acc)
    return pl.pallas_call(
        functools.partial(_inproj_kernel, rope=rope, n_aliased=n_aliased, cache_out=cache_seq is not None,
                          cache_layer=layer),
        grid=(B, T // tm),
        in_specs=in_specs,
        out_specs=out_specs,
        out_shape=out_shape,
        input_output_aliases=aliases,
        compiler_params=_params("parallel", "parallel"),
        name="inproj",
    )(*args)


def _diff_attn_kernel(*refs, n_cache, lam_init, nq, hb):
    if n_cache:
        q_ref, k_ref, v_ref, ck_ref, cv_ref, lam_ref, g_ref, o_ref, s_even, s_odd, m_even, m_odd, vcache = refs
    else:
        q_ref, k_ref, v_ref, lam_ref, g_ref, o_ref, s_even, s_odd, m_even, m_odd = refs
    hd = 2 * DH_A
    t = pl.program_id(0)
    heads = [(hh, slice(hh * hd, (hh + 1) * hd)) for hh in range(hb)]

    @pl.when(t == 0)
    def _():
        s_odd[...] = jnp.zeros(s_odd.shape, F32)
        m_odd[...] = jnp.zeros(m_odd.shape, F32)
        if n_cache:
            vcache[...] = jnp.zeros(vcache.shape, BF16)

    if n_cache:
        @pl.when((t > 0) & ((t - 1) % nq == 0))
        def _():
            head = ((t - 1) // nq) % H_A
            vcache[...] = cv_ref[pl.ds(head, n_cache, stride=H_A), :].astype(BF16)

    def step(s_write, m_write, s_read, m_read):
        lam = lam_ref[...]
        lam_full = (jnp.exp(jnp.sum(lam[0:1] * lam[1:2], axis=1, keepdims=True))
                    - jnp.exp(jnp.sum(lam[2:3] * lam[3:4], axis=1, keepdims=True)) + lam_init)
        tq = q_ref.shape[0]
        lane = lax.broadcasted_iota(jnp.int32, (tq, hd), 1)
        zero = jnp.zeros((tq, hd), q_ref.dtype)
        for hh, lanes in heads:
            q = q_ref[:, lanes]
            q2 = jnp.concatenate([jnp.where(lane < DH_A, q, zero), jnp.where(lane >= DH_A, q, zero)], axis=0)
            row_max = jnp.concatenate([m_read[hh]] * (KEY_CHUNK // LANES), axis=1)
            acc = None
            run_max = None
            for lo in range(0, s_read.shape[2], KEY_CHUNK):
                cols = slice(lo, lo + KEY_CHUNK)
                if lo < n_cache:
                    s = _dot(q2, ck_ref[:, cols].astype(BF16))
                    v = vcache[cols, :]
                else:
                    own = slice(lo - n_cache, lo - n_cache + KEY_CHUNK)
                    s = _dot_nt(q2, k_ref[own, lanes].astype(BF16))
                    v = v_ref[own, lanes].astype(BF16)
                s_write[hh, :, cols] = s
                run_max = s if run_max is None else jnp.maximum(run_max, s)
                e = jnp.exp2(s_read[hh, :, cols] - row_max).astype(BF16)
                part = _dot(e, jnp.concatenate([v, jnp.ones(v.shape, BF16)], axis=1))
                acc = part if acc is None else acc + part
            m_write[hh] = jnp.broadcast_to(jnp.max(run_max, axis=1, keepdims=True), m_write.shape[1:])
            ratio = acc[:, :hd] / acc[:, hd:]
            o = ratio[:tq] - lam_full * ratio[tq:]
            o = _rms(o, SUBLN_EPS) * g_ref[...] * (1.0 - lam_init)
            o_ref[:, lanes] = o.astype(o_ref.dtype)

    pl.when(t % 2 == 0)(lambda: step(s_even, m_even, s_odd, m_odd))
    pl.when(t % 2 == 1)(lambda: step(s_odd, m_odd, s_even, m_even))


def _diff_attn(q, k, v, lam, subln_g, *, lam_init, tq, hb, cache=None):
    B, T, _ = q.shape
    n_cache = 0 if cache is None else cache[0].shape[-1]
    assert n_cache % KEY_CHUNK == 0 and (cache is None or hb == 1)
    hd = 2 * DH_A
    nq = T // tq
    n_groups = H_A // hb
    n_blocks = B * n_groups * nq

    def block_of(t):
        blk = jnp.clip(t, 0, n_blocks - 1)
        return blk // (n_groups * nq), (blk // nq) % n_groups, blk % nq

    def q_map(t):
        b, h, i = block_of(t)
        return b, i, h

    def kv_map(t):
        b, h, _ = block_of(t)
        return b, 0, h

    def out_map(t):
        return q_map(t - 1)

    in_specs = [
        pl.BlockSpec((None, tq, hb * hd), q_map),
        pl.BlockSpec((None, T, hb * hd), kv_map),
        pl.BlockSpec((None, T, hb * hd), lambda t: kv_map(t - 1)),
    ]
    args = [q, k, v]
    if cache is not None:
        ck, cv, layer = cache

        def ck_map(t):
            b, h, _ = block_of(t)
            return b, layer, h, 0, 0

        def cv_map(t):
            b, _, _ = block_of(t - 1)
            return b, layer, 0, 0

        in_specs += [pl.BlockSpec((None, None, None, hd, n_cache), ck_map),
                     pl.BlockSpec((None, None, n_cache * H_A, hd), cv_map)]
        args += [ck, cv]
    in_specs += [_resident(lam.shape), _resident(subln_g.shape)]
    args += [lam, subln_g]
    n_keys = n_cache + T
    return pl.pallas_call(
        functools.partial(_diff_attn_kernel, n_cache=n_cache, lam_init=lam_init, nq=nq, hb=hb),
        grid=(n_blocks + 1,),
        in_specs=in_specs,
        out_specs=pl.BlockSpec((None, tq, hb * hd), out_map),
        out_shape=jax.ShapeDtypeStruct((B, T, N_QA), BF16),
        scratch_shapes=[pltpu.VMEM((hb, 2 * tq, n_keys), F32), pltpu.VMEM((hb, 2 * tq, n_keys), F32),
                        pltpu.VMEM((hb, 2 * tq, LANES), F32), pltpu.VMEM((hb, 2 * tq, LANES), F32)]
        + ([pltpu.VMEM((n_cache, hd), BF16)] if n_cache else []),
        compiler_params=_params("arbitrary"),
        name="diff_attn",
    )(*args)


def _win_attn_kernel(*refs, local, has_ctx, seq_len, tq, nq, nb):
    if has_ctx:
        q_ref, k_ref, v_ref, ck_ref, cv_ref, sink_ref, o_ref = refs
        ck = ck_ref[...].astype(BF16)
        cv = cv_ref[...].astype(BF16)
        cv_ones = jnp.concatenate([cv, jnp.ones(cv.shape, BF16)], axis=0)
    else:
        q_ref, k_ref, v_ref, sink_ref, o_ref = refs
    n_loc = 3 * WIN_BLOCK
    lane = lax.broadcasted_iota(jnp.int32, (tq, LANES), 1)
    zero = jnp.zeros((tq, LANES), BF16)
    blocks = {}

    def block(bi, qi):
        if (bi, qi) in blocks:
            return blocks[bi, qi]
        blk = {"rows": slice(qi * tq, (qi + 1) * tq)}
        blk["q"] = q_ref[bi, blk["rows"], :]
        if local:
            n = pl.program_id(1) * nq + qi
            start = pl.multiple_of(jnp.clip((n - 1) * WIN_BLOCK, 0, seq_len - n_loc), WIN_BLOCK)
            blk["k"] = k_ref[bi, pl.ds(start, n_loc), :].astype(BF16)
            v = v_ref[bi, pl.ds(start, n_loc), :].astype(BF16)
            rows = lax.broadcasted_iota(jnp.int32, (tq, n_loc), 0)
            cols = lax.broadcasted_iota(jnp.int32, (tq, n_loc), 1)
            blk["valid"] = jnp.abs(cols - rows - (n * WIN_BLOCK - start)) <= WINDOW
        else:
            blk["k"] = k_ref[bi].astype(BF16)
            v = v_ref[bi].astype(BF16)
        blk["v_ones"] = jnp.concatenate([v, jnp.ones(v.shape, BF16)], axis=1)
        blocks[bi, qi] = blk
        return blk

    def scores(bi, qi, j):
        blk = block(bi, qi)
        sel = (lane >= j * DH_B) & (lane < (j + 1) * DH_B)
        qj = jnp.concatenate(
            [jnp.where(sel, blk["q"][:, g * LANES:(g + 1) * LANES], zero) for g in range(G_B)], axis=0)
        s = _dot_nt(qj, blk["k"])
        if has_ctx:
            s = jnp.concatenate([s, _dot(qj, ck)], axis=1)
        return s

    def softmax(bi, qi, j, s):
        es, sink_terms = [], []
        for g in range(G_B):
            sg = s[g * tq:(g + 1) * tq]
            if local:
                band = jnp.where(block(bi, qi)["valid"], sg[:, :n_loc], -jnp.inf)
                sg = jnp.concatenate([band, sg[:, n_loc:]], axis=1) if has_ctx else band
            sink = sink_ref[j * G_B + g] * LOG2E
            m = jnp.maximum(jnp.max(sg, axis=1, keepdims=True), sink)
            es.append(jnp.exp2(sg - m).astype(BF16))
            sink_terms.append(jnp.exp2(sink - m))
        return jnp.concatenate(es, axis=0), sink_terms

    def values(bi, qi, e):
        blk = block(bi, qi)
        n_own = blk["k"].shape[0]
        ov = _dot(e[:, :n_own], blk["v_ones"])
        if has_ctx:
            ov = ov + _dot_nt(e[:, n_own:], cv_ones)
        return ov

    def finish(bi, qi, halves):
        for g in range(G_B):
            rows_g = slice(g * tq, (g + 1) * tq)
            outs = [ov[rows_g, :LANES] / (ov[rows_g, LANES:] + sink_terms[g]) for ov, sink_terms in halves]
            og = jnp.where(lane < DH_B, outs[0], outs[1])
            o_ref[bi, block(bi, qi)["rows"], g * LANES:(g + 1) * LANES] = og.astype(o_ref.dtype)

    units = list(itertools.product(range(nb), range(nq), range(KV_B)))
    pending_s, pending_e, done = {}, {}, {}
    lag = WIN_SKEW
    for i in range(len(units) + 2 * lag):
        if i < len(units):
            pending_s[i] = scores(*units[i])
        if 0 <= i - lag < len(units):
            pending_e[i - lag] = softmax(*units[i - lag], pending_s.pop(i - lag))
        if 0 <= i - 2 * lag < len(units):
            bi, qi, j = units[i - 2 * lag]
            e, sink_terms = pending_e.pop(i - 2 * lag)
            done[bi, qi, j] = (values(bi, qi, e), sink_terms)
            if j == KV_B - 1:
                finish(bi, qi, [done.pop((bi, qi, jj)) for jj in range(KV_B)])


def _win_attn(q, k, v, sink, *, tq, nq, nb, local, cache=None):
    B, T, _ = q.shape
    has_ctx = cache is not None
    assert not local or tq == WIN_BLOCK
    assert not has_ctx or nb == 1
    rows = nq * tq
    in_specs = [
        pl.BlockSpec((nb, rows, G_B * LANES), lambda b, n: (b, n, 0)),
        pl.BlockSpec((nb, T, N_KVB), lambda b, n: (b, 0, 0)),
        pl.BlockSpec((nb, T, N_KVB), lambda b, n: (b, 0, 0)),
    ]
    args = [q, k, v]
    if has_ctx:
        ck, cv, layer = cache
        spec = pl.BlockSpec((None, None, N_KVB, ck.shape[3]), lambda b, n: (b, layer, 0, 0))
        in_specs += [spec, spec]
        args += [ck, cv]
    in_specs.append(pl.BlockSpec(memory_space=pltpu.SMEM))
    args.append(sink)
    return pl.pallas_call(
        functools.partial(_win_attn_kernel, local=local, has_ctx=has_ctx, seq_len=T, tq=tq, nq=nq, nb=nb),
        grid=(B // nb, T // rows),
        in_specs=in_specs,
        out_specs=pl.BlockSpec((nb, rows, G_B * LANES), lambda b, n: (b, n, 0)),
        out_shape=jax.ShapeDtypeStruct((B, T, H_B * DH_B), BF16),
        compiler_params=_params("parallel", "parallel"),
        name="win_attn",
    )(*args)


def _conv_kernel(prev_ref, cur_ref, next_ref, w_ref, bdw_ref, g_ref, b_ref, o_ref, win_ref, ph_ref):
    i = pl.program_id(1)
    rt = cur_ref.shape[0]
    zeros = jnp.zeros((CONV_HALO, C_CONV), F32)
    win_ref[0:CONV_HALO, :] = jnp.where(i > 0, prev_ref[...], zeros)
    win_ref[CONV_HALO:CONV_HALO + rt, :] = cur_ref[...]
    win_ref[CONV_HALO + rt:, :] = jnp.where(i < pl.num_programs(1) - 1, next_ref[...], zeros)
    n_ph = ph_ref.shape[1]
    for p in range(1, SUBLANES):
        ph_ref[p - 1] = win_ref[p:p + n_ph, :]
    first = CONV_HALO - CONV_K // 2
    tiles = CONV_ROWS // SUBLANES
    for r0 in range(0, rt, CONV_ROWS):
        acc = jnp.zeros((tiles, SUBLANES, C_CONV), F32) + bdw_ref[...]
        for t in range(CONV_K):
            p = (first + t) % SUBLANES
            base = r0 + first + t - p
            x = win_ref[base:base + CONV_ROWS, :] if p == 0 else ph_ref[p - 1, base:base + CONV_ROWS, :]
            acc = acc + x.reshape(tiles, SUBLANES, C_CONV) * w_ref[t]
        acc = acc.reshape(CONV_ROWS, C_CONV)
        mu = jnp.mean(acc, axis=-1, keepdims=True)
        cen = acc - mu
        var = jnp.mean(cen * cen, axis=-1, keepdims=True)
        y = cen * lax.rsqrt(var + NORM_EPS) * g_ref[...] + b_ref[...]
        o_ref[r0:r0 + CONV_ROWS, :] = _silu(y).astype(o_ref.dtype)


def _conv_module(u, w_dw, b_dw, g, b, *, rt):
    B, T, C = u.shape
    hb = rt // CONV_HALO
    last = T // CONV_HALO - 1
    n_ph = rt + (CONV_HALO + CONV_K // 2) // SUBLANES * SUBLANES
    w_tiles = jnp.broadcast_to(w_dw[:, None, :], (CONV_K, SUBLANES, C))
    in_specs = [
        pl.BlockSpec((None, CONV_HALO, C), lambda bb, i: (bb, jnp.maximum(i * hb - 1, 0), 0)),
        pl.BlockSpec((None, rt, C), lambda bb, i: (bb, i, 0)),
        pl.BlockSpec((None, CONV_HALO, C), lambda bb, i: (bb, jnp.minimum((i + 1) * hb, last), 0)),
        _resident(w_tiles.shape), _resident(b_dw.shape), _resident(g.shape), _resident(b.shape),
    ]
    return pl.pallas_call(
        _conv_kernel,
        grid=(B, T // rt),
        in_specs=in_specs,
        out_specs=pl.BlockSpec((None, rt, C), lambda bb, i: (bb, i, 0)),
        out_shape=jax.ShapeDtypeStruct((B, T, C), BF16),
        scratch_shapes=[pltpu.VMEM((rt + 2 * CONV_HALO, C), F32),
                        pltpu.VMEM((SUBLANES - 1, n_ph, C), F32)],
        compiler_params=_params("parallel", "parallel"),
        name="conv_module",
    )(u, u, u, w_tiles, b_dw, g, b)


def _merge_kernel(h_ref, mod_ref, g_ref, ya_ref, yb_ref, yc_ref, wg_ref, wb_ref, wo_ref, o_ref):
    h = h_ref[...]
    nb = _mod_rms(h, g_ref[1:2, :], mod_ref[3:4, :], mod_ref[4:5, :]).astype(BF16)
    merged = None
    for i, y_ref in enumerate((ya_ref, yb_ref, yc_ref)):
        gate = jax.nn.sigmoid(_dot(nb, wg_ref[:, i * D_MODEL:(i + 1) * D_MODEL]))
        term = gate * _dot(y_ref[...], wb_ref[i])
        merged = term if merged is None else merged + term
    o_ref[...] = h + mod_ref[5:6, :] * _dot(merged.astype(BF16), wo_ref[...])


def _merge(h, mod, norm_g, w_gate, ya, yb, yc, w_branch, w_out, layer, *, tm):
    B, T, D = h.shape
    per_batch = mod.shape[0] > 1
    row = lambda b, i: (b, i, 0)
    y_spec = pl.BlockSpec((None, tm, BRANCH_W), row)
    return pl.pallas_call(
        _merge_kernel,
        grid=(B, T // tm),
        in_specs=[
            pl.BlockSpec((None, tm, D), row),
            pl.BlockSpec((None, 9, D), (lambda b, i: (b, 0, 0)) if per_batch else (lambda b, i: (0, 0, 0))),
            _resident(norm_g.shape), y_spec, y_spec, y_spec,
            _resident_slice(w_gate.shape, (layer,)), _resident_slice(w_branch.shape, (layer,)),
            _resident_slice(w_out.shape, (layer,)),
        ],
        out_specs=pl.BlockSpec((None, tm, D), row),
        out_shape=jax.ShapeDtypeStruct((B, T, D), F32),
        compiler_params=_params("parallel", "parallel"),
        name="merge",
    )(h, mod, norm_g, ya, yb, yc, w_gate, w_branch, w_out)


def _rope_tables(n_tokens):
    quarter = DH_A // 4
    t = np.arange(n_tokens)
    inv = (ROPE_BASE ** (-np.arange(0, DH_A // 2, 2, dtype=np.float32) / (DH_A // 2))).astype(np.float32)
    pos = np.stack([(t // GRID_W).astype(np.float32), (t % GRID_W).astype(np.float32)], axis=1)
    ang = (pos[:, :, None] * inv[None, None, :]).astype(np.float32)
    ang = np.broadcast_to(ang[:, :, None, :], (n_tokens, 2, 2, quarter)).reshape(n_tokens, DH_A)
    sign = np.where((np.arange(DH_A) % (2 * quarter)) < quarter, -1.0, 1.0).astype(np.float32)
    cos = np.cos(ang).astype(np.float32)
    sin_signed = (np.sin(ang) * sign).astype(np.float32)
    reps = LANES // DH_A
    return jnp.asarray(np.tile(cos, (1, reps))), jnp.asarray(np.tile(sin_signed, (1, reps)))


def _group_major(w, axis):
    shape = w.shape
    split = shape[:axis] + (KV_B, G_B, DH_B) + shape[axis + 1:]
    return jnp.swapaxes(w.reshape(split), axis, axis + 1).reshape(shape)


def _prepare_weights(ffn_w_in, ffn_w_out, w_in, w_branch, w_out):
    qb_lo = 3 * N_QA
    qb_hi = qb_lo + H_B * DH_B
    w_proj = jnp.concatenate(
        [w_in[:, :, :qb_lo], _group_major(w_in[:, :, qb_lo:qb_hi], 2), w_in[:, :, qb_hi:N_PROJ]],
        axis=2).astype(BF16)
    w_gate = w_in[:, :, N_PROJ:].astype(BF16)
    wb = jnp.concatenate(
        [w_branch[:, 0:1], _group_major(w_branch[:, 1:2], 2), w_branch[:, 2:3]], axis=1).astype(BF16)
    return ffn_w_in.astype(BF16), ffn_w_out.astype(BF16), w_proj, w_gate, wb, w_out.astype(BF16)


def kernel(x_prompt, x_sample, cache_diff_k, cache_diff_v, cache_win_k, cache_win_v, c, c_ctx,
           w_mod, b_mod, norm_g, ffn_w_in, ffn_w_out, w_in, diff_lambda, diff_subln_g, win_sink,
           conv_dw_w, conv_dw_b, conv_norm_g, conv_norm_b, w_branch, w_out, final_norm_g):
    n_ctx_b, n_ctx_t, D = x_prompt.shape
    n_lat_b, n_lat_t, _ = x_sample.shape
    past = cache_diff_k.shape[2]
    tm = 512

    n_rows = 1 + n_lat_b
    pad_rows = -n_rows % 8
    cond = jnp.concatenate([c_ctx[None, :], c, jnp.zeros((pad_rows, D), F32)], axis=0)
    mod_all = _modulation(cond, w_mod, b_mod.reshape(DEPTH, 1, 9 * D))

    rope_tables = _rope_tables(n_lat_t)
    ck_a = jnp.transpose(cache_diff_k, (0, 1, 3, 4, 5, 2)).reshape(n_lat_b, DEPTH, H_A, 2 * DH_A, past)
    cv_a = cache_diff_v.reshape(n_lat_b, DEPTH, past * H_A, 2 * DH_A)
    ck_b = jnp.transpose(cache_win_k, (0, 1, 3, 4, 2)).reshape(n_lat_b, DEPTH, N_KVB, past)
    cv_b = jnp.transpose(cache_win_v, (0, 1, 3, 4, 2)).reshape(n_lat_b, DEPTH, N_KVB, past)
    final_g = final_norm_g.reshape(1, D)

    xp = x_prompt.reshape(1, n_ctx_b * n_ctx_t, D)
    xs = x_sample
    new_cache = None
    ffn_in, ffn_out, w_proj, w_gate, wb, wo = _prepare_weights(ffn_w_in, ffn_w_out, w_in, w_branch, w_out)
    for l in range(DEPTH):
        lam_init = 0.8 - 0.6 * math.exp(-0.3 * l)
        g_l = norm_g[l]
        lam = diff_lambda[l]
        subln = diff_subln_g[l].reshape(1, 2 * DH_A)
        sink = win_sink[l]
        conv_p = (conv_dw_w[l], conv_dw_b[l].reshape(1, C_CONV),
                  conv_norm_g[l].reshape(1, C_CONV), conv_norm_b[l].reshape(1, C_CONV))
        mod_ctx = mod_all[l, 0:1].reshape(1, 9, D)
        mod_lat = mod_all[l, 1:n_rows].reshape(n_lat_b, 9, D)
        last = l == DEPTH - 1

        h = _ffn(xp, mod_ctx, g_l, ffn_in, ffn_out, (l, 0), mod_row=0, g_row=0, tm=tm)
        qa, ka, va, qb, kb, vb, u, *new_cache = _inproj(
            h, mod_ctx, g_l, w_proj, l, tm=tm, kv_dtype=F32,
            cache_seq=(n_ctx_b, n_ctx_t), cache_acc=new_cache)
        seq = lambda a: a.reshape(n_ctx_b, n_ctx_t, a.shape[-1])
        ka, va, kb, vb = seq(ka), seq(va), seq(kb), seq(vb)
        ya = _diff_attn(seq(qa), ka, va, lam, subln, lam_init=lam_init, tq=n_ctx_t, hb=H_A)
        yb = _win_attn(seq(qb), kb, vb, sink, tq=n_ctx_t, nq=1, nb=4, local=False)
        yc = _conv_module(seq(u), *conv_p, rt=n_ctx_t)
        flat = lambda a: a.reshape(1, n_ctx_b * n_ctx_t, a.shape[-1])
        h = _merge(h, mod_ctx, g_l, w_gate, flat(ya), flat(yb), flat(yc), wb, wo, l, tm=tm)
        xp = _ffn(h, mod_ctx, g_l, ffn_in, ffn_out, (l, 1), mod_row=6, g_row=2, tm=tm,
                  final_g=final_g if last else None)

        h = _ffn(xs, mod_lat, g_l, ffn_in, ffn_out, (l, 0), mod_row=0, g_row=0, tm=tm)
        qa, ka, va, qb, kb, vb, u = _inproj(h, mod_lat, g_l, w_proj, l, tm=tm, kv_dtype=BF16,
                                            rope_tables=rope_tables)
        ya = _diff_attn(qa, ka, va, lam, subln, lam_init=lam_init, tq=512, hb=1, cache=(ck_a, cv_a, l))
        yb = _win_attn(qb, kb, vb, sink, tq=WIN_BLOCK, nq=8, nb=1, local=True, cache=(ck_b, cv_b, l))
        yc = _conv_module(u, *conv_p, rt=512)
        h = _merge(h, mod_lat, g_l, w_gate, ya, yb, yc, wb, wo, l, tm=tm)
        xs = _ffn(h, mod_lat, g_l, ffn_in, ffn_out, (l, 1), mod_row=6, g_row=2, tm=tm,
                  final_g=final_g if last else None)

    y_prompt = xp.reshape(n_ctx_b, n_ctx_t, D)
    dk, dv, wk, wv = new_cache
    new_diff_k = jnp.transpose(dk.reshape(n_ctx_b, DEPTH, H_A, 2, DH_A, n_ctx_t), (0, 1, 5, 2, 3, 4))
    new_diff_v = dv.reshape(n_ctx_b, DEPTH, n_ctx_t, H_A, 2 * DH_A)
    new_win_k = jnp.transpose(wk.reshape(n_ctx_b, DEPTH, KV_B, DH_B, n_ctx_t), (0, 1, 4, 2, 3))
    new_win_v = jnp.transpose(wv.reshape(n_ctx_b, DEPTH, KV_B, DH_B, n_ctx_t), (0, 1, 4, 2, 3))
    return (y_prompt, xs, new_diff_k, new_diff_v, new_win_k, new_win_v)
```

```python
import functools
import itertools
import math

import numpy as np
import jax
import jax.numpy as jnp
from jax import lax
from jax.experimental import pallas as pl
from jax.experimental.pallas import tpu as pltpu

D_MODEL = 1024
DEPTH = 2
GRID_W = 64
H_A = 4
DH_A = 64
H_B = 8
KV_B = 2
G_B = H_B // KV_B
DH_B = 64
WINDOW = 128
WIN_BLOCK = 128
C_CONV = 512
CONV_K = 31
BRANCH_W = 512
D_FF = 2816
ROPE_BASE = 10000.0
NORM_EPS = 1e-6
SUBLN_EPS = 1e-5
SCALE_A = DH_A ** -0.5
SCALE_B = DH_B ** -0.5
LOG2E = math.log2(math.e)

N_QA = H_A * 2 * DH_A
N_KVB = KV_B * DH_B
N_ATTN = 3 * N_QA + H_B * DH_B + 2 * N_KVB
N_PROJ = N_ATTN + 2 * C_CONV

LANES = 128
SUBLANES = 8
ROPE_HALF = DH_A // 4
FF_CHUNK = 256
KEY_CHUNK = 256
WIN_SKEW = 1
CONV_HALO = 16
CONV_ROWS = 32
VMEM_LIMIT = 56 * 1024 * 1024

BF16 = jnp.bfloat16
F32 = jnp.float32


def _params(*sem):
    return pltpu.CompilerParams(dimension_semantics=sem, vmem_limit_bytes=VMEM_LIMIT)


def _resident(shape):
    nd = len(shape)
    return pl.BlockSpec(shape, lambda *_: (0,) * nd, pipeline_mode=pl.Buffered(1))


def _resident_slice(shape, lead):
    n = len(lead)
    block = (None,) * n + tuple(shape[n:])
    index = tuple(lead) + (0,) * (len(shape) - n)
    return pl.BlockSpec(block, lambda *_: index, pipeline_mode=pl.Buffered(1))


def _dot(a, b):
    return jnp.dot(a, b, preferred_element_type=F32)


def _dot_nt(a, b):
    return lax.dot_general(a, b, (((1,), (1,)), ((), ())), preferred_element_type=F32)


def _rms(x, eps):
    return x * lax.rsqrt(jnp.mean(x * x, axis=-1, keepdims=True) + eps)


def _mod_rms(x, g, shift, scale):
    return _rms(x, NORM_EPS) * g * (1.0 + scale) + shift


def _silu(x):
    return x * jax.nn.sigmoid(x)


def _mod_kernel(c_ref, w_ref, b_ref, o_ref):
    a = _silu(c_ref[...]).astype(BF16)
    o_ref[...] = _dot(a, w_ref[...].astype(BF16)) + b_ref[...]


def _modulation(cond, w_mod, b_mod):
    R = cond.shape[0]
    tn = 9 * D_MODEL // 4
    return pl.pallas_call(
        _mod_kernel,
        grid=(DEPTH, 9 * D_MODEL // tn),
        in_specs=[
            pl.BlockSpec((R, D_MODEL), lambda l, j: (0, 0)),
            pl.BlockSpec((None, D_MODEL, tn), lambda l, j: (l, 0, j)),
            pl.BlockSpec((None, 1, tn), lambda l, j: (l, 0, j)),
        ],
        out_specs=pl.BlockSpec((None, R, tn), lambda l, j: (l, 0, j)),
        out_shape=jax.ShapeDtypeStruct((DEPTH, R, 9 * D_MODEL), F32),
        compiler_params=_params("parallel", "parallel"),
        name="modulation",
    )(cond, w_mod, b_mod)


def _ffn_kernel(x_ref, mod_ref, g_ref, win_ref, wout_ref, *rest, mod_row, g_row, final):
    if final:
        fg_ref, o_ref = rest
    else:
        (o_ref,) = rest
    x = x_ref[...]
    shift = mod_ref[mod_row:mod_row + 1, :]
    scale = mod_ref[mod_row + 1:mod_row + 2, :]
    gate = mod_ref[mod_row + 2:mod_row + 3, :]
    nb = _mod_rms(x, g_ref[g_row:g_row + 1, :], shift, scale).astype(BF16)
    acc = None
    for lo in range(0, D_FF, FF_CHUNK):
        a = _dot(nb, win_ref[:, lo:lo + FF_CHUNK])
        b = _dot(nb, win_ref[:, D_FF + lo:D_FF + lo + FF_CHUNK])
        hid = (_silu(a) * b).astype(BF16)
        part = _dot(hid, wout_ref[lo:lo + FF_CHUNK, :])
        acc = part if acc is None else acc + part
    h = x + 0.5 * gate * acc
    if final:
        h = _rms(h, NORM_EPS) * fg_ref[...]
    o_ref[...] = h


def _ffn(x, mod, norm_g, w_in, w_out, which, *, mod_row, g_row, tm, final_g=None):
    B, T, D = x.shape
    per_batch = mod.shape[0] > 1
    final = final_g is not None
    l, s = which
    in_specs = [
        pl.BlockSpec((None, tm, D), lambda b, i: (b, i, 0)),
        pl.BlockSpec((None, 9, D), (lambda b, i: (b, 0, 0)) if per_batch else (lambda b, i: (0, 0, 0))),
        _resident(norm_g.shape),
    ]
    in_specs += [_resident_slice(w_in.shape, (l, s)), _resident_slice(w_out.shape, (l, s))]
    args = [x, mod, norm_g, w_in, w_out]
    if final:
        in_specs.append(_resident(final_g.shape))
        args.append(final_g)
    return pl.pallas_call(
        functools.partial(_ffn_kernel, mod_row=mod_row, g_row=g_row, final=final),
        grid=(B, T // tm),
        in_specs=in_specs,
        out_specs=pl.BlockSpec((None, tm, D), lambda b, i: (b, i, 0)),
        out_shape=jax.ShapeDtypeStruct((B, T, D), F32),
        compiler_params=_params("parallel", "parallel"),
        name="ffn",
    )(*args)


def _rope(x, cos, sin_signed, lo_half):
    outs = []
    for j in range(x.shape[1] // LANES):
        xj = x[:, j * LANES:(j + 1) * LANES]
        rot = jnp.where(lo_half, pltpu.roll(xj, LANES - ROPE_HALF, 1), pltpu.roll(xj, ROPE_HALF, 1))
        outs.append(xj * cos + rot * sin_signed)
    return outs[0] if len(outs) == 1 else jnp.concatenate(outs, axis=1)


def _inproj_kernel(x_ref, mod_ref, g_ref, w_ref, *rest, rope, n_aliased, cache_out, cache_layer):
    if rope:
        cos_ref, sin_ref = rest[:2]
        rest = rest[2:]
    rest = rest[n_aliased:]
    qa_ref, ka_ref, va_ref, qb_ref, kb_ref, vb_ref, u_ref = rest[:7]
    x = x_ref[...]
    nb = _mod_rms(x, g_ref[1:2, :], mod_ref[3:4, :], mod_ref[4:5, :]).astype(BF16)

    def proj(lo, hi):
        return _dot(nb, w_ref[:, lo:hi])

    qa = proj(0, N_QA)
    ka = proj(N_QA, 2 * N_QA)
    va = proj(2 * N_QA, 3 * N_QA)
    qb = proj(3 * N_QA, 4 * N_QA)
    kvb = proj(4 * N_QA, N_ATTN)
    kb = kvb[:, :N_KVB]
    vb = kvb[:, N_KVB:]
    cu = proj(N_ATTN, N_PROJ)
    if rope:
        cos = cos_ref[...]
        sin_signed = sin_ref[...]
        lane = lax.broadcasted_iota(jnp.int32, cos.shape, 1)
        lo_half = (lane % (2 * ROPE_HALF)) < ROPE_HALF
        qa = _rope(qa, cos, sin_signed, lo_half)
        ka = _rope(ka, cos, sin_signed, lo_half)
        qb = _rope(qb, cos, sin_signed, lo_half)
        kb = _rope(kb, cos, sin_signed, lo_half)
    qa_ref[...] = (qa * (SCALE_A * LOG2E)).astype(qa_ref.dtype)
    ka_ref[...] = ka.astype(ka_ref.dtype)
    va_ref[...] = va.astype(va_ref.dtype)
    qb_ref[...] = (qb * (SCALE_B * LOG2E)).astype(qb_ref.dtype)
    kb_ref[...] = kb.astype(kb_ref.dtype)
    vb_ref[...] = vb.astype(vb_ref.dtype)
    u_ref[...] = cu[:, :C_CONV] * jax.nn.sigmoid(cu[:, C_CONV:])
    if cache_out:
        cache_refs = rest[7:]
        if n_aliased == 0:
            for ref in cache_refs:
                for other in range(ref.shape[1]):
                    if other != cache_layer:
                        ref[:, other] = jnp.zeros(ref.shape[:1] + ref.shape[2:], F32)
            cache_refs = [ref.at[:, cache_layer] for ref in cache_refs]
        dk_ref, dv_ref, wk_ref, wv_ref = cache_refs
        n_seq, _, _, seq_len = dk_ref.shape
        ka_t, kb_t, vb_t = ka.T, kb.T, vb.T
        for s in range(n_seq):
            tok = slice(s * seq_len, (s + 1) * seq_len)
            for hc in range(2 * H_A):
                dk_ref[s, hc] = ka_t[hc * DH_A:(hc + 1) * DH_A, tok]
            for hh in range(H_A):
                dv_ref[s, pl.ds(hh, seq_len, stride=H_A), :] = va[tok, hh * 2 * DH_A:(hh + 1) * 2 * DH_A]
            wk_ref[s] = kb_t[:, tok]
            wv_ref[s] = vb_t[:, tok]


def _inproj(h, mod, norm_g, w_proj, layer, *, tm, kv_dtype, rope_tables=None, cache_seq=None, cache_acc=None):
    B, T, D = h.shape
    per_batch = mod.shape[0] > 1
    rope = rope_tables is not None
    row = lambda b, i: (b, i, 0)
    in_specs = [
        pl.BlockSpec((None, tm, D), row),
        pl.BlockSpec((None, 9, D), (lambda b, i: (b, 0, 0)) if per_batch else (lambda b, i: (0, 0, 0))),
        _resident(norm_g.shape),
        _resident_slice(w_proj.shape, (layer,)),
    ]
    args = [h, mod, norm_g, w_proj]
    if rope:
        in_specs += [pl.BlockSpec((tm, LANES), lambda b, i: (i, 0))] * 2
        args += list(rope_tables)
    widths = (N_QA, N_QA, N_QA, H_B * DH_B, N_KVB, N_KVB, C_CONV)
    dtypes = (BF16, kv_dtype, kv_dtype, BF16, kv_dtype, kv_dtype, F32)
    out_specs = [pl.BlockSpec((None, tm, w), row) for w in widths]
    out_shape = [jax.ShapeDtypeStruct((B, T, w), dt) for w, dt in zip(widths, dtypes)]
    aliases = {}
    n_aliased = 0
    if cache_seq is not None:
        n_all, seq_len = cache_seq
        n_seq = tm // seq_len
        assert B == 1 and n_seq * seq_len == tm
        cache_shapes = ((2 * H_A, DH_A, seq_len), (seq_len * H_A, 2 * DH_A), (N_KVB, seq_len), (N_KVB, seq_len))
        for shape in cache_shapes:
            nd = len(shape)
            if cache_acc is None:
                spec = pl.BlockSpec((n_seq, DEPTH) + shape, lambda b, i, nd=nd: (i, 0) + (0,) * nd)
            else:
                spec = pl.BlockSpec((n_seq, None) + shape, lambda b, i, nd=nd: (i, layer) + (0,) * nd)
            out_specs.append(spec)
            out_shape.append(jax.ShapeDtypeStruct((n_all, DEPTH) + shape, F32))
        if cache_acc is not None:
            n_aliased = len(cache_acc)
            aliases = {len(args) + j: len(widths) + j for j in range(n_aliased)}
            in_specs += [pl.BlockSpec(memory_space=pl.ANY)] * n_aliased
            args += list(cache_acc)
    return pl.pallas_call(
        functools.partial(_inproj_kernel, rope=rope, n_aliased=n_aliased, cache_out=cache_seq is not None,
                          cache_layer=layer),
        grid=(B, T // tm),
        in_specs=in_specs,
        out_specs=out_specs,
        out_shape=out_shape,
        input_output_aliases=aliases,
        compiler_params=_params("parallel", "parallel"),
        name="inproj",
    )(*args)


def _diff_attn_kernel(*refs, n_cache, lam_init, nq, hb):
    if n_cache:
        q_ref, k_ref, v_ref, ck_ref, cv_ref, lam_ref, g_ref, o_ref, s_even, s_odd, m_even, m_odd, vcache = refs
    else:
        q_ref, k_ref, v_ref, lam_ref, g_ref, o_ref, s_even, s_odd, m_even, m_odd = refs
    hd = 2 * DH_A
    t = pl.program_id(0)
    heads = [(hh, slice(hh * hd, (hh + 1) * hd)) for hh in range(hb)]

    @pl.when(t == 0)
    def _():
        s_odd[...] = jnp.zeros(s_odd.shape, F32)
        m_odd[...] = jnp.zeros(m_odd.shape, F32)
        if n_cache:
            vcache[...] = jnp.zeros(vcache.shape, BF16)

    if n_cache:
        @pl.when((t > 0) & ((t - 1) % nq == 0))
        def _():
            head = ((t - 1) // nq) % H_A
            vcache[...] = cv_ref[pl.ds(head, n_cache, stride=H_A), :].astype(BF16)

    def step(s_write, m_write, s_read, m_read):
        lam = lam_ref[...]
        lam_full = (jnp.exp(jnp.sum(lam[0:1] * lam[1:2], axis=1, keepdims=True))
                    - jnp.exp(jnp.sum(lam[2:3] * lam[3:4], axis=1, keepdims=True)) + lam_init)
        tq = q_ref.shape[0]
        lane = lax.broadcasted_iota(jnp.int32, (tq, hd), 1)
        zero = jnp.zeros((tq, hd), q_ref.dtype)
        for hh, lanes in heads:
            q = q_ref[:, lanes]
            q2 = jnp.concatenate([jnp.where(lane < DH_A, q, zero), jnp.where(lane >= DH_A, q, zero)], axis=0)
            row_max = jnp.concatenate([m_read[hh]] * (KEY_CHUNK // LANES), axis=1)
            acc = None
            run_max = None
            for lo in range(0, s_read.shape[2], KEY_CHUNK):
                cols = slice(lo, lo + KEY_CHUNK)
                if lo < n_cache:
                    s = _dot(q2, ck_ref[:, cols].astype(BF16))
                    v = vcache[cols, :]
                else:
                    own = slice(lo - n_cache, lo - n_cache + KEY_CHUNK)
                    s = _dot_nt(q2, k_ref[own, lanes].astype(BF16))
                    v = v_ref[own, lanes].astype(BF16)
                s_write[hh, :, cols] = s
                run_max = s if run_max is None else jnp.maximum(run_max, s)
                e = jnp.exp2(s_read[hh, :, cols] - row_max).astype(BF16)
                part = _dot(e, jnp.concatenate([v, jnp.ones(v.shape, BF16)], axis=1))
                acc = part if acc is None else acc + part
            m_write[hh] = jnp.broadcast_to(jnp.max(run_max, axis=1, keepdims=True), m_write.shape[1:])
            ratio = acc[:, :hd] / acc[:, hd:]
            o = ratio[:tq] - lam_full * ratio[tq:]
            o = _rms(o, SUBLN_EPS) * g_ref[...] * (1.0 - lam_init)
            o_ref[:, lanes] = o.astype(o_ref.dtype)

    pl.when(t % 2 == 0)(lambda: step(s_even, m_even, s_odd, m_odd))
    pl.when(t % 2 == 1)(lambda: step(s_odd, m_odd, s_even, m_even))


def _diff_attn(q, k, v, lam, subln_g, *, lam_init, tq, hb, cache=None):
    B, T, _ = q.shape
    n_cache = 0 if cache is None else cache[0].shape[-1]
    assert n_cache % KEY_CHUNK == 0 and (cache is None or hb == 1)
    hd = 2 * DH_A
    nq = T // tq
    n_groups = H_A // hb
    n_blocks = B * n_groups * nq

    def block_of(t):
        blk = jnp.clip(t, 0, n_blocks - 1)
        return blk // (n_groups * nq), (blk // nq) % n_groups, blk % nq

    def q_map(t):
        b, h, i = block_of(t)
        return b, i, h

    def kv_map(t):
        b, h, _ = block_of(t)
        return b, 0, h

    def out_map(t):
        return q_map(t - 1)

    in_specs = [
        pl.BlockSpec((None, tq, hb * hd), q_map),
        pl.BlockSpec((None, T, hb * hd), kv_map),
        pl.BlockSpec((None, T, hb * hd), lambda t: kv_map(t - 1)),
    ]
    args = [q, k, v]
    if cache is not None:
        ck, cv, layer = cache

        def ck_map(t):
            b, h, _ = block_of(t)
            return b, layer, h, 0, 0

        def cv_map(t):
            b, _, _ = block_of(t - 1)
            return b, layer, 0, 0

        in_specs += [pl.BlockSpec((None, None, None, hd, n_cache), ck_map),
                     pl.BlockSpec((None, None, n_cache * H_A, hd), cv_map)]
        args += [ck, cv]
    in_specs += [_resident(lam.shape), _resident(subln_g.shape)]
    args += [lam, subln_g]
    n_keys = n_cache + T
    return pl.pallas_call(
        functools.partial(_diff_attn_kernel, n_cache=n_cache, lam_init=lam_init, nq=nq, hb=hb),
        grid=(n_blocks + 1,),
        in_specs=in_specs,
        out_specs=pl.BlockSpec((None, tq, hb * hd), out_map),
        out_shape=jax.ShapeDtypeStruct((B, T, N_QA), BF16),
        scratch_shapes=[pltpu.VMEM((hb, 2 * tq, n_keys), F32), pltpu.VMEM((hb, 2 * tq, n_keys), F32),
                        pltpu.VMEM((hb, 2 * tq, LANES), F32), pltpu.VMEM((hb, 2 * tq, LANES), F32)]
        + ([pltpu.VMEM((n_cache, hd), BF16)] if n_cache else []),
        compiler_params=_params("arbitrary"),
        name="diff_attn",
    )(*args)


def _win_attn_kernel(*refs, local, has_ctx, seq_len, tq, nq, nb):
    if has_ctx:
        q_ref, k_ref, v_ref, ck_ref, cv_ref, sink_ref, o_ref = refs
        ck = ck_ref[...].astype(BF16)
        cv = cv_ref[...].astype(BF16)
        cv_ones = jnp.concatenate([cv, jnp.ones(cv.shape, BF16)], axis=0)
    else:
        q_ref, k_ref, v_ref, sink_ref, o_ref = refs
    n_loc = 3 * WIN_BLOCK
    lane = lax.broadcasted_iota(jnp.int32, (tq, LANES), 1)
    zero = jnp.zeros((tq, LANES), BF16)
    blocks = {}

    def block(bi, qi):
        if (bi, qi) in blocks:
            return blocks[bi, qi]
        blk = {"rows": slice(qi * tq, (qi + 1) * tq)}
        blk["q"] = q_ref[bi, blk["rows"], :]
        if local:
            n = pl.program_id(1) * nq + qi
            start = pl.multiple_of(jnp.clip((n - 1) * WIN_BLOCK, 0, seq_len - n_loc), WIN_BLOCK)
            blk["k"] = k_ref[bi, pl.ds(start, n_loc), :].astype(BF16)
            v = v_ref[bi, pl.ds(start, n_loc), :].astype(BF16)
            rows = lax.broadcasted_iota(jnp.int32, (tq, n_loc), 0)
            cols = lax.broadcasted_iota(jnp.int32, (tq, n_loc), 1)
            blk["valid"] = jnp.abs(cols - rows - (n * WIN_BLOCK - start)) <= WINDOW
        else:
            blk["k"] = k_ref[bi].astype(BF16)
            v = v_ref[bi].astype(BF16)
        blk["v_ones"] = jnp.concatenate([v, jnp.ones(v.shape, BF16)], axis=1)
        blocks[bi, qi] = blk
        return blk

    def scores(bi, qi, j):
        blk = block(bi, qi)
        sel = (lane >= j * DH_B) & (lane < (j + 1) * DH_B)
        qj = jnp.concatenate(
            [jnp.where(sel, blk["q"][:, g * LANES:(g + 1) * LANES], zero) for g in range(G_B)], axis=0)
        s = _dot_nt(qj, blk["k"])
        if has_ctx:
            s = jnp.concatenate([s, _dot(qj, ck)], axis=1)
        return s

    def softmax(bi, qi, j, s):
        es, sink_terms = [], []
        for g in range(G_B):
            sg = s[g * tq:(g + 1) * tq]
            if local:
                band = jnp.where(block(bi, qi)["valid"], sg[:, :n_loc], -jnp.inf)
                sg = jnp.concatenate([band, sg[:, n_loc:]], axis=1) if has_ctx else band
            sink = sink_ref[j * G_B + g] * LOG2E
            m = jnp.maximum(jnp.max(sg, axis=1, keepdims=True), sink)
            es.append(jnp.exp2(sg - m).astype(BF16))
            sink_terms.append(jnp.exp2(sink - m))
        return jnp.concatenate(es, axis=0), sink_terms

    def values(bi, qi, e):
        blk = block(bi, qi)
        n_own = blk["k"].shape[0]
        ov = _dot(e[:, :n_own], blk["v_ones"])
        if has_ctx:
            ov = ov + _dot_nt(e[:, n_own:], cv_ones)
        return ov

    def finish(bi, qi, halves):
        for g in range(G_B):
            rows_g = slice(g * tq, (g + 1) * tq)
            outs = [ov[rows_g, :LANES] / (ov[rows_g, LANES:] + sink_terms[g]) for ov, sink_terms in halves]
            og = jnp.where(lane < DH_B, outs[0], outs[1])
            o_ref[bi, block(bi, qi)["rows"], g * LANES:(g + 1) * LANES] = og.astype(o_ref.dtype)

    units = list(itertools.product(range(nb), range(nq), range(KV_B)))
    pending_s, pending_e, done = {}, {}, {}
    lag = WIN_SKEW
    for i in range(len(units) + 2 * lag):
        if i < len(units):
            pending_s[i] = scores(*units[i])
        if 0 <= i - lag < len(units):
            pending_e[i - lag] = softmax(*units[i - lag], pending_s.pop(i - lag))
        if 0 <= i - 2 * lag < len(units):
            bi, qi, j = units[i - 2 * lag]
            e, sink_terms = pending_e.pop(i - 2 * lag)
            done[bi, qi, j] = (values(bi, qi, e), sink_terms)
            if j == KV_B - 1:
                finish(bi, qi, [done.pop((bi, qi, jj)) for jj in range(KV_B)])


def _win_attn(q, k, v, sink, *, tq, nq, nb, local, cache=None):
    B, T, _ = q.shape
    has_ctx = cache is not None
    assert not local or tq == WIN_BLOCK
    assert not has_ctx or nb == 1
    rows = nq * tq
    in_specs = [
        pl.BlockSpec((nb, rows, G_B * LANES), lambda b, n: (b, n, 0)),
        pl.BlockSpec((nb, T, N_KVB), lambda b, n: (b, 0, 0)),
        pl.BlockSpec((nb, T, N_KVB), lambda b, n: (b, 0, 0)),
    ]
    args = [q, k, v]
    if has_ctx:
        ck, cv, layer = cache
        spec = pl.BlockSpec((None, None, N_KVB, ck.shape[3]), lambda b, n: (b, layer, 0, 0))
        in_specs += [spec, spec]
        args += [ck, cv]
    in_specs.append(pl.BlockSpec(memory_space=pltpu.SMEM))
    args.append(sink)
    return pl.pallas_call(
        functools.partial(_win_attn_kernel, local=local, has_ctx=has_ctx, seq_len=T, tq=tq, nq=nq, nb=nb),
        grid=(B // nb, T // rows),
        in_specs=in_specs,
        out_specs=pl.BlockSpec((nb, rows, G_B * LANES), lambda b, n: (b, n, 0)),
        out_shape=jax.ShapeDtypeStruct((B, T, H_B * DH_B), BF16),
        compiler_params=_params("parallel", "parallel"),
        name="win_attn",
    )(*args)


def _conv_kernel(prev_ref, cur_ref, next_ref, w_ref, bdw_ref, g_ref, b_ref, o_ref, win_ref, ph_ref):
    i = pl.program_id(1)
    rt = cur_ref.shape[0]
    zeros = jnp.zeros((CONV_HALO, C_CONV), F32)
    win_ref[0:CONV_HALO, :] = jnp.where(i > 0, prev_ref[...], zeros)
    win_ref[CONV_HALO:CONV_HALO + rt, :] = cur_ref[...]
    win_ref[CONV_HALO + rt:, :] = jnp.where(i < pl.num_programs(1) - 1, next_ref[...], zeros)
    n_ph = ph_ref.shape[1]
    for p in range(1, SUBLANES):
        ph_ref[p - 1] = win_ref[p:p + n_ph, :]
    first = CONV_HALO - CONV_K // 2
    tiles = CONV_ROWS // SUBLANES
    for r0 in range(0, rt, CONV_ROWS):
        acc = jnp.zeros((tiles, SUBLANES, C_CONV), F32) + bdw_ref[...]
        for t in range(CONV_K):
            p = (first + t) % SUBLANES
            base = r0 + first + t - p
            x = win_ref[base:base + CONV_ROWS, :] if p == 0 else ph_ref[p - 1, base:base + CONV_ROWS, :]
            acc = acc + x.reshape(tiles, SUBLANES, C_CONV) * w_ref[t]
        acc = acc.reshape(CONV_ROWS, C_CONV)
        mu = jnp.mean(acc, axis=-1, keepdims=True)
        cen = acc - mu
        var = jnp.mean(cen * cen, axis=-1, keepdims=True)
        y = cen * lax.rsqrt(var + NORM_EPS) * g_ref[...] + b_ref[...]
        o_ref[r0:r0 + CONV_ROWS, :] = _silu(y).astype(o_ref.dtype)


def _conv_module(u, w_dw, b_dw, g, b, *, rt):
    B, T, C = u.shape
    hb = rt // CONV_HALO
    last = T // CONV_HALO - 1
    n_ph = rt + (CONV_HALO + CONV_K // 2) // SUBLANES * SUBLANES
    w_tiles = jnp.broadcast_to(w_dw[:, None, :], (CONV_K, SUBLANES, C))
    in_specs = [
        pl.BlockSpec((None, CONV_HALO, C), lambda bb, i: (bb, jnp.maximum(i * hb - 1, 0), 0)),
        pl.BlockSpec((None, rt, C), lambda bb, i: (bb, i, 0)),
        pl.BlockSpec((None, CONV_HALO, C), lambda bb, i: (bb, jnp.minimum((i + 1) * hb, last), 0)),
        _resident(w_tiles.shape), _resident(b_dw.shape), _resident(g.shape), _resident(b.shape),
    ]
    return pl.pallas_call(
        _conv_kernel,
        grid=(B, T // rt),
        in_specs=in_specs,
        out_specs=pl.BlockSpec((None, rt, C), lambda bb, i: (bb, i, 0)),
        out_shape=jax.ShapeDtypeStruct((B, T, C), BF16),
        scratch_shapes=[pltpu.VMEM((rt + 2 * CONV_HALO, C), F32),
                        pltpu.VMEM((SUBLANES - 1, n_ph, C), F32)],
        compiler_params=_params("parallel", "parallel"),
        name="conv_module",
    )(u, u, u, w_tiles, b_dw, g, b)


def _merge_kernel(h_ref, mod_ref, g_ref, ya_ref, yb_ref, yc_ref, wg_ref, wb_ref, wo_ref, o_ref):
    h = h_ref[...]
    nb = _mod_rms(h, g_ref[1:2, :], mod_ref[3:4, :], mod_ref[4:5, :]).astype(BF16)
    merged = None
    for i, y_ref in enumerate((ya_ref, yb_ref, yc_ref)):
        gate = jax.nn.sigmoid(_dot(nb, wg_ref[:, i * D_MODEL:(i + 1) * D_MODEL]))
        term = gate * _dot(y_ref[...], wb_ref[i])
        merged = term if merged is None else merged + term
    o_ref[...] = h + mod_ref[5:6, :] * _dot(merged.astype(BF16), wo_ref[...])


def _merge(h, mod, norm_g, w_gate, ya, yb, yc, w_branch, w_out, layer, *, tm):
    B, T, D = h.shape
    per_batch = mod.shape[0] > 1
    row = lambda b, i: (b, i, 0)
    y_spec = pl.BlockSpec((None, tm, BRANCH_W), row)
    return pl.pallas_call(
        _merge_kernel,
        grid=(B, T // tm),
        in_specs=[
            pl.BlockSpec((None, tm, D), row),
            pl.BlockSpec((None, 9, D), (lambda b, i: (b, 0, 0)) if per_batch else (lambda b, i: (0, 0, 0))),
            _resident(norm_g.shape), y_spec, y_spec, y_spec,
            _resident_slice(w_gate.shape, (layer,)), _resident_slice(w_branch.shape, (layer,)),
            _resident_slice(w_out.shape, (layer,)),
        ],
        out_specs=pl.BlockSpec((None, tm, D), row),
        out_shape=jax.ShapeDtypeStruct((B, T, D), F32),
        compiler_params=_params("parallel", "parallel"),
        name="merge",
    )(h, mod, norm_g, ya, yb, yc, w_gate, w_branch, w_out)


def _rope_tables(n_tokens):
    quarter = DH_A // 4
    t = np.arange(n_tokens)
    inv = (ROPE_BASE ** (-np.arange(0, DH_A // 2, 2, dtype=np.float32) / (DH_A // 2))).astype(np.float32)
    pos = np.stack([(t // GRID_W).astype(np.float32), (t % GRID_W).astype(np.float32)], axis=1)
    ang = (pos[:, :, None] * inv[None, None, :]).astype(np.float32)
    ang = np.broadcast_to(ang[:, :, None, :], (n_tokens, 2, 2, quarter)).reshape(n_tokens, DH_A)
    sign = np.where((np.arange(DH_A) % (2 * quarter)) < quarter, -1.0, 1.0).astype(np.float32)
    cos = np.cos(ang).astype(np.float32)
    sin_signed = (np.sin(ang) * sign).astype(np.float32)
    reps = LANES // DH_A
    return jnp.asarray(np.tile(cos, (1, reps))), jnp.asarray(np.tile(sin_signed, (1, reps)))


def _group_major(w, axis):
    shape = w.shape
    split = shape[:axis] + (KV_B, G_B, DH_B) + shape[axis + 1:]
    return jnp.swapaxes(w.reshape(split), axis, axis + 1).reshape(shape)


def _prepare_weights(ffn_w_in, ffn_w_out, w_in, w_branch, w_out):
    qb_lo = 3 * N_QA
    qb_hi = qb_lo + H_B * DH_B
    w_proj = jnp.concatenate(
        [w_in[:, :, :qb_lo], _group_major(w_in[:, :, qb_lo:qb_hi], 2), w_in[:, :, qb_hi:N_PROJ]],
        axis=2).astype(BF16)
    w_gate = w_in[:, :, N_PROJ:].astype(BF16)
    wb = jnp.concatenate(
        [w_branch[:, 0:1], _group_major(w_branch[:, 1:2], 2), w_branch[:, 2:3]], axis=1).astype(BF16)
    return ffn_w_in.astype(BF16), ffn_w_out.astype(BF16), w_proj, w_gate, wb, w_out.astype(BF16)


def kernel(x_prompt, x_sample, cache_diff_k, cache_diff_v, cache_win_k, cache_win_v, c, c_ctx,
           w_mod, b_mod, norm_g, ffn_w_in, ffn_w_out, w_in, diff_lambda, diff_subln_g, win_sink,
           conv_dw_w, conv_dw_b, conv_norm_g, conv_norm_b, w_branch, w_out, final_norm_g):
    n_ctx_b, n_ctx_t, D = x_prompt.shape
    n_lat_b, n_lat_t, _ = x_sample.shape
    past = cache_diff_k.shape[2]
    tm = 512

    n_rows = 1 + n_lat_b
    pad_rows = -n_rows % 8
    cond = jnp.concatenate([c_ctx[None, :], c, jnp.zeros((pad_rows, D), F32)], axis=0)
    mod_all = _modulation(cond, w_mod, b_mod.reshape(DEPTH, 1, 9 * D))

    rope_tables = _rope_tables(n_lat_t)
    ck_a = jnp.transpose(cache_diff_k, (0, 1, 3, 4, 5, 2)).reshape(n_lat_b, DEPTH, H_A, 2 * DH_A, past)
    cv_a = cache_diff_v.reshape(n_lat_b, DEPTH, past * H_A, 2 * DH_A)
    ck_b = jnp.transpose(cache_win_k, (0, 1, 3, 4, 2)).reshape(n_lat_b, DEPTH, N_KVB, past)
    cv_b = jnp.transpose(cache_win_v, (0, 1, 3, 4, 2)).reshape(n_lat_b, DEPTH, N_KVB, past)
    final_g = final_norm_g.reshape(1, D)

    xp = x_prompt.reshape(1, n_ctx_b * n_ctx_t, D)
    xs = x_sample
    new_cache = None
    ffn_in, ffn_out, w_proj, w_gate, wb, wo = _prepare_weights(ffn_w_in, ffn_w_out, w_in, w_branch, w_out)
    for l in range(DEPTH):
        lam_init = 0.8 - 0.6 * math.exp(-0.3 * l)
        g_l = norm_g[l]
        lam = diff_lambda[l]
        subln = diff_subln_g[l].reshape(1, 2 * DH_A)
        sink = win_sink[l]
        conv_p = (conv_dw_w[l], conv_dw_b[l].reshape(1, C_CONV),
                  conv_norm_g[l].reshape(1, C_CONV), conv_norm_b[l].reshape(1, C_CONV))
        mod_ctx = mod_all[l, 0:1].reshape(1, 9, D)
        mod_lat = mod_all[l, 1:n_rows].reshape(n_lat_b, 9, D)
        last = l == DEPTH - 1

        h = _ffn(xp, mod_ctx, g_l, ffn_in, ffn_out, (l, 0), mod_row=0, g_row=0, tm=tm)
        qa, ka, va, qb, kb, vb, u, *new_cache = _inproj(
            h, mod_ctx, g_l, w_proj, l, tm=tm, kv_dtype=F32,
            cache_seq=(n_ctx_b, n_ctx_t), cache_acc=new_cache)
        seq = lambda a: a.reshape(n_ctx_b, n_ctx_t, a.shape[-1])
        ka, va, kb, vb = seq(ka), seq(va), seq(kb), seq(vb)
        ya = _diff_attn(seq(qa), ka, va, lam, subln, lam_init=lam_init, tq=n_ctx_t, hb=H_A)
        yb = _win_attn(seq(qb), kb, vb, sink, tq=n_ctx_t, nq=1, nb=4, local=False)
        yc = _conv_module(seq(u), *conv_p, rt=n_ctx_t)
        flat = lambda a: a.reshape(1, n_ctx_b * n_ctx_t, a.shape[-1])
        h = _merge(h, mod_ctx, g_l, w_gate, flat(ya), flat(yb), flat(yc), wb, wo, l, tm=tm)
        xp = _ffn(h, mod_ctx, g_l, ffn_in, ffn_out, (l, 1), mod_row=6, g_row=2, tm=tm,
                  final_g=final_g if last else None)

        h = _ffn(xs, mod_lat, g_l, ffn_in, ffn_out, (l, 0), mod_row=0, g_row=0, tm=tm)
        qa, ka, va, qb, kb, vb, u = _inproj(h, mod_lat, g_l, w_proj, l, tm=tm, kv_dtype=BF16,
                                            rope_tables=rope_tables)
        ya = _diff_attn(qa, ka, va, lam, subln, lam_init=lam_init, tq=1024, hb=1, cache=(ck_a, cv_a, l))
        yb = _win_attn(qb, kb, vb, sink, tq=WIN_BLOCK, nq=8, nb=1, local=True, cache=(ck_b, cv_b, l))
        yc = _conv_module(u, *conv_p, rt=512)
        h = _merge(h, mod_lat, g_l, w_gate, ya, yb, yc, wb, wo, l, tm=tm)
        xs = _ffn(h, mod_lat, g_l, ffn_in, ffn_out, (l, 1), mod_row=6, g_row=2, tm=tm,
                  final_g=final_g if last else None)

    y_prompt = xp.reshape(n_ctx_b, n_ctx_t, D)
    dk, dv, wk, wv = new_cache
    new_diff_k = jnp.transpose(dk.reshape(n_ctx_b, DEPTH, H_A, 2, DH_A, n_ctx_t), (0, 1, 5, 2, 3, 4))
    new_diff_v = dv.reshape(n_ctx_b, DEPTH, n_ctx_t, H_A, 2 * DH_A)
    new_win_k = jnp.transpose(wk.reshape(n_ctx_b, DEPTH, KV_B, DH_B, n_ctx_t), (0, 1, 4, 2, 3))
    new_win_v = jnp.transpose(wv.reshape(n_ctx_b, DEPTH, KV_B, DH_B, n_ctx_t), (0, 1, 4, 2, 3))
    return (y_prompt, xs, new_diff_k, new_diff_v, new_win_k, new_win_v)
```

```python
import functools
import itertools
import math

import numpy as np
import jax
import jax.numpy as jnp
from jax import lax
from jax.experimental import pallas as pl
from jax.experimental.pallas import tpu as pltpu

D_MODEL = 1024
DEPTH = 2
GRID_W = 64
H_A = 4
DH_A = 64
H_B = 8
KV_B = 2
G_B = H_B // KV_B
DH_B = 64
WINDOW = 128
WIN_BLOCK = 128
C_CONV = 512
CONV_K = 31
BRANCH_W = 512
D_FF = 2816
ROPE_BASE = 10000.0
NORM_EPS = 1e-6
SUBLN_EPS = 1e-5
SCALE_A = DH_A ** -0.5
SCALE_B = DH_B ** -0.5
LOG2E = math.log2(math.e)

N_QA = H_A * 2 * DH_A
N_KVB = KV_B * DH_B
N_ATTN = 3 * N_QA + H_B * DH_B + 2 * N_KVB
N_PROJ = N_ATTN + 2 * C_CONV

LANES = 128
SUBLANES = 8
ROPE_HALF = DH_A // 4
FF_CHUNK = 256
KEY_CHUNK = 256
WIN_SKEW = 1
CONV_HALO = 16
CONV_ROWS = 32
VMEM_LIMIT = 56 * 1024 * 1024

BF16 = jnp.bfloat16
F32 = jnp.float32


def _params(*sem):
    return pltpu.CompilerParams(dimension_semantics=sem, vmem_limit_bytes=VMEM_LIMIT)


def _resident(shape):
    nd = len(shape)
    return pl.BlockSpec(shape, lambda *_: (0,) * nd, pipeline_mode=pl.Buffered(1))


def _resident_slice(shape, lead):
    n = len(lead)
    block = (None,) * n + tuple(shape[n:])
    index = tuple(lead) + (0,) * (len(shape) - n)
    return pl.BlockSpec(block, lambda *_: index, pipeline_mode=pl.Buffered(1))


def _dot(a, b):
    return jnp.dot(a, b, preferred_element_type=F32)


def _dot_nt(a, b):
    return lax.dot_general(a, b, (((1,), (1,)), ((), ())), preferred_element_type=F32)


def _rms(x, eps):
    return x * lax.rsqrt(jnp.mean(x * x, axis=-1, keepdims=True) + eps)


def _mod_rms(x, g, shift, scale):
    return _rms(x, NORM_EPS) * g * (1.0 + scale) + shift


def _silu(x):
    return x * jax.nn.sigmoid(x)


def _mod_kernel(c_ref, w_ref, b_ref, o_ref):
    a = _silu(c_ref[...]).astype(BF16)
    o_ref[...] = _dot(a, w_ref[...].astype(BF16)) + b_ref[...]


def _modulation(cond, w_mod, b_mod):
    R = cond.shape[0]
    tn = 9 * D_MODEL // 4
    return pl.pallas_call(
        _mod_kernel,
        grid=(DEPTH, 9 * D_MODEL // tn),
        in_specs=[
            pl.BlockSpec((R, D_MODEL), lambda l, j: (0, 0)),
            pl.BlockSpec((None, D_MODEL, tn), lambda l, j: (l, 0, j)),
            pl.BlockSpec((None, 1, tn), lambda l, j: (l, 0, j)),
        ],
        out_specs=pl.BlockSpec((None, R, tn), lambda l, j: (l, 0, j)),
        out_shape=jax.ShapeDtypeStruct((DEPTH, R, 9 * D_MODEL), F32),
        compiler_params=_params("parallel", "parallel"),
        name="modulation",
    )(cond, w_mod, b_mod)


def _ffn_kernel(x_ref, mod_ref, g_ref, win_ref, wout_ref, *rest, mod_row, g_row, final):
    if final:
        fg_ref, o_ref = rest
    else:
        (o_ref,) = rest
    x = x_ref[...]
    shift = mod_ref[mod_row:mod_row + 1, :]
    scale = mod_ref[mod_row + 1:mod_row + 2, :]
    gate = mod_ref[mod_row + 2:mod_row + 3, :]
    nb = _mod_rms(x, g_ref[g_row:g_row + 1, :], shift, scale).astype(BF16)
    acc = None
    for lo in range(0, D_FF, FF_CHUNK):
        a = _dot(nb, win_ref[:, lo:lo + FF_CHUNK])
        b = _dot(nb, win_ref[:, D_FF + lo:D_FF + lo + FF_CHUNK])
        hid = (_silu(a) * b).astype(BF16)
        part = _dot(hid, wout_ref[lo:lo + FF_CHUNK, :])
        acc = part if acc is None else acc + part
    h = x + 0.5 * gate * acc
    if final:
        h = _rms(h, NORM_EPS) * fg_ref[...]
    o_ref[...] = h


def _ffn(x, mod, norm_g, w_in, w_out, which, *, mod_row, g_row, tm, final_g=None):
    B, T, D = x.shape
    per_batch = mod.shape[0] > 1
    final = final_g is not None
    l, s = which
    in_specs = [
        pl.BlockSpec((None, tm, D), lambda b, i: (b, i, 0)),
        pl.BlockSpec((None, 9, D), (lambda b, i: (b, 0, 0)) if per_batch else (lambda b, i: (0, 0, 0))),
        _resident(norm_g.shape),
    ]
    in_specs += [_resident_slice(w_in.shape, (l, s)), _resident_slice(w_out.shape, (l, s))]
    args = [x, mod, norm_g, w_in, w_out]
    if final:
        in_specs.append(_resident(final_g.shape))
        args.append(final_g)
    return pl.pallas_call(
        functools.partial(_ffn_kernel, mod_row=mod_row, g_row=g_row, final=final),
        grid=(B, T // tm),
        in_specs=in_specs,
        out_specs=pl.BlockSpec((None, tm, D), lambda b, i: (b, i, 0)),
        out_shape=jax.ShapeDtypeStruct((B, T, D), F32),
        compiler_params=_params("parallel", "parallel"),
        name="ffn",
    )(*args)


def _rope(x, cos, sin_signed, lo_half):
    outs = []
    for j in range(x.shape[1] // LANES):
        xj = x[:, j * LANES:(j + 1) * LANES]
        rot = jnp.where(lo_half, pltpu.roll(xj, LANES - ROPE_HALF, 1), pltpu.roll(xj, ROPE_HALF, 1))
        outs.append(xj * cos + rot * sin_signed)
    return outs[0] if len(outs) == 1 else jnp.concatenate(outs, axis=1)


def _inproj_kernel(x_ref, mod_ref, g_ref, w_ref, *rest, rope, n_aliased, cache_out, cache_layer):
    if rope:
        cos_ref, sin_ref = rest[:2]
        rest = rest[2:]
    rest = rest[n_aliased:]
    qa_ref, ka_ref, va_ref, qb_ref, kb_ref, vb_ref, u_ref = rest[:7]
    x = x_ref[...]
    nb = _mod_rms(x, g_ref[1:2, :], mod_ref[3:4, :], mod_ref[4:5, :]).astype(BF16)

    def proj(lo, hi):
        return _dot(nb, w_ref[:, lo:hi])

    qa = proj(0, N_QA)
    ka = proj(N_QA, 2 * N_QA)
    va = proj(2 * N_QA, 3 * N_QA)
    qb = proj(3 * N_QA, 4 * N_QA)
    kvb = proj(4 * N_QA, N_ATTN)
    kb = kvb[:, :N_KVB]
    vb = kvb[:, N_KVB:]
    cu = proj(N_ATTN, N_PROJ)
    if rope:
        cos = cos_ref[...]
        sin_signed = sin_ref[...]
        lane = lax.broadcasted_iota(jnp.int32, cos.shape, 1)
        lo_half = (lane % (2 * ROPE_HALF)) < ROPE_HALF
        qa = _rope(qa, cos, sin_signed, lo_half)
        ka = _rope(ka, cos, sin_signed, lo_half)
        qb = _rope(qb, cos, sin_signed, lo_half)
        kb = _rope(kb, cos, sin_signed, lo_half)
    qa_ref[...] = (qa * (SCALE_A * LOG2E)).astype(qa_ref.dtype)
    ka_ref[...] = ka.astype(ka_ref.dtype)
    va_ref[...] = va.astype(va_ref.dtype)
    qb_ref[...] = (qb * (SCALE_B * LOG2E)).astype(qb_ref.dtype)
    kb_ref[...] = kb.astype(kb_ref.dtype)
    vb_ref[...] = vb.astype(vb_ref.dtype)
    u_ref[...] = cu[:, :C_CONV] * jax.nn.sigmoid(cu[:, C_CONV:])
    if cache_out:
        cache_refs = rest[7:]
        if n_aliased == 0:
            for ref in cache_refs:
                for other in range(ref.shape[1]):
                    if other != cache_layer:
                        ref[:, other] = jnp.zeros(ref.shape[:1] + ref.shape[2:], F32)
            cache_refs = [ref.at[:, cache_layer] for ref in cache_refs]
        dk_ref, dv_ref, wk_ref, wv_ref = cache_refs
        n_seq, _, _, seq_len = dk_ref.shape
        ka_t, kb_t, vb_t = ka.T, kb.T, vb.T
        for s in range(n_seq):
            tok = slice(s * seq_len, (s + 1) * seq_len)
            for hc in range(2 * H_A):
                dk_ref[s, hc] = ka_t[hc * DH_A:(hc + 1) * DH_A, tok]
            for hh in range(H_A):
                dv_ref[s, pl.ds(hh, seq_len, stride=H_A), :] = va[tok, hh * 2 * DH_A:(hh + 1) * 2 * DH_A]
            wk_ref[s] = kb_t[:, tok]
            wv_ref[s] = vb_t[:, tok]


def _inproj(h, mod, norm_g, w_proj, layer, *, tm, kv_dtype, rope_tables=None, cache_seq=None, cache_acc=None):
    B, T, D = h.shape
    per_batch = mod.shape[0] > 1
    rope = rope_tables is not None
    row = lambda b, i: (b, i, 0)
    in_specs = [
        pl.BlockSpec((None, tm, D), row),
        pl.BlockSpec((None, 9, D), (lambda b, i: (b, 0, 0)) if per_batch else (lambda b, i: (0, 0, 0))),
        _resident(norm_g.shape),
        _resident_slice(w_proj.shape, (layer,)),
    ]
    args = [h, mod, norm_g, w_proj]
    if rope:
        in_specs += [pl.BlockSpec((tm, LANES), lambda b, i: (i, 0))] * 2
        args += list(rope_tables)
    widths = (N_QA, N_QA, N_QA, H_B * DH_B, N_KVB, N_KVB, C_CONV)
    dtypes = (BF16, kv_dtype, kv_dtype, BF16, kv_dtype, kv_dtype, F32)
    out_specs = [pl.BlockSpec((None, tm, w), row) for w in widths]
    out_shape = [jax.ShapeDtypeStruct((B, T, w), dt) for w, dt in zip(widths, dtypes)]
    aliases = {}
    n_aliased = 0
    if cache_seq is not None:
        n_all, seq_len = cache_seq
        n_seq = tm // seq_len
        assert B == 1 and n_seq * seq_len == tm
        cache_shapes = ((2 * H_A, DH_A, seq_len), (seq_len * H_A, 2 * DH_A), (N_KVB, seq_len), (N_KVB, seq_len))
        for shape in cache_shapes:
            nd = len(shape)
            if cache_acc is None:
                spec = pl.BlockSpec((n_seq, DEPTH) + shape, lambda b, i, nd=nd: (i, 0) + (0,) * nd)
            else:
                spec = pl.BlockSpec((n_seq, None) + shape, lambda b, i, nd=nd: (i, layer) + (0,) * nd)
            out_specs.append(spec)
            out_shape.append(jax.ShapeDtypeStruct((n_all, DEPTH) + shape, F32))
        if cache_acc is not None:
            n_aliased = len(cache_acc)
            aliases = {len(args) + j: len(widths) + j for j in range(n_aliased)}
            in_specs += [pl.BlockSpec(memory_space=pl.ANY)] * n_aliased
            args += list(cache_acc)
    return pl.pallas_call(
        functools.partial(_inproj_kernel, rope=rope, n_aliased=n_aliased, cache_out=cache_seq is not None,
                          cache_layer=layer),
        grid=(B, T // tm),
        in_specs=in_specs,
        out_specs=out_specs,
        out_shape=out_shape,
        input_output_aliases=aliases,
        compiler_params=_params("parallel", "parallel"),
        name="inproj",
    )(*args)


def _diff_attn_kernel(*refs, n_cache, lam_init, nq, hb):
    if n_cache:
        q_ref, k_ref, v_ref, ck_ref, cv_ref, lam_ref, g_ref, o_ref, s_even, s_odd, m_even, m_odd, vcache = refs
    else:
        q_ref, k_ref, v_ref, lam_ref, g_ref, o_ref, s_even, s_odd, m_even, m_odd = refs
    hd = 2 * DH_A
    t = pl.program_id(0)
    heads = [(hh, slice(hh * hd, (hh + 1) * hd)) for hh in range(hb)]

    @pl.when(t == 0)
    def _():
        s_odd[...] = jnp.zeros(s_odd.shape, F32)
        m_odd[...] = jnp.zeros(m_odd.shape, F32)
        if n_cache:
            vcache[...] = jnp.zeros(vcache.shape, BF16)

    if n_cache:
        @pl.when((t > 0) & ((t - 1) % nq == 0))
        def _():
            head = ((t - 1) // nq) % H_A
            vcache[...] = cv_ref[pl.ds(head, n_cache, stride=H_A), :].astype(BF16)

    def step(s_write, m_write, s_read, m_read):
        lam = lam_ref[...]
        lam_full = (jnp.exp(jnp.sum(lam[0:1] * lam[1:2], axis=1, keepdims=True))
                    - jnp.exp(jnp.sum(lam[2:3] * lam[3:4], axis=1, keepdims=True)) + lam_init)
        tq = q_ref.shape[0]
        lane = lax.broadcasted_iota(jnp.int32, (tq, hd), 1)
        zero = jnp.zeros((tq, hd), q_ref.dtype)
        for hh, lanes in heads:
            q = q_ref[:, lanes]
            q2 = jnp.concatenate([jnp.where(lane < DH_A, q, zero), jnp.where(lane >= DH_A, q, zero)], axis=0)
            row_max = jnp.concatenate([m_read[hh]] * (KEY_CHUNK // LANES), axis=1)
            acc = None
            run_max = None
            for lo in range(0, s_read.shape[2], KEY_CHUNK):
                cols = slice(lo, lo + KEY_CHUNK)
                if lo < n_cache:
                    s = _dot(q2, ck_ref[:, cols].astype(BF16))
                    v = vcache[cols, :]
                else:
                    own = slice(lo - n_cache, lo - n_cache + KEY_CHUNK)
                    s = _dot_nt(q2, k_ref[own, lanes].astype(BF16))
                    v = v_ref[own, lanes].astype(BF16)
                s_write[hh, :, cols] = s
                run_max = s if run_max is None else jnp.maximum(run_max, s)
                e = jnp.exp2(s_read[hh, :, cols] - row_max).astype(BF16)
                part = _dot(e, jnp.concatenate([v, jnp.ones(v.shape, BF16)], axis=1))
                acc = part if acc is None else acc + part
            m_write[hh] = jnp.broadcast_to(jnp.max(run_max, axis=1, keepdims=True), m_write.shape[1:])
            ratio = acc[:, :hd] / acc[:, hd:]
            o = ratio[:tq] - lam_full * ratio[tq:]
            o = _rms(o, SUBLN_EPS) * g_ref[...] * (1.0 - lam_init)
            o_ref[:, lanes] = o.astype(o_ref.dtype)

    pl.when(t % 2 == 0)(lambda: step(s_even, m_even, s_odd, m_odd))
    pl.when(t % 2 == 1)(lambda: step(s_odd, m_odd, s_even, m_even))


def _diff_attn(q, k, v, lam, subln_g, *, lam_init, tq, hb, cache=None):
    B, T, _ = q.shape
    n_cache = 0 if cache is None else cache[0].shape[-1]
    assert n_cache % KEY_CHUNK == 0 and (cache is None or hb == 1)
    hd = 2 * DH_A
    nq = T // tq
    n_groups = H_A // hb
    n_blocks = B * n_groups * nq

    def block_of(t):
        blk = jnp.clip(t, 0, n_blocks - 1)
        return blk // (n_groups * nq), (blk // nq) % n_groups, blk % nq

    def q_map(t):
        b, h, i = block_of(t)
        return b, i, h

    def kv_map(t):
        b, h, _ = block_of(t)
        return b, 0, h

    def out_map(t):
        return q_map(t - 1)

    in_specs = [
        pl.BlockSpec((None, tq, hb * hd), q_map),
        pl.BlockSpec((None, T, hb * hd), kv_map),
        pl.BlockSpec((None, T, hb * hd), lambda t: kv_map(t - 1)),
    ]
    args = [q, k, v]
    if cache is not None:
        ck, cv, layer = cache

        def ck_map(t):
            b, h, _ = block_of(t)
            return b, layer, h, 0, 0

        def cv_map(t):
            b, _, _ = block_of(t - 1)
            return b, layer, 0, 0

        in_specs += [pl.BlockSpec((None, None, None, hd, n_cache), ck_map),
                     pl.BlockSpec((None, None, n_cache * H_A, hd), cv_map)]
        args += [ck, cv]
    in_specs += [_resident(lam.shape), _resident(subln_g.shape)]
    args += [lam, subln_g]
    n_keys = n_cache + T
    return pl.pallas_call(
        functools.partial(_diff_attn_kernel, n_cache=n_cache, lam_init=lam_init, nq=nq, hb=hb),
        grid=(n_blocks + 1,),
        in_specs=in_specs,
        out_specs=pl.BlockSpec((None, tq, hb * hd), out_map),
        out_shape=jax.ShapeDtypeStruct((B, T, N_QA), BF16),
        scratch_shapes=[pltpu.VMEM((hb, 2 * tq, n_keys), F32), pltpu.VMEM((hb, 2 * tq, n_keys), F32),
                        pltpu.VMEM((hb, 2 * tq, LANES), F32), pltpu.VMEM((hb, 2 * tq, LANES), F32)]
        + ([pltpu.VMEM((n_cache, hd), BF16)] if n_cache else []),
        compiler_params=_params("arbitrary"),
        name="diff_attn",
    )(*args)


def _win_attn_kernel(*refs, local, has_ctx, seq_len, tq, nq, nb):
    if has_ctx:
        q_ref, k_ref, v_ref, ck_ref, cv_ref, sink_ref, o_ref = refs
        ck = ck_ref[...].astype(BF16)
        cv = cv_ref[...].astype(BF16)
        cv_ones = jnp.concatenate([cv, jnp.ones(cv.shape, BF16)], axis=0)
    else:
        q_ref, k_ref, v_ref, sink_ref, o_ref = refs
    n_loc = 3 * WIN_BLOCK
    lane = lax.broadcasted_iota(jnp.int32, (tq, LANES), 1)
    zero = jnp.zeros((tq, LANES), BF16)
    blocks = {}

    def block(bi, qi):
        if (bi, qi) in blocks:
            return blocks[bi, qi]
        blk = {"rows": slice(qi * tq, (qi + 1) * tq)}
        blk["q"] = q_ref[bi, blk["rows"], :]
        if local:
            n = pl.program_id(1) * nq + qi
            start = pl.multiple_of(jnp.clip((n - 1) * WIN_BLOCK, 0, seq_len - n_loc), WIN_BLOCK)
            blk["k"] = k_ref[bi, pl.ds(start, n_loc), :].astype(BF16)
            v = v_ref[bi, pl.ds(start, n_loc), :].astype(BF16)
            rows = lax.broadcasted_iota(jnp.int32, (tq, n_loc), 0)
            cols = lax.broadcasted_iota(jnp.int32, (tq, n_loc), 1)
            blk["valid"] = jnp.abs(cols - rows - (n * WIN_BLOCK - start)) <= WINDOW
        else:
            blk["k"] = k_ref[bi].astype(BF16)
            v = v_ref[bi].astype(BF16)
        blk["v_ones"] = jnp.concatenate([v, jnp.ones(v.shape, BF16)], axis=1)
        blocks[bi, qi] = blk
        return blk

    def scores(bi, qi, j):
        blk = block(bi, qi)
        sel = (lane >= j * DH_B) & (lane < (j + 1) * DH_B)
        qj = jnp.concatenate(
            [jnp.where(sel, blk["q"][:, g * LANES:(g + 1) * LANES], zero) for g in range(G_B)], axis=0)
        s = _dot_nt(qj, blk["k"])
        if has_ctx:
            s = jnp.concatenate([s, _dot(qj, ck)], axis=1)
        return s

    def softmax(bi, qi, j, s):
        es, sink_terms = [], []
        for g in range(G_B):
            sg = s[g * tq:(g + 1) * tq]
            if local:
                band = jnp.where(block(bi, qi)["valid"], sg[:, :n_loc], -jnp.inf)
                sg = jnp.concatenate([band, sg[:, n_loc:]], axis=1) if has_ctx else band
            sink = sink_ref[j * G_B + g] * LOG2E
            m = jnp.maximum(jnp.max(sg, axis=1, keepdims=True), sink)
            es.append(jnp.exp2(sg - m).astype(BF16))
            sink_terms.append(jnp.exp2(sink - m))
        return jnp.concatenate(es, axis=0), sink_terms

    def values(bi, qi, e):
        blk = block(bi, qi)
        n_own = blk["k"].shape[0]
        ov = _dot(e[:, :n_own], blk["v_ones"])
        if has_ctx:
            ov = ov + _dot_nt(e[:, n_own:], cv_ones)
        return ov

    def finish(bi, qi, halves):
        for g in range(G_B):
            rows_g = slice(g * tq, (g + 1) * tq)
            outs = [ov[rows_g, :LANES] / (ov[rows_g, LANES:] + sink_terms[g]) for ov, sink_terms in halves]
            og = jnp.where(lane < DH_B, outs[0], outs[1])
            o_ref[bi, block(bi, qi)["rows"], g * LANES:(g + 1) * LANES] = og.astype(o_ref.dtype)

    units = list(itertools.product(range(nb), range(nq), range(KV_B)))
    pending_s, pending_e, done = {}, {}, {}
    lag = WIN_SKEW
    for i in range(len(units) + 2 * lag):
        if i < len(units):
            pending_s[i] = scores(*units[i])
        if 0 <= i - lag < len(units):
            pending_e[i - lag] = softmax(*units[i - lag], pending_s.pop(i - lag))
        if 0 <= i - 2 * lag < len(units):
            bi, qi, j = units[i - 2 * lag]
            e, sink_terms = pending_e.pop(i - 2 * lag)
            done[bi, qi, j] = (values(bi, qi, e), sink_terms)
            if j == KV_B - 1:
                finish(bi, qi, [done.pop((bi, qi, jj)) for jj in range(KV_B)])


def _win_attn(q, k, v, sink, *, tq, nq, nb, local, cache=None):
    B, T, _ = q.shape
    has_ctx = cache is not None
    assert not local or tq == WIN_BLOCK
    assert not has_ctx or nb == 1
    rows = nq * tq
    in_specs = [
        pl.BlockSpec((nb, rows, G_B * LANES), lambda b, n: (b, n, 0)),
        pl.BlockSpec((nb, T, N_KVB), lambda b, n: (b, 0, 0)),
        pl.BlockSpec((nb, T, N_KVB), lambda b, n: (b, 0, 0)),
    ]
    args = [q, k, v]
    if has_ctx:
        ck, cv, layer = cache
        spec = pl.BlockSpec((None, None, N_KVB, ck.shape[3]), lambda b, n: (b, layer, 0, 0))
        in_specs += [spec, spec]
        args += [ck, cv]
    in_specs.append(pl.BlockSpec(memory_space=pltpu.SMEM))
    args.append(sink)
    return pl.pallas_call(
        functools.partial(_win_attn_kernel, local=local, has_ctx=has_ctx, seq_len=T, tq=tq, nq=nq, nb=nb),
        grid=(B // nb, T // rows),
        in_specs=in_specs,
        out_specs=pl.BlockSpec((nb, rows, G_B * LANES), lambda b, n: (b, n, 0)),
        out_shape=jax.ShapeDtypeStruct((B, T, H_B * DH_B), BF16),
        compiler_params=_params("parallel", "parallel"),
        name="win_attn",
    )(*args)


def _conv_kernel(prev_ref, cur_ref, next_ref, w_ref, bdw_ref, g_ref, b_ref, o_ref, win_ref, ph_ref):
    i = pl.program_id(1)
    rt = cur_ref.shape[0]
    zeros = jnp.zeros((CONV_HALO, C_CONV), F32)
    win_ref[0:CONV_HALO, :] = jnp.where(i > 0, prev_ref[...], zeros)
    win_ref[CONV_HALO:CONV_HALO + rt, :] = cur_ref[...]
    win_ref[CONV_HALO + rt:, :] = jnp.where(i < pl.num_programs(1) - 1, next_ref[...], zeros)
    n_ph = ph_ref.shape[1]
    for p in range(1, SUBLANES):
        ph_ref[p - 1] = win_ref[p:p + n_ph, :]
    first = CONV_HALO - CONV_K // 2
    tiles = CONV_ROWS // SUBLANES
    for r0 in range(0, rt, CONV_ROWS):
        acc = jnp.zeros((tiles, SUBLANES, C_CONV), F32) + bdw_ref[...]
        for t in range(CONV_K):
            p = (first + t) % SUBLANES
            base = r0 + first + t - p
            x = win_ref[base:base + CONV_ROWS, :] if p == 0 else ph_ref[p - 1, base:base + CONV_ROWS, :]
            acc = acc + x.reshape(tiles, SUBLANES, C_CONV) * w_ref[t]
        acc = acc.reshape(CONV_ROWS, C_CONV)
        mu = jnp.mean(acc, axis=-1, keepdims=True)
        cen = acc - mu
        var = jnp.mean(cen * cen, axis=-1, keepdims=True)
        y = cen * lax.rsqrt(var + NORM_EPS) * g_ref[...] + b_ref[...]
        o_ref[r0:r0 + CONV_ROWS, :] = _silu(y).astype(o_ref.dtype)


def _conv_module(u, w_dw, b_dw, g, b, *, rt):
    B, T, C = u.shape
    hb = rt // CONV_HALO
    last = T // CONV_HALO - 1
    n_ph = rt + (CONV_HALO + CONV_K // 2) // SUBLANES * SUBLANES
    w_tiles = jnp.broadcast_to(w_dw[:, None, :], (CONV_K, SUBLANES, C))
    in_specs = [
        pl.BlockSpec((None, CONV_HALO, C), lambda bb, i: (bb, jnp.maximum(i * hb - 1, 0), 0)),
        pl.BlockSpec((None, rt, C), lambda bb, i: (bb, i, 0)),
        pl.BlockSpec((None, CONV_HALO, C), lambda bb, i: (bb, jnp.minimum((i + 1) * hb, last), 0)),
        _resident(w_tiles.shape), _resident(b_dw.shape), _resident(g.shape), _resident(b.shape),
    ]
    return pl.pallas_call(
        _conv_kernel,
        grid=(B, T // rt),
        in_specs=in_specs,
        out_specs=pl.BlockSpec((None, rt, C), lambda bb, i: (bb, i, 0)),
        out_shape=jax.ShapeDtypeStruct((B, T, C), BF16),
        scratch_shapes=[pltpu.VMEM((rt + 2 * CONV_HALO, C), F32),
                        pltpu.VMEM((SUBLANES - 1, n_ph, C), F32)],
        compiler_params=_params("parallel", "parallel"),
        name="conv_module",
    )(u, u, u, w_tiles, b_dw, g, b)


def _merge_kernel(h_ref, mod_ref, g_ref, ya_ref, yb_ref, yc_ref, wg_ref, wb_ref, wo_ref, o_ref):
    h = h_ref[...]
    nb = _mod_rms(h, g_ref[1:2, :], mod_ref[3:4, :], mod_ref[4:5, :]).astype(BF16)
    merged = None
    for i, y_ref in enumerate((ya_ref, yb_ref, yc_ref)):
        gate = jax.nn.sigmoid(_dot(nb, wg_ref[:, i * D_MODEL:(i + 1) * D_MODEL]))
        term = gate * _dot(y_ref[...], wb_ref[i])
        merged = term if merged is None else merged + term
    o_ref[...] = h + mod_ref[5:6, :] * _dot(merged.astype(BF16), wo_ref[...])


def _merge(h, mod, norm_g, w_gate, ya, yb, yc, w_branch, w_out, layer, *, tm):
    B, T, D = h.shape
    per_batch = mod.shape[0] > 1
    row = lambda b, i: (b, i, 0)
    y_spec = pl.BlockSpec((None, tm, BRANCH_W), row)
    return pl.pallas_call(
        _merge_kernel,
        grid=(B, T // tm),
        in_specs=[
            pl.BlockSpec((None, tm, D), row),
            pl.BlockSpec((None, 9, D), (lambda b, i: (b, 0, 0)) if per_batch else (lambda b, i: (0, 0, 0))),
            _resident(norm_g.shape), y_spec, y_spec, y_spec,
            _resident_slice(w_gate.shape, (layer,)), _resident_slice(w_branch.shape, (layer,)),
            _resident_slice(w_out.shape, (layer,)),
        ],
        out_specs=pl.BlockSpec((None, tm, D), row),
        out_shape=jax.ShapeDtypeStruct((B, T, D), F32),
        compiler_params=_params("parallel", "parallel"),
        name="merge",
    )(h, mod, norm_g, ya, yb, yc, w_gate, w_branch, w_out)


def _rope_tables(n_tokens):
    quarter = DH_A // 4
    t = np.arange(n_tokens)
    inv = (ROPE_BASE ** (-np.arange(0, DH_A // 2, 2, dtype=np.float32) / (DH_A // 2))).astype(np.float32)
    pos = np.stack([(t // GRID_W).astype(np.float32), (t % GRID_W).astype(np.float32)], axis=1)
    ang = (pos[:, :, None] * inv[None, None, :]).astype(np.float32)
    ang = np.broadcast_to(ang[:, :, None, :], (n_tokens, 2, 2, quarter)).reshape(n_tokens, DH_A)
    sign = np.where((np.arange(DH_A) % (2 * quarter)) < quarter, -1.0, 1.0).astype(np.float32)
    cos = np.cos(ang).astype(np.float32)
    sin_signed = (np.sin(ang) * sign).astype(np.float32)
    reps = LANES // DH_A
    return jnp.asarray(np.tile(cos, (1, reps))), jnp.asarray(np.tile(sin_signed, (1, reps)))


def _group_major(w, axis):
    shape = w.shape
    split = shape[:axis] + (KV_B, G_B, DH_B) + shape[axis + 1:]
    return jnp.swapaxes(w.reshape(split), axis, axis + 1).reshape(shape)


def _prepare_weights(ffn_w_in, ffn_w_out, w_in, w_branch, w_out):
    qb_lo = 3 * N_QA
    qb_hi = qb_lo + H_B * DH_B
    w_proj = jnp.concatenate(
        [w_in[:, :, :qb_lo], _group_major(w_in[:, :, qb_lo:qb_hi], 2), w_in[:, :, qb_hi:N_PROJ]],
        axis=2).astype(BF16)
    w_gate = w_in[:, :, N_PROJ:].astype(BF16)
    wb = jnp.concatenate(
        [w_branch[:, 0:1], _group_major(w_branch[:, 1:2], 2), w_branch[:, 2:3]], axis=1).astype(BF16)
    return ffn_w_in.astype(BF16), ffn_w_out.astype(BF16), w_proj, w_gate, wb, w_out.astype(BF16)


def kernel(x_prompt, x_sample, cache_diff_k, cache_diff_v, cache_win_k, cache_win_v, c, c_ctx,
           w_mod, b_mod, norm_g, ffn_w_in, ffn_w_out, w_in, diff_lambda, diff_subln_g, win_sink,
           conv_dw_w, conv_dw_b, conv_norm_g, conv_norm_b, w_branch, w_out, final_norm_g):
    n_ctx_b, n_ctx_t, D = x_prompt.shape
    n_lat_b, n_lat_t, _ = x_sample.shape
    past = cache_diff_k.shape[2]
    tm = 512

    n_rows = 1 + n_lat_b
    pad_rows = -n_rows % 8
    cond = jnp.concatenate([c_ctx[None, :], c, jnp.zeros((pad_rows, D), F32)], axis=0)
    mod_all = _modulation(cond, w_mod, b_mod.reshape(DEPTH, 1, 9 * D))

    rope_tables = _rope_tables(n_lat_t)
    ck_a = jnp.transpose(cache_diff_k, (0, 1, 3, 4, 5, 2)).reshape(n_lat_b, DEPTH, H_A, 2 * DH_A, past)
    cv_a = cache_diff_v.reshape(n_lat_b, DEPTH, past * H_A, 2 * DH_A)
    ck_b = jnp.transpose(cache_win_k, (0, 1, 3, 4, 2)).reshape(n_lat_b, DEPTH, N_KVB, past)
    cv_b = jnp.transpose(cache_win_v, (0, 1, 3, 4, 2)).reshape(n_lat_b, DEPTH, N_KVB, past)
    final_g = final_norm_g.reshape(1, D)

    xp = x_prompt.reshape(1, n_ctx_b * n_ctx_t, D)
    xs = x_sample
    new_cache = None
    ffn_in, ffn_out, w_proj, w_gate, wb, wo = _prepare_weights(ffn_w_in, ffn_w_out, w_in, w_branch, w_out)
    for l in range(DEPTH):
        lam_init = 0.8 - 0.6 * math.exp(-0.3 * l)
        g_l = norm_g[l]
        lam = diff_lambda[l]
        subln = diff_subln_g[l].reshape(1, 2 * DH_A)
        sink = win_sink[l]
        conv_p = (conv_dw_w[l], conv_dw_b[l].reshape(1, C_CONV),
                  conv_norm_g[l].reshape(1, C_CONV), conv_norm_b[l].reshape(1, C_CONV))
        mod_ctx = mod_all[l, 0:1].reshape(1, 9, D)
        mod_lat = mod_all[l, 1:n_rows].reshape(n_lat_b, 9, D)
        last = l == DEPTH - 1

        h = _ffn(xp, mod_ctx, g_l, ffn_in, ffn_out, (l, 0), mod_row=0, g_row=0, tm=tm)
        qa, ka, va, qb, kb, vb, u, *new_cache = _inproj(
            h, mod_ctx, g_l, w_proj, l, tm=tm, kv_dtype=F32,
            cache_seq=(n_ctx_b, n_ctx_t), cache_acc=new_cache)
        seq = lambda a: a.reshape(n_ctx_b, n_ctx_t, a.shape[-1])
        ka, va, kb, vb = seq(ka), seq(va), seq(kb), seq(vb)
        ya = _diff_attn(seq(qa), ka, va, lam, subln, lam_init=lam_init, tq=n_ctx_t, hb=H_A)
        yb = _win_attn(seq(qb), kb, vb, sink, tq=n_ctx_t, nq=1, nb=4, local=False)
        yc = _conv_module(seq(u), *conv_p, rt=n_ctx_t)
        flat = lambda a: a.reshape(1, n_ctx_b * n_ctx_t, a.shape[-1])
        h = _merge(h, mod_ctx, g_l, w_gate, flat(ya), flat(yb), flat(yc), wb, wo, l, tm=tm)
        xp = _ffn(h, mod_ctx, g_l, ffn_in, ffn_out, (l, 1), mod_row=6, g_row=2, tm=tm,
                  final_g=final_g if last else None)

        h = _ffn(xs, mod_lat, g_l, ffn_in, ffn_out, (l, 0), mod_row=0, g_row=0, tm=tm)
        qa, ka, va, qb, kb, vb, u = _inproj(h, mod_lat, g_l, w_proj, l, tm=tm, kv_dtype=BF16,
                                            rope_tables=rope_tables)
        ya = _diff_attn(qa, ka, va, lam, subln, lam_init=lam_init, tq=1024, hb=1, cache=(ck_a, cv_a, l))
        yb = _win_attn(qb, kb, vb, sink, tq=WIN_BLOCK, nq=16, nb=1, local=True, cache=(ck_b, cv_b, l))
        yc = _conv_module(u, *conv_p, rt=1024)
        h = _merge(h, mod_lat, g_l, w_gate, ya, yb, yc, wb, wo, l, tm=tm)
        xs = _ffn(h, mod_lat, g_l, ffn_in, ffn_out, (l, 1), mod_row=6, g_row=2, tm=tm,
                  final_g=final_g if last else None)

    y_prompt = xp.reshape(n_ctx_b, n_ctx_t, D)
    dk, dv, wk, wv = new_cache
    new_diff_k = jnp.transpose(dk.reshape(n_ctx_b, DEPTH, H_A, 2, DH_A, n_ctx_t), (0, 1, 5, 2, 3, 4))
    new_diff_v = dv.reshape(n_ctx_b, DEPTH, n_ctx_t, H_A, 2 * DH_A)
    new_win_k = jnp.transpose(wk.reshape(n_ctx_b, DEPTH, KV_B, DH_B, n_ctx_t), (0, 1, 4, 2, 3))
    new_win_v = jnp.transpose(wv.reshape(n_ctx_b, DEPTH, KV_B, DH_B, n_ctx_t), (0, 1, 4, 2, 3))
    return (y_prompt, xs, new_diff_k, new_diff_v, new_win_k, new_win_v)
```

```python
import functools
import itertools
import math

import numpy as np
import jax
import jax.numpy as jnp
from jax import lax
from jax.experimental import pallas as pl
from jax.experimental.pallas import tpu as pltpu

D_MODEL = 1024
DEPTH = 2
GRID_W = 64
H_A = 4
DH_A = 64
H_B = 8
KV_B = 2
G_B = H_B // KV_B
DH_B = 64
WINDOW = 128
WIN_BLOCK = 128
C_CONV = 512
CONV_K = 31
BRANCH_W = 512
D_FF = 2816
ROPE_BASE = 10000.0
NORM_EPS = 1e-6
SUBLN_EPS = 1e-5
SCALE_A = DH_A ** -0.5
SCALE_B = DH_B ** -0.5
LOG2E = math.log2(math.e)

N_QA = H_A * 2 * DH_A
N_KVB = KV_B * DH_B
N_ATTN = 3 * N_QA + H_B * DH_B + 2 * N_KVB
N_PROJ = N_ATTN + 2 * C_CONV

LANES = 128
SUBLANES = 8
ROPE_HALF = DH_A // 4
FF_CHUNK = 256
KEY_CHUNK = 256
WIN_SKEW = 1
CONV_HALO = 16
CONV_ROWS = 32
VMEM_LIMIT = 56 * 1024 * 1024

BF16 = jnp.bfloat16
F32 = jnp.float32


def _params(*sem):
    return pltpu.CompilerParams(dimension_semantics=sem, vmem_limit_bytes=VMEM_LIMIT)


def _resident(shape):
    nd = len(shape)
    return pl.BlockSpec(shape, lambda *_: (0,) * nd, pipeline_mode=pl.Buffered(1))


def _resident_slice(shape, lead):
    n = len(lead)
    block = (None,) * n + tuple(shape[n:])
    index = tuple(lead) + (0,) * (len(shape) - n)
    return pl.BlockSpec(block, lambda *_: index, pipeline_mode=pl.Buffered(1))


def _dot(a, b):
    return jnp.dot(a, b, preferred_element_type=F32)


def _dot_nt(a, b):
    return lax.dot_general(a, b, (((1,), (1,)), ((), ())), preferred_element_type=F32)


def _rms(x, eps):
    return x * lax.rsqrt(jnp.mean(x * x, axis=-1, keepdims=True) + eps)


def _mod_rms(x, g, shift, scale):
    return _rms(x, NORM_EPS) * g * (1.0 + scale) + shift


def _silu(x):
    return x * jax.nn.sigmoid(x)


def _mod_kernel(c_ref, w_ref, b_ref, o_ref):
    a = _silu(c_ref[...]).astype(BF16)
    o_ref[...] = _dot(a, w_ref[...].astype(BF16)) + b_ref[...]


def _modulation(cond, w_mod, b_mod):
    R = cond.shape[0]
    tn = 9 * D_MODEL // 4
    return pl.pallas_call(
        _mod_kernel,
        grid=(DEPTH, 9 * D_MODEL // tn),
        in_specs=[
            pl.BlockSpec((R, D_MODEL), lambda l, j: (0, 0)),
            pl.BlockSpec((None, D_MODEL, tn), lambda l, j: (l, 0, j)),
            pl.BlockSpec((None, 1, tn), lambda l, j: (l, 0, j)),
        ],
        out_specs=pl.BlockSpec((None, R, tn), lambda l, j: (l, 0, j)),
        out_shape=jax.ShapeDtypeStruct((DEPTH, R, 9 * D_MODEL), F32),
        compiler_params=_params("parallel", "parallel"),
        name="modulation",
    )(cond, w_mod, b_mod)


def _ffn_kernel(x_ref, mod_ref, g_ref, win_ref, wout_ref, *rest, mod_row, g_row, final):
    if final:
        fg_ref, o_ref = rest
    else:
        (o_ref,) = rest
    x = x_ref[...]
    shift = mod_ref[mod_row:mod_row + 1, :]
    scale = mod_ref[mod_row + 1:mod_row + 2, :]
    gate = mod_ref[mod_row + 2:mod_row + 3, :]
    nb = _mod_rms(x, g_ref[g_row:g_row + 1, :], shift, scale).astype(BF16)
    acc = None
    for lo in range(0, D_FF, FF_CHUNK):
        a = _dot(nb, win_ref[:, lo:lo + FF_CHUNK])
        b = _dot(nb, win_ref[:, D_FF + lo:D_FF + lo + FF_CHUNK])
        hid = (_silu(a) * b).astype(BF16)
        part = _dot(hid, wout_ref[lo:lo + FF_CHUNK, :])
        acc = part if acc is None else acc + part
    h = x + 0.5 * gate * acc
    if final:
        h = _rms(h, NORM_EPS) * fg_ref[...]
    o_ref[...] = h


def _ffn(x, mod, norm_g, w_in, w_out, which, *, mod_row, g_row, tm, final_g=None):
    B, T, D = x.shape
    per_batch = mod.shape[0] > 1
    final = final_g is not None
    l, s = which
    in_specs = [
        pl.BlockSpec((None, tm, D), lambda b, i: (b, i, 0)),
        pl.BlockSpec((None, 9, D), (lambda b, i: (b, 0, 0)) if per_batch else (lambda b, i: (0, 0, 0))),
        _resident(norm_g.shape),
    ]
    in_specs += [_resident_slice(w_in.shape, (l, s)), _resident_slice(w_out.shape, (l, s))]
    args = [x, mod, norm_g, w_in, w_out]
    if final:
        in_specs.append(_resident(final_g.shape))
        args.append(final_g)
    return pl.pallas_call(
        functools.partial(_ffn_kernel, mod_row=mod_row, g_row=g_row, final=final),
        grid=(B, T // tm),
        in_specs=in_specs,
        out_specs=pl.BlockSpec((None, tm, D), lambda b, i: (b, i, 0)),
        out_shape=jax.ShapeDtypeStruct((B, T, D), F32),
        compiler_params=_params("parallel", "parallel"),
        name="ffn",
    )(*args)


def _rope(x, cos, sin_signed, lo_half):
    outs = []
    for j in range(x.shape[1] // LANES):
        xj = x[:, j * LANES:(j + 1) * LANES]
        rot = jnp.where(lo_half, pltpu.roll(xj, LANES - ROPE_HALF, 1), pltpu.roll(xj, ROPE_HALF, 1))
        outs.append(xj * cos + rot * sin_signed)
    return outs[0] if len(outs) == 1 else jnp.concatenate(outs, axis=1)


def _inproj_kernel(x_ref, mod_ref, g_ref, w_ref, *rest, rope, n_aliased, cache_out, cache_layer):
    if rope:
        cos_ref, sin_ref = rest[:2]
        rest = rest[2:]
    rest = rest[n_aliased:]
    qa_ref, ka_ref, va_ref, qb_ref, kb_ref, vb_ref, u_ref = rest[:7]
    x = x_ref[...]
    nb = _mod_rms(x, g_ref[1:2, :], mod_ref[3:4, :], mod_ref[4:5, :]).astype(BF16)

    def proj(lo, hi):
        return _dot(nb, w_ref[:, lo:hi])

    qa = proj(0, N_QA)
    ka = proj(N_QA, 2 * N_QA)
    va = proj(2 * N_QA, 3 * N_QA)
    qb = proj(3 * N_QA, 4 * N_QA)
    kvb = proj(4 * N_QA, N_ATTN)
    kb = kvb[:, :N_KVB]
    vb = kvb[:, N_KVB:]
    cu = proj(N_ATTN, N_PROJ)
    if rope:
        cos = cos_ref[...]
        sin_signed = sin_ref[...]
        lane = lax.broadcasted_iota(jnp.int32, cos.shape, 1)
        lo_half = (lane % (2 * ROPE_HALF)) < ROPE_HALF
        qa = _rope(qa, cos, sin_signed, lo_half)
        ka = _rope(ka, cos, sin_signed, lo_half)
        qb = _rope(qb, cos, sin_signed, lo_half)
        kb = _rope(kb, cos, sin_signed, lo_half)
    qa_ref[...] = (qa * (SCALE_A * LOG2E)).astype(qa_ref.dtype)
    ka_ref[...] = ka.astype(ka_ref.dtype)
    va_ref[...] = va.astype(va_ref.dtype)
    qb_ref[...] = (qb * (SCALE_B * LOG2E)).astype(qb_ref.dtype)
    kb_ref[...] = kb.astype(kb_ref.dtype)
    vb_ref[...] = vb.astype(vb_ref.dtype)
    u_ref[...] = cu[:, :C_CONV] * jax.nn.sigmoid(cu[:, C_CONV:])
    if cache_out:
        cache_refs = rest[7:]
        if n_aliased == 0:
            for ref in cache_refs:
                for other in range(ref.shape[1]):
                    if other != cache_layer:
                        ref[:, other] = jnp.zeros(ref.shape[:1] + ref.shape[2:], F32)
            cache_refs = [ref.at[:, cache_layer] for ref in cache_refs]
        dk_ref, dv_ref, wk_ref, wv_ref = cache_refs
        n_seq, _, _, seq_len = dk_ref.shape
        ka_t, kb_t, vb_t = ka.T, kb.T, vb.T
        for s in range(n_seq):
            tok = slice(s * seq_len, (s + 1) * seq_len)
            for hc in range(2 * H_A):
                dk_ref[s, hc] = ka_t[hc * DH_A:(hc + 1) * DH_A, tok]
            for hh in range(H_A):
                dv_ref[s, pl.ds(hh, seq_len, stride=H_A), :] = va[tok, hh * 2 * DH_A:(hh + 1) * 2 * DH_A]
            wk_ref[s] = kb_t[:, tok]
            wv_ref[s] = vb_t[:, tok]


def _inproj(h, mod, norm_g, w_proj, layer, *, tm, kv_dtype, rope_tables=None, cache_seq=None, cache_acc=None):
    B, T, D = h.shape
    per_batch = mod.shape[0] > 1
    rope = rope_tables is not None
    row = lambda b, i: (b, i, 0)
    in_specs = [
        pl.BlockSpec((None, tm, D), row),
        pl.BlockSpec((None, 9, D), (lambda b, i: (b, 0, 0)) if per_batch else (lambda b, i: (0, 0, 0))),
        _resident(norm_g.shape),
        _resident_slice(w_proj.shape, (layer,)),
    ]
    args = [h, mod, norm_g, w_proj]
    if rope:
        in_specs += [pl.BlockSpec((tm, LANES), lambda b, i: (i, 0))] * 2
        args += list(rope_tables)
    widths = (N_QA, N_QA, N_QA, H_B * DH_B, N_KVB, N_KVB, C_CONV)
    dtypes = (BF16, kv_dtype, kv_dtype, BF16, kv_dtype, kv_dtype, F32)
    out_specs = [pl.BlockSpec((None, tm, w), row) for w in widths]
    out_shape = [jax.ShapeDtypeStruct((B, T, w), dt) for w, dt in zip(widths, dtypes)]
    aliases = {}
    n_aliased = 0
    if cache_seq is not None:
        n_all, seq_len = cache_seq
        n_seq = tm // seq_len
        assert B == 1 and n_seq * seq_len == tm
        cache_shapes = ((2 * H_A, DH_A, seq_len), (seq_len * H_A, 2 * DH_A), (N_KVB, seq_len), (N_KVB, seq_len))
        for shape in cache_shapes:
            nd = len(shape)
            if cache_acc is None:
                spec = pl.BlockSpec((n_seq, DEPTH) + shape, lambda b, i, nd=nd: (i, 0) + (0,) * nd)
            else:
                spec = pl.BlockSpec((n_seq, None) + shape, lambda b, i, nd=nd: (i, layer) + (0,) * nd)
            out_specs.append(spec)
            out_shape.append(jax.ShapeDtypeStruct((n_all, DEPTH) + shape, F32))
        if cache_acc is not None:
            n_aliased = len(cache_acc)
            aliases = {len(args) + j: len(widths) + j for j in range(n_aliased)}
            in_specs += [pl.BlockSpec(memory_space=pl.ANY)] * n_aliased
            args += list(cache_acc)
    return pl.pallas_call(
        functools.partial(_inproj_kernel, rope=rope, n_aliased=n_aliased, cache_out=cache_seq is not None,
                          cache_layer=layer),
        grid=(B, T // tm),
        in_specs=in_specs,
        out_specs=out_specs,
        out_shape=out_shape,
        input_output_aliases=aliases,
        compiler_params=_params("parallel", "parallel"),
        name="inproj",
    )(*args)


def _diff_attn_kernel(*refs, n_cache, lam_init, nq, hb):
    if n_cache:
        q_ref, k_ref, v_ref, ck_ref, cv_ref, lam_ref, g_ref, o_ref, s_even, s_odd, m_even, m_odd, vcache = refs
    else:
        q_ref, k_ref, v_ref, lam_ref, g_ref, o_ref, s_even, s_odd, m_even, m_odd = refs
    hd = 2 * DH_A
    t = pl.program_id(0)
    heads = [(hh, slice(hh * hd, (hh + 1) * hd)) for hh in range(hb)]

    @pl.when(t == 0)
    def _():
        s_odd[...] = jnp.zeros(s_odd.shape, F32)
        m_odd[...] = jnp.zeros(m_odd.shape, F32)
        if n_cache:
            vcache[...] = jnp.zeros(vcache.shape, BF16)

    if n_cache:
        @pl.when((t > 0) & ((t - 1) % nq == 0))
        def _():
            head = ((t - 1) // nq) % H_A
            vcache[...] = cv_ref[pl.ds(head, n_cache, stride=H_A), :].astype(BF16)

    def step(s_write, m_write, s_read, m_read):
        lam = lam_ref[...]
        lam_full = (jnp.exp(jnp.sum(lam[0:1] * lam[1:2], axis=1, keepdims=True))
                    - jnp.exp(jnp.sum(lam[2:3] * lam[3:4], axis=1, keepdims=True)) + lam_init)
        tq = q_ref.shape[0]
        lane = lax.broadcasted_iota(jnp.int32, (tq, hd), 1)
        zero = jnp.zeros((tq, hd), q_ref.dtype)
        for hh, lanes in heads:
            q = q_ref[:, lanes]
            q2 = jnp.concatenate([jnp.where(lane < DH_A, q, zero), jnp.where(lane >= DH_A, q, zero)], axis=0)
            row_max = jnp.concatenate([m_read[hh]] * (KEY_CHUNK // LANES), axis=1)
            acc = None
            run_max = None
            for lo in range(0, s_read.shape[2], KEY_CHUNK):
                cols = slice(lo, lo + KEY_CHUNK)
                if lo < n_cache:
                    s = _dot(q2, ck_ref[:, cols].astype(BF16))
                    v = vcache[cols, :]
                else:
                    own = slice(lo - n_cache, lo - n_cache + KEY_CHUNK)
                    s = _dot_nt(q2, k_ref[own, lanes].astype(BF16))
                    v = v_ref[own, lanes].astype(BF16)
                s_write[hh, :, cols] = s
                run_max = s if run_max is None else jnp.maximum(run_max, s)
                e = jnp.exp2(s_read[hh, :, cols] - row_max).astype(BF16)
                part = _dot(e, jnp.concatenate([v, jnp.ones(v.shape, BF16)], axis=1))
                acc = part if acc is None else acc + part
            m_write[hh] = jnp.broadcast_to(jnp.max(run_max, axis=1, keepdims=True), m_write.shape[1:])
            ratio = acc[:, :hd] / acc[:, hd:]
            o = ratio[:tq] - lam_full * ratio[tq:]
            o = _rms(o, SUBLN_EPS) * g_ref[...] * (1.0 - lam_init)
            o_ref[:, lanes] = o.astype(o_ref.dtype)

    pl.when(t % 2 == 0)(lambda: step(s_even, m_even, s_odd, m_odd))
    pl.when(t % 2 == 1)(lambda: step(s_odd, m_odd, s_even, m_even))


def _diff_attn(q, k, v, lam, subln_g, *, lam_init, tq, hb, cache=None):
    B, T, _ = q.shape
    n_cache = 0 if cache is None else cache[0].shape[-1]
    assert n_cache % KEY_CHUNK == 0 and (cache is None or hb == 1)
    hd = 2 * DH_A
    nq = T // tq
    n_groups = H_A // hb
    n_blocks = B * n_groups * nq

    def block_of(t):
        blk = jnp.clip(t, 0, n_blocks - 1)
        return blk // (n_groups * nq), (blk // nq) % n_groups, blk % nq

    def q_map(t):
        b, h, i = block_of(t)
        return b, i, h

    def kv_map(t):
        b, h, _ = block_of(t)
        return b, 0, h

    def out_map(t):
        return q_map(t - 1)

    in_specs = [
        pl.BlockSpec((None, tq, hb * hd), q_map),
        pl.BlockSpec((None, T, hb * hd), kv_map),
        pl.BlockSpec((None, T, hb * hd), lambda t: kv_map(t - 1)),
    ]
    args = [q, k, v]
    if cache is not None:
        ck, cv, layer = cache

        def ck_map(t):
            b, h, _ = block_of(t)
            return b, layer, h, 0, 0

        def cv_map(t):
            b, _, _ = block_of(t - 1)
            return b, layer, 0, 0

        in_specs += [pl.BlockSpec((None, None, None, hd, n_cache), ck_map),
                     pl.BlockSpec((None, None, n_cache * H_A, hd), cv_map)]
        args += [ck, cv]
    in_specs += [_resident(lam.shape), _resident(subln_g.shape)]
    args += [lam, subln_g]
    n_keys = n_cache + T
    return pl.pallas_call(
        functools.partial(_diff_attn_kernel, n_cache=n_cache, lam_init=lam_init, nq=nq, hb=hb),
        grid=(n_blocks + 1,),
        in_specs=in_specs,
        out_specs=pl.BlockSpec((None, tq, hb * hd), out_map),
        out_shape=jax.ShapeDtypeStruct((B, T, N_QA), BF16),
        scratch_shapes=[pltpu.VMEM((hb, 2 * tq, n_keys), F32), pltpu.VMEM((hb, 2 * tq, n_keys), F32),
                        pltpu.VMEM((hb, 2 * tq, LANES), F32), pltpu.VMEM((hb, 2 * tq, LANES), F32)]
        + ([pltpu.VMEM((n_cache, hd), BF16)] if n_cache else []),
        compiler_params=_params("arbitrary"),
        name="diff_attn",
    )(*args)


def _win_attn_kernel(*refs, local, has_ctx, seq_len, tq, nq, nb):
    if has_ctx:
        q_ref, k_ref, v_ref, ck_ref, cv_ref, sink_ref, o_ref = refs
        ck = ck_ref[...].astype(BF16)
        cv = cv_ref[...].astype(BF16)
        cv_ones = jnp.concatenate([cv, jnp.ones(cv.shape, BF16)], axis=0)
    else:
        q_ref, k_ref, v_ref, sink_ref, o_ref = refs
    n_loc = 3 * WIN_BLOCK
    lane = lax.broadcasted_iota(jnp.int32, (tq, LANES), 1)
    zero = jnp.zeros((tq, LANES), BF16)
    blocks = {}

    def block(bi, qi):
        if (bi, qi) in blocks:
            return blocks[bi, qi]
        blk = {"rows": slice(qi * tq, (qi + 1) * tq)}
        blk["q"] = q_ref[bi, blk["rows"], :]
        if local:
            n = pl.program_id(1) * nq + qi
            start = pl.multiple_of(jnp.clip((n - 1) * WIN_BLOCK, 0, seq_len - n_loc), WIN_BLOCK)
            blk["k"] = k_ref[bi, pl.ds(start, n_loc), :].astype(BF16)
            v = v_ref[bi, pl.ds(start, n_loc), :].astype(BF16)
            rows = lax.broadcasted_iota(jnp.int32, (tq, n_loc), 0)
            cols = lax.broadcasted_iota(jnp.int32, (tq, n_loc), 1)
            blk["valid"] = jnp.abs(cols - rows - (n * WIN_BLOCK - start)) <= WINDOW
        else:
            blk["k"] = k_ref[bi].astype(BF16)
            v = v_ref[bi].astype(BF16)
        blk["v_ones"] = jnp.concatenate([v, jnp.ones(v.shape, BF16)], axis=1)
        blocks[bi, qi] = blk
        return blk

    def scores(bi, qi, j):
        blk = block(bi, qi)
        sel = (lane >= j * DH_B) & (lane < (j + 1) * DH_B)
        qj = jnp.concatenate(
            [jnp.where(sel, blk["q"][:, g * LANES:(g + 1) * LANES], zero) for g in range(G_B)], axis=0)
        s = _dot_nt(qj, blk["k"])
        if has_ctx:
            s = jnp.concatenate([s, _dot(qj, ck)], axis=1)
        return s

    def softmax(bi, qi, j, s):
        es, sink_terms = [], []
        for g in range(G_B):
            sg = s[g * tq:(g + 1) * tq]
            if local:
                band = jnp.where(block(bi, qi)["valid"], sg[:, :n_loc], -jnp.inf)
                sg = jnp.concatenate([band, sg[:, n_loc:]], axis=1) if has_ctx else band
            sink = sink_ref[j * G_B + g] * LOG2E
            m = jnp.maximum(jnp.max(sg, axis=1, keepdims=True), sink)
            es.append(jnp.exp2(sg - m).astype(BF16))
            sink_terms.append(jnp.exp2(sink - m))
        return jnp.concatenate(es, axis=0), sink_terms

    def values(bi, qi, e):
        blk = block(bi, qi)
        n_own = blk["k"].shape[0]
        ov = _dot(e[:, :n_own], blk["v_ones"])
        if has_ctx:
            ov = ov + _dot_nt(e[:, n_own:], cv_ones)
        return ov

    def finish(bi, qi, halves):
        for g in range(G_B):
            rows_g = slice(g * tq, (g + 1) * tq)
            outs = [ov[rows_g, :LANES] / (ov[rows_g, LANES:] + sink_terms[g]) for ov, sink_terms in halves]
            og = jnp.where(lane < DH_B, outs[0], outs[1])
            o_ref[bi, block(bi, qi)["rows"], g * LANES:(g + 1) * LANES] = og.astype(o_ref.dtype)

    units = list(itertools.product(range(nb), range(nq), range(KV_B)))
    pending_s, pending_e, done = {}, {}, {}
    lag = WIN_SKEW
    for i in range(len(units) + 2 * lag):
        if i < len(units):
            pending_s[i] = scores(*units[i])
        if 0 <= i - lag < len(units):
            pending_e[i - lag] = softmax(*units[i - lag], pending_s.pop(i - lag))
        if 0 <= i - 2 * lag < len(units):
            bi, qi, j = units[i - 2 * lag]
            e, sink_terms = pending_e.pop(i - 2 * lag)
            done[bi, qi, j] = (values(bi, qi, e), sink_terms)
            if j == KV_B - 1:
                finish(bi, qi, [done.pop((bi, qi, jj)) for jj in range(KV_B)])


def _win_attn(q, k, v, sink, *, tq, nq, nb, local, cache=None):
    B, T, _ = q.shape
    has_ctx = cache is not None
    assert not local or tq == WIN_BLOCK
    assert not has_ctx or nb == 1
    rows = nq * tq
    in_specs = [
        pl.BlockSpec((nb, rows, G_B * LANES), lambda b, n: (b, n, 0)),
        pl.BlockSpec((nb, T, N_KVB), lambda b, n: (b, 0, 0)),
        pl.BlockSpec((nb, T, N_KVB), lambda b, n: (b, 0, 0)),
    ]
    args = [q, k, v]
    if has_ctx:
        ck, cv, layer = cache
        spec = pl.BlockSpec((None, None, N_KVB, ck.shape[3]), lambda b, n: (b, layer, 0, 0))
        in_specs += [spec, spec]
        args += [ck, cv]
    in_specs.append(pl.BlockSpec(memory_space=pltpu.SMEM))
    args.append(sink)
    return pl.pallas_call(
        functools.partial(_win_attn_kernel, local=local, has_ctx=has_ctx, seq_len=T, tq=tq, nq=nq, nb=nb),
        grid=(B // nb, T // rows),
        in_specs=in_specs,
        out_specs=pl.BlockSpec((nb, rows, G_B * LANES), lambda b, n: (b, n, 0)),
        out_shape=jax.ShapeDtypeStruct((B, T, H_B * DH_B), BF16),
        compiler_params=_params("parallel", "parallel"),
        name="win_attn",
    )(*args)


def _conv_kernel(prev_ref, cur_ref, next_ref, w_ref, bdw_ref, g_ref, b_ref, o_ref, win_ref, ph_ref):
    i = pl.program_id(1)
    rt = cur_ref.shape[0]
    zeros = jnp.zeros((CONV_HALO, C_CONV), F32)
    win_ref[0:CONV_HALO, :] = jnp.where(i > 0, prev_ref[...], zeros)
    win_ref[CONV_HALO:CONV_HALO + rt, :] = cur_ref[...]
    win_ref[CONV_HALO + rt:, :] = jnp.where(i < pl.num_programs(1) - 1, next_ref[...], zeros)
    n_ph = ph_ref.shape[1]
    for p in range(1, SUBLANES):
        ph_ref[p - 1] = win_ref[p:p + n_ph, :]
    first = CONV_HALO - CONV_K // 2
    tiles = CONV_ROWS // SUBLANES
    for r0 in range(0, rt, CONV_ROWS):
        acc = jnp.zeros((tiles, SUBLANES, C_CONV), F32) + bdw_ref[...]
        for t in range(CONV_K):
            p = (first + t) % SUBLANES
            base = r0 + first + t - p
            x = win_ref[base:base + CONV_ROWS, :] if p == 0 else ph_ref[p - 1, base:base + CONV_ROWS, :]
            acc = acc + x.reshape(tiles, SUBLANES, C_CONV) * w_ref[t]
        acc = acc.reshape(CONV_ROWS, C_CONV)
        mu = jnp.mean(acc, axis=-1, keepdims=True)
        cen = acc - mu
        var = jnp.mean(cen * cen, axis=-1, keepdims=True)
        y = cen * lax.rsqrt(var + NORM_EPS) * g_ref[...] + b_ref[...]
        o_ref[r0:r0 + CONV_ROWS, :] = _silu(y).astype(o_ref.dtype)


def _conv_module(u, w_dw, b_dw, g, b, *, rt):
    B, T, C = u.shape
    hb = rt // CONV_HALO
    last = T // CONV_HALO - 1
    n_ph = rt + (CONV_HALO + CONV_K // 2) // SUBLANES * SUBLANES
    w_tiles = jnp.broadcast_to(w_dw[:, None, :], (CONV_K, SUBLANES, C))
    in_specs = [
        pl.BlockSpec((None, CONV_HALO, C), lambda bb, i: (bb, jnp.maximum(i * hb - 1, 0), 0)),
        pl.BlockSpec((None, rt, C), lambda bb, i: (bb, i, 0)),
        pl.BlockSpec((None, CONV_HALO, C), lambda bb, i: (bb, jnp.minimum((i + 1) * hb, last), 0)),
        _resident(w_tiles.shape), _resident(b_dw.shape), _resident(g.shape), _resident(b.shape),
    ]
    return pl.pallas_call(
        _conv_kernel,
        grid=(B, T // rt),
        in_specs=in_specs,
        out_specs=pl.BlockSpec((None, rt, C), lambda bb, i: (bb, i, 0)),
        out_shape=jax.ShapeDtypeStruct((B, T, C), BF16),
        scratch_shapes=[pltpu.VMEM((rt + 2 * CONV_HALO, C), F32),
                        pltpu.VMEM((SUBLANES - 1, n_ph, C), F32)],
        compiler_params=_params("parallel", "parallel"),
        name="conv_module",
    )(u, u, u, w_tiles, b_dw, g, b)


def _merge_kernel(h_ref, mod_ref, g_ref, ya_ref, yb_ref, yc_ref, wg_ref, wb_ref, wo_ref, win_ref, wout_ref,
                  *rest, final):
    if final:
        fg_ref, o_ref = rest
    else:
        (o_ref,) = rest
    h = h_ref[...]
    nb = _mod_rms(h, g_ref[1:2, :], mod_ref[3:4, :], mod_ref[4:5, :]).astype(BF16)
    merged = None
    for i, y_ref in enumerate((ya_ref, yb_ref, yc_ref)):
        gate = jax.nn.sigmoid(_dot(nb, wg_ref[:, i * D_MODEL:(i + 1) * D_MODEL]))
        term = gate * _dot(y_ref[...], wb_ref[i])
        merged = term if merged is None else merged + term
    h = h + mod_ref[5:6, :] * _dot(merged.astype(BF16), wo_ref[...])
    nb = _mod_rms(h, g_ref[2:3, :], mod_ref[6:7, :], mod_ref[7:8, :]).astype(BF16)
    acc = None
    for lo in range(0, D_FF, FF_CHUNK):
        a = _dot(nb, win_ref[:, lo:lo + FF_CHUNK])
        b = _dot(nb, win_ref[:, D_FF + lo:D_FF + lo + FF_CHUNK])
        hid = (_silu(a) * b).astype(BF16)
        part = _dot(hid, wout_ref[lo:lo + FF_CHUNK, :])
        acc = part if acc is None else acc + part
    h = h + 0.5 * mod_ref[8:9, :] * acc
    if final:
        h = _rms(h, NORM_EPS) * fg_ref[...]
    o_ref[...] = h


def _merge(h, mod, norm_g, w_gate, ya, yb, yc, w_branch, w_out, ffn_in, ffn_out, layer, *, tm, final_g=None):
    B, T, D = h.shape
    per_batch = mod.shape[0] > 1
    final = final_g is not None
    row = lambda b, i: (b, i, 0)
    y_spec = pl.BlockSpec((None, tm, BRANCH_W), row)
    in_specs = [
        pl.BlockSpec((None, tm, D), row),
        pl.BlockSpec((None, 9, D), (lambda b, i: (b, 0, 0)) if per_batch else (lambda b, i: (0, 0, 0))),
        _resident(norm_g.shape), y_spec, y_spec, y_spec,
        _resident_slice(w_gate.shape, (layer,)), _resident_slice(w_branch.shape, (layer,)),
        _resident_slice(w_out.shape, (layer,)),
        _resident_slice(ffn_in.shape, (layer, 1)), _resident_slice(ffn_out.shape, (layer, 1)),
    ]
    args = [h, mod, norm_g, ya, yb, yc, w_gate, w_branch, w_out, ffn_in, ffn_out]
    if final:
        in_specs.append(_resident(final_g.shape))
        args.append(final_g)
    return pl.pallas_call(
        functools.partial(_merge_kernel, final=final),
        grid=(B, T // tm),
        in_specs=in_specs,
        out_specs=pl.BlockSpec((None, tm, D), row),
        out_shape=jax.ShapeDtypeStruct((B, T, D), F32),
        compiler_params=_params("parallel", "parallel"),
        name="merge",
    )(*args)


def _rope_tables(n_tokens):
    quarter = DH_A // 4
    t = np.arange(n_tokens)
    inv = (ROPE_BASE ** (-np.arange(0, DH_A // 2, 2, dtype=np.float32) / (DH_A // 2))).astype(np.float32)
    pos = np.stack([(t // GRID_W).astype(np.float32), (t % GRID_W).astype(np.float32)], axis=1)
    ang = (pos[:, :, None] * inv[None, None, :]).astype(np.float32)
    ang = np.broadcast_to(ang[:, :, None, :], (n_tokens, 2, 2, quarter)).reshape(n_tokens, DH_A)
    sign = np.where((np.arange(DH_A) % (2 * quarter)) < quarter, -1.0, 1.0).astype(np.float32)
    cos = np.cos(ang).astype(np.float32)
    sin_signed = (np.sin(ang) * sign).astype(np.float32)
    reps = LANES // DH_A
    return jnp.asarray(np.tile(cos, (1, reps))), jnp.asarray(np.tile(sin_signed, (1, reps)))


def _group_major(w, axis):
    shape = w.shape
    split = shape[:axis] + (KV_B, G_B, DH_B) + shape[axis + 1:]
    return jnp.swapaxes(w.reshape(split), axis, axis + 1).reshape(shape)


def _prepare_weights(ffn_w_in, ffn_w_out, w_in, w_branch, w_out):
    qb_lo = 3 * N_QA
    qb_hi = qb_lo + H_B * DH_B
    w_proj = jnp.concatenate(
        [w_in[:, :, :qb_lo], _group_major(w_in[:, :, qb_lo:qb_hi], 2), w_in[:, :, qb_hi:N_PROJ]],
        axis=2).astype(BF16)
    w_gate = w_in[:, :, N_PROJ:].astype(BF16)
    wb = jnp.concatenate(
        [w_branch[:, 0:1], _group_major(w_branch[:, 1:2], 2), w_branch[:, 2:3]], axis=1).astype(BF16)
    return ffn_w_in.astype(BF16), ffn_w_out.astype(BF16), w_proj, w_gate, wb, w_out.astype(BF16)


def kernel(x_prompt, x_sample, cache_diff_k, cache_diff_v, cache_win_k, cache_win_v, c, c_ctx,
           w_mod, b_mod, norm_g, ffn_w_in, ffn_w_out, w_in, diff_lambda, diff_subln_g, win_sink,
           conv_dw_w, conv_dw_b, conv_norm_g, conv_norm_b, w_branch, w_out, final_norm_g):
    n_ctx_b, n_ctx_t, D = x_prompt.shape
    n_lat_b, n_lat_t, _ = x_sample.shape
    past = cache_diff_k.shape[2]
    tm = 512

    n_rows = 1 + n_lat_b
    pad_rows = -n_rows % 8
    cond = jnp.concatenate([c_ctx[None, :], c, jnp.zeros((pad_rows, D), F32)], axis=0)
    mod_all = _modulation(cond, w_mod, b_mod.reshape(DEPTH, 1, 9 * D))

    rope_tables = _rope_tables(n_lat_t)
    ck_a = jnp.transpose(cache_diff_k, (0, 1, 3, 4, 5, 2)).reshape(n_lat_b, DEPTH, H_A, 2 * DH_A, past)
    cv_a = cache_diff_v.reshape(n_lat_b, DEPTH, past * H_A, 2 * DH_A)
    ck_b = jnp.transpose(cache_win_k, (0, 1, 3, 4, 2)).reshape(n_lat_b, DEPTH, N_KVB, past)
    cv_b = jnp.transpose(cache_win_v, (0, 1, 3, 4, 2)).reshape(n_lat_b, DEPTH, N_KVB, past)
    final_g = final_norm_g.reshape(1, D)

    xp = x_prompt.reshape(1, n_ctx_b * n_ctx_t, D)
    xs = x_sample
    new_cache = None
    ffn_in, ffn_out, w_proj, w_gate, wb, wo = _prepare_weights(ffn_w_in, ffn_w_out, w_in, w_branch, w_out)
    for l in range(DEPTH):
        lam_init = 0.8 - 0.6 * math.exp(-0.3 * l)
        g_l = norm_g[l]
        lam = diff_lambda[l]
        subln = diff_subln_g[l].reshape(1, 2 * DH_A)
        sink = win_sink[l]
        conv_p = (conv_dw_w[l], conv_dw_b[l].reshape(1, C_CONV),
                  conv_norm_g[l].reshape(1, C_CONV), conv_norm_b[l].reshape(1, C_CONV))
        mod_ctx = mod_all[l, 0:1].reshape(1, 9, D)
        mod_lat = mod_all[l, 1:n_rows].reshape(n_lat_b, 9, D)
        last = l == DEPTH - 1

        h = _ffn(xp, mod_ctx, g_l, ffn_in, ffn_out, (l, 0), mod_row=0, g_row=0, tm=tm)
        qa, ka, va, qb, kb, vb, u, *new_cache = _inproj(
            h, mod_ctx, g_l, w_proj, l, tm=tm, kv_dtype=F32,
            cache_seq=(n_ctx_b, n_ctx_t), cache_acc=new_cache)
        seq = lambda a: a.reshape(n_ctx_b, n_ctx_t, a.shape[-1])
        ka, va, kb, vb = seq(ka), seq(va), seq(kb), seq(vb)
        ya = _diff_attn(seq(qa), ka, va, lam, subln, lam_init=lam_init, tq=n_ctx_t, hb=H_A)
        yb = _win_attn(seq(qb), kb, vb, sink, tq=n_ctx_t, nq=1, nb=4, local=False)
        yc = _conv_module(seq(u), *conv_p, rt=n_ctx_t)
        flat = lambda a: a.reshape(1, n_ctx_b * n_ctx_t, a.shape[-1])
        xp = _merge(h, mod_ctx, g_l, w_gate, flat(ya), flat(yb), flat(yc), wb, wo, ffn_in, ffn_out, l, tm=tm,
                    final_g=final_g if last else None)

        h = _ffn(xs, mod_lat, g_l, ffn_in, ffn_out, (l, 0), mod_row=0, g_row=0, tm=tm)
        qa, ka, va, qb, kb, vb, u = _inproj(h, mod_lat, g_l, w_proj, l, tm=tm, kv_dtype=BF16,
                                            rope_tables=rope_tables)
        ya = _diff_attn(qa, ka, va, lam, subln, lam_init=lam_init, tq=1024, hb=1, cache=(ck_a, cv_a, l))
        yb = _win_attn(qb, kb, vb, sink, tq=WIN_BLOCK, nq=16, nb=1, local=True, cache=(ck_b, cv_b, l))
        yc = _conv_module(u, *conv_p, rt=1024)
        xs = _merge(h, mod_lat, g_l, w_gate, ya, yb, yc, wb, wo, ffn_in, ffn_out, l, tm=tm,
                    final_g=final_g if last else None)

    y_prompt = xp.reshape(n_ctx_b, n_ctx_t, D)
    dk, dv, wk, wv = new_cache
    new_diff_k = jnp.transpose(dk.reshape(n_ctx_b, DEPTH, H_A, 2, DH_A, n_ctx_t), (0, 1, 5, 2, 3, 4))
    new_diff_v = dv.reshape(n_ctx_b, DEPTH, n_ctx_t, H_A, 2 * DH_A)
    new_win_k = jnp.transpose(wk.reshape(n_ctx_b, DEPTH, KV_B, DH_B, n_ctx_t), (0, 1, 4, 2, 3))
    new_win_v = jnp.transpose(wv.reshape(n_ctx_b, DEPTH, KV_B, DH_B, n_ctx_t), (0, 1, 4, 2, 3))
    return (y_prompt, xs, new_diff_k, new_diff_v, new_win_k, new_win_v)
```
